```python
import math
import jax, jax.numpy as jnp
from jax import lax
import numpy as np

D_MODEL = 1024
BATCH = 8
SEQ = 2048
DEPTH = 2
DEC_BATCH = 32
DEC_SEQ = 8
PAST_LEN = 8192
PAGE_SIZE = 128

NSA_HEADS = 8
NSA_KV_HEADS = 2
NSA_GROUP = NSA_HEADS // NSA_KV_HEADS
HEAD_DIM = 64
CMP_BLOCK = 32
SEL_BLOCK = 64
TOP_K = 16
WINDOW = 512
WIN_QBLOCK = 128
SEL_QBLOCK = 64
GLA_HEADS = 4
GLA_DK = 64
GLA_DV = 128
GLA_RANK = 16
GLA_TAU = 16.0
GLA_CHUNK = 64
MIX_WIDTH = NSA_HEADS * HEAD_DIM + GLA_HEADS * GLA_DV
D_FF = -(-(8 * D_MODEL) // (3 * 256)) * 256
KV_WIDTH = NSA_KV_HEADS * HEAD_DIM
IN_WIDTHS = (NSA_HEADS * HEAD_DIM, 6 * KV_WIDTH, 3 * NSA_HEADS, GLA_HEADS * GLA_DK, GLA_HEADS * GLA_DK, GLA_HEADS * GLA_DV, GLA_HEADS * GLA_DV, GLA_RANK)
IN_WIDTH = sum(IN_WIDTHS)
SPLIT_POINTS = tuple(int(v) for v in np.cumsum(IN_WIDTHS)[:-1])
EPS = 1e-6
NEG_INF = -1e30
FORCE_SCORE = 1e4
SCALE = HEAD_DIM ** -0.5

kernel_name = 'hybrid_nsa_gla_decode_step'


def rms_norm(x, g):
    xf = x.astype(jnp.float32)
    y = xf * lax.rsqrt(jnp.mean(xf * xf, axis=-1, keepdims=True) + EPS)
    return (y * g.astype(jnp.float32)).astype(x.dtype)


def alibi_slopes():
    h = jnp.arange(1, NSA_HEADS + 1, dtype=jnp.float32)
    return (2.0 ** (-8.0 * h / NSA_HEADS)).reshape(NSA_KV_HEADS, NSA_GROUP)


def compressed_branch(q, q_pos, k_c, v_c, w_pos, slopes):
    B, L = k_c.shape[:2]
    n_cmp = L // CMP_BLOCK
    kb = k_c[:, :n_cmp * CMP_BLOCK].reshape(B, n_cmp, CMP_BLOCK, NSA_KV_HEADS, HEAD_DIM)
    vb = v_c[:, :n_cmp * CMP_BLOCK].reshape(B, n_cmp, CMP_BLOCK, NSA_KV_HEADS, HEAD_DIM)
    kc = jnp.einsum('bnckd,c->bnkd', kb, w_pos[0])
    vc = jnp.einsum('bnckd,c->bnkd', vb, w_pos[1])
    end_pos = (jnp.arange(n_cmp) + 1) * CMP_BLOCK - 1
    dist = (q_pos[:, None] - end_pos[None, :]).astype(jnp.float32)
    valid = (dist >= 0)[None, :, None, None, :]
    s = jnp.einsum('btkgd,bnkd->btkgn', q, kc).astype(jnp.float32) * SCALE
    s = s - slopes[None, None, :, :, None] * dist[None, :, None, None, :]
    p = jax.nn.softmax(jnp.where(valid, s, NEG_INF), axis=-1)
    p = jnp.where(valid, p, 0.0)
    o = jnp.einsum('btkgn,bnkd->btkgd', p.astype(vc.dtype), vc)
    return o, p


def select_blocks(p_cmp, q_pos, L):
    B, T = p_cmp.shape[:2]
    n_cmp = p_cmp.shape[-1]
    ratio = SEL_BLOCK // CMP_BLOCK
    n_sel = -(-L // SEL_BLOCK)
    imp = p_cmp.sum(axis=3)
    imp = jnp.pad(imp, ((0, 0), (0, 0), (0, 0), (0, n_sel * ratio - n_cmp)))
    imp = imp.reshape(B, T, NSA_KV_HEADS, n_sel, ratio).sum(-1)
    blk = jnp.arange(n_sel)[None, :]
    cur = (q_pos // SEL_BLOCK)[:, None]
    forced = (blk == 0) | (blk == cur) | (blk == cur - 1)
    valid = blk <= cur
    score = jnp.where(forced[None, :, None, :], FORCE_SCORE, jnp.where(valid[None, :, None, :], imp, NEG_INF))
    _, idx = lax.top_k(score, min(TOP_K, n_sel))
    return idx


def selected_branch(q, q_pos, idx, k_s, v_s, slopes):
    B, L = k_s.shape[:2]
    T = q.shape[1]
    n_sel = -(-L // SEL_BLOCK)
    padn = n_sel * SEL_BLOCK - L
    kt = jnp.pad(k_s, ((0, 0), (0, padn), (0, 0), (0, 0))).reshape(B, n_sel, SEL_BLOCK, NSA_KV_HEADS, HEAD_DIM).transpose(0, 3, 1, 2, 4)
    vt = jnp.pad(v_s, ((0, 0), (0, padn), (0, 0), (0, 0))).reshape(B, n_sel, SEL_BLOCK, NSA_KV_HEADS, HEAD_DIM).transpose(0, 3, 1, 2, 4)
    qb = math.gcd(T, SEL_QBLOCK)
    nb = T // qb
    n_k = idx.shape[-1]
    bi = jnp.arange(B)[:, None, None, None]
    hi = jnp.arange(NSA_KV_HEADS)[None, None, :, None]

    def one_block(args):
        qi, pi, ii = args
        ksel = kt[bi, hi, ii]
        vsel = vt[bi, hi, ii]
        kpos = ii[..., None] * SEL_BLOCK + jnp.arange(SEL_BLOCK)
        dist = (pi[None, :, None, None, None] - kpos).astype(jnp.float32)
        valid = (dist >= 0)[:, :, :, None]
        s = jnp.einsum('btkgd,btkjsd->btkgjs', qi, ksel).astype(jnp.float32) * SCALE
        s = s - slopes[None, None, :, :, None, None] * dist[:, :, :, None]
        s = jnp.where(valid, s, NEG_INF)
        p = jax.nn.softmax(s.reshape(B, qb, NSA_KV_HEADS, NSA_GROUP, n_k * SEL_BLOCK), axis=-1)
        p = p.reshape(B, qb, NSA_KV_HEADS, NSA_GROUP, n_k, SEL_BLOCK)
        return jnp.einsum('btkgjs,btkjsd->btkgd', p.astype(vsel.dtype), vsel)

    xs = (q.reshape((B, nb, qb) + q.shape[2:]).swapaxes(0, 1),
          q_pos.reshape(nb, qb),
          idx.reshape(B, nb, qb, NSA_KV_HEADS, n_k).swapaxes(0, 1))
    o = lax.map(one_block, xs)
    return o.swapaxes(0, 1).reshape(q.shape)


def window_attend(q, q_pos, k, v, k_pos, slopes):
    dist = q_pos[:, :, None] - k_pos[:, None, :]
    valid = ((dist >= 0) & (dist < WINDOW) & (k_pos[:, None, :] >= 0))[None, :, :, None, None, :]
    s = jnp.einsum('bntkgd,bnskd->bntkgs', q, k).astype(jnp.float32) * SCALE
    s = s - slopes[None, None, None, :, :, None] * dist.astype(jnp.float32)[None, :, :, None, None, :]
    p = jax.nn.softmax(jnp.where(valid, s, NEG_INF), axis=-1)
    return jnp.einsum('bntkgs,bnskd->bntkgd', p.astype(v.dtype), v)


def window_prompt(q, k_w, v_w, slopes):
    B, T = q.shape[:2]
    wb = math.gcd(T, WIN_QBLOCK)
    nb = T // wb
    kp = jnp.pad(k_w, ((0, 0), (WINDOW, 0), (0, 0), (0, 0)))
    vp = jnp.pad(v_w, ((0, 0), (WINDOW, 0), (0, 0), (0, 0)))
    idx = jnp.arange(nb)[:, None] * wb + jnp.arange(wb + WINDOW)[None, :]
    q_pos = jnp.arange(T).reshape(nb, wb)
    o = window_attend(q.reshape((B, nb, wb) + q.shape[2:]), q_pos, kp[:, idx], vp[:, idx], idx - WINDOW, slopes)
    return o.reshape(q.shape)


def gla_recurrence(q, k, v, log_a, s0):
    B, T = q.shape[:2]
    c = math.gcd(T, GLA_CHUNK)
    n = T // c
    causal = jnp.tril(jnp.ones((c, c), dtype=bool))[None, :, :, None, None]

    def to_chunks(a):
        return a.reshape((B, n, c) + a.shape[2:]).swapaxes(0, 1)

    def step(S, xs):
        qc, kc, vc, lac = xs
        cum = jnp.cumsum(lac, axis=1)
        inter = jnp.einsum('bthd,bhde->bthe', qc * jnp.exp(cum), S)
        diff = cum[:, :, None] - cum[:, None, :]
        decay = jnp.exp(jnp.where(causal, diff, NEG_INF))
        A = jnp.einsum('bthd,bshd,btshd->btsh', qc, kc, decay)
        intra = jnp.einsum('btsh,bshe->bthe', A, vc)
        last = cum[:, -1]
        S_new = jnp.exp(last)[..., None] * S + jnp.einsum('bshd,bshe->bhde', kc * jnp.exp(last[:, None] - cum), vc)
        return S_new, inter + intra

    S, o = lax.scan(step, s0, (to_chunks(q), to_chunks(k), to_chunks(v), to_chunks(log_a)))
    return o.swapaxes(0, 1).reshape(B, T, GLA_HEADS, GLA_DV), S


def trunk_layer(x, q_pos, past_kv, win_buf, gla_s0, ln_mix, w_in, q_norm, k_norm, cmp_pos_w, w_a2, b_a, gla_norm, w_out, ln_ffn, w_gate, w_up, w_down):
    B, T, _ = x.shape
    slopes = alibi_slopes()
    h = rms_norm(x, ln_mix)
    q, kv, gate, gq, gk, gv, gg, ga = jnp.split(h @ w_in, SPLIT_POINTS, axis=-1)
    q = rms_norm(q.reshape(B, T, NSA_KV_HEADS, NSA_GROUP, HEAD_DIM), q_norm)
    kv = kv.reshape(B, T, 6, NSA_KV_HEADS, HEAD_DIM)
    k_n = rms_norm(kv[:, :, 0::2], k_norm[:, None, :])
    kv = jnp.stack([k_n, kv[:, :, 1::2]], axis=3).reshape(B, T, 6, NSA_KV_HEADS, HEAD_DIM)
    new_rows = kv[:, :, :4]
    win_rows = kv[:, :, 4:]
    if past_kv is None:
        full = new_rows
        o_w = window_prompt(q, win_rows[:, :, 0], win_rows[:, :, 1], slopes)
        keep = min(WINDOW, T)
        new_win = win_rows[:, T - keep:]
    else:
        full = jnp.concatenate([past_kv, new_rows], axis=1)
        buf = jnp.concatenate([win_buf, win_rows], axis=1)
        n_buf = buf.shape[1]
        k_pos = past_kv.shape[1] - win_buf.shape[1] + jnp.arange(n_buf)
        o_w = window_attend(q[:, None], q_pos[None], buf[:, None, :, 0], buf[:, None, :, 1], k_pos[None], slopes)[:, 0]
        keep = min(WINDOW, n_buf)
        new_win = buf[:, n_buf - keep:]
    L = full.shape[1]
    o_c, p_c = compressed_branch(q, q_pos, full[:, :, 0], full[:, :, 1], cmp_pos_w, slopes)
    idx = select_blocks(p_c, q_pos, L)
    o_s = selected_branch(q, q_pos, idx, full[:, :, 2], full[:, :, 3], slopes)
    g = jax.nn.sigmoid(gate.reshape(B, T, 3, NSA_KV_HEADS, NSA_GROUP, 1))
    o_nsa = (g[:, :, 0] * o_c + g[:, :, 1] * o_s + g[:, :, 2] * o_w).reshape(B, T, NSA_HEADS * HEAD_DIM)
    gq = gq.reshape(B, T, GLA_HEADS, GLA_DK).astype(jnp.float32) * (GLA_DK ** -0.5)
    gk = gk.reshape(B, T, GLA_HEADS, GLA_DK).astype(jnp.float32)
    gv = gv.reshape(B, T, GLA_HEADS, GLA_DV).astype(jnp.float32)
    log_a = jax.nn.log_sigmoid((ga @ w_a2 + b_a).astype(jnp.float32)).reshape(B, T, GLA_HEADS, GLA_DK) / GLA_TAU
    o_g, s_new = gla_recurrence(gq, gk, gv, log_a, gla_s0.astype(jnp.float32))
    o_g = rms_norm(o_g, gla_norm) * jax.nn.silu(gg.reshape(B, T, GLA_HEADS, GLA_DV).astype(jnp.float32))
    o_g = o_g.reshape(B, T, GLA_HEADS * GLA_DV).astype(x.dtype)
    x = x + jnp.concatenate([o_nsa.astype(x.dtype), o_g], axis=-1) @ w_out
    h2 = rms_norm(x, ln_ffn)
    x = x + (jax.nn.silu(h2 @ w_gate) * (h2 @ w_up)) @ w_down
    return x, new_rows, new_win, s_new


def setup_inputs(seed: int = 0) -> dict:
    key = jax.random.key(seed)
    ks = jax.random.split(key, 20)
    f32 = jnp.float32
    n_pages = PAST_LEN // PAGE_SIZE
    n_used = DEC_BATCH * n_pages
    n_phys = n_used + n_used // 4
    win_buf = min(WINDOW, PAST_LEN)

    def nrm(k, shape, scale):
        return jax.random.normal(k, shape, f32) * scale

    return {
        'x_prompt': nrm(ks[0], (BATCH, SEQ, D_MODEL), 1.0),
        'x_sample': nrm(ks[1], (DEC_BATCH, DEC_SEQ, D_MODEL), 1.0),
        'cache_nsa_kv': nrm(ks[2], (DEPTH, n_phys, PAGE_SIZE, 4, NSA_KV_HEADS, HEAD_DIM), 1.0),
        'state_nsa_win': nrm(ks[3], (DEPTH, DEC_BATCH, win_buf, 2, NSA_KV_HEADS, HEAD_DIM), 1.0),
        'state_gla': nrm(ks[4], (DEPTH, DEC_BATCH, GLA_HEADS, GLA_DK, GLA_DV), 0.1),
        'page_table': jax.random.permutation(ks[5], n_phys)[:n_used].reshape(DEC_BATCH, n_pages).astype(jnp.int32),
        'ln_mix': 1.0 + nrm(ks[6], (DEPTH, D_MODEL), 0.02),
        'w_in': nrm(ks[7], (DEPTH, D_MODEL, IN_WIDTH), D_MODEL ** -0.5),
        'q_norm': 1.0 + nrm(ks[8], (DEPTH, HEAD_DIM), 0.02),
        'k_norm': 1.0 + nrm(ks[9], (DEPTH, 3, HEAD_DIM), 0.02),
        'cmp_pos_w': (1.0 + nrm(ks[10], (DEPTH, 2, CMP_BLOCK), 0.1)) / CMP_BLOCK,
        'w_a2': nrm(ks[11], (DEPTH, GLA_RANK, GLA_HEADS * GLA_DK), GLA_RANK ** -0.5),
        'b_a': nrm(ks[12], (DEPTH, GLA_HEADS * GLA_DK), 0.01),
        'gla_norm': 1.0 + nrm(ks[13], (DEPTH, GLA_DV), 0.02),
        'w_out': nrm(ks[14], (DEPTH, MIX_WIDTH, D_MODEL), MIX_WIDTH ** -0.5),
        'ln_ffn': 1.0 + nrm(ks[15], (DEPTH, D_MODEL), 0.02),
        'w_gate': nrm(ks[16], (DEPTH, D_MODEL, D_FF), D_MODEL ** -0.5),
        'w_up': nrm(ks[17], (DEPTH, D_MODEL, D_FF), D_MODEL ** -0.5),
        'w_down': nrm(ks[18], (DEPTH, D_FF, D_MODEL), D_FF ** -0.5),
    }


def reference(x_prompt, x_sample, cache_nsa_kv, state_nsa_win, state_gla, page_table, ln_mix, w_in, q_norm, k_norm, cmp_pos_w, w_a2, b_a, gla_norm, w_out, ln_ffn, w_gate, w_up, w_down):
    B, T = x_prompt.shape[:2]
    DB, TS = x_sample.shape[:2]
    n_pages = page_table.shape[1]
    past_len = n_pages * cache_nsa_kv.shape[2]
    pos_p = jnp.arange(T)
    pos_s = past_len + jnp.arange(TS)
    s0_p = jnp.zeros((B, GLA_HEADS, GLA_DK, GLA_DV), jnp.float32)
    yp, ys = x_prompt, x_sample
    rows_p, win_p, gla_p, rows_s, win_s, gla_s = [], [], [], [], [], []
    for l in range(DEPTH):
        w = (ln_mix[l], w_in[l], q_norm[l], k_norm[l], cmp_pos_w[l], w_a2[l], b_a[l], gla_norm[l], w_out[l], ln_ffn[l], w_gate[l], w_up[l], w_down[l])
        yp, r_p, nw_p, st_p = trunk_layer(yp, pos_p, None, None, s0_p, *w)
        past = cache_nsa_kv[l][page_table].reshape(DB, past_len, 4, NSA_KV_HEADS, HEAD_DIM)
        ys, r_s, nw_s, st_s = trunk_layer(ys, pos_s, past, state_nsa_win[l], state_gla[l], *w)
        rows_p.append(r_p)
        win_p.append(nw_p)
        gla_p.append(st_p.astype(state_gla.dtype))
        rows_s.append(r_s)
        win_s.append(nw_s)
        gla_s.append(st_s.astype(state_gla.dtype))
    return (yp, ys, jnp.stack(rows_p), jnp.stack(win_p), jnp.stack(gla_p), jnp.stack(rows_s), jnp.stack(win_s), jnp.stack(gla_s))
```

```python
import functools
import math

import jax
import jax.numpy as jnp
from jax import lax
from jax.experimental import pallas as pl
from jax.experimental.pallas import tpu as pltpu

F32 = jnp.float32
BF16 = jnp.bfloat16

D_MODEL = 1024
NSA_HEADS = 8
NSA_KV_HEADS = 2
NSA_GROUP = NSA_HEADS // NSA_KV_HEADS
HEAD_DIM = 64
CMP_BLOCK = 32
SEL_BLOCK = 64
TOP_K = 16
WINDOW = 512
GLA_HEADS = 4
GLA_DK = 64
GLA_DV = 128
GLA_RANK = 16
GLA_TAU = 16.0
GLA_CHUNK = 64
D_FF = -(-(8 * D_MODEL) // (3 * 256)) * 256
KV_WIDTH = NSA_KV_HEADS * HEAD_DIM
Q_WIDTH = NSA_HEADS * HEAD_DIM
N_GATES = 3 * NSA_HEADS
GLA_QK_WIDTH = GLA_HEADS * GLA_DK
GLA_V_WIDTH = GLA_HEADS * GLA_DV
EPS = 1e-6
NEG_INF = -1e30
FORCE_SCORE = 1e4
BELOW_ALL = -3e38
SCALE = HEAD_DIM ** -0.5
SLOPES = tuple(tuple(2.0 ** (-8.0 * (k * NSA_GROUP + g + 1) / NSA_HEADS) for g in range(NSA_GROUP))
               for k in range(NSA_KV_HEADS))

LANES = 128
SUBLANES = 8
VMEM_LIMIT = 56 * 1024 * 1024

G_WIDTH = 2 * GLA_QK_WIDTH + 2 * GLA_V_WIDTH
MISC_WIDTH = LANES
KV6_WIDTH = 6 * KV_WIDTH


def _dot(a, b):
    return jnp.dot(a, b, preferred_element_type=F32)


def _dot_nt(a, b):
    return lax.dot_general(a, b, (((1,), (1,)), ((), ())), preferred_element_type=F32)


def _dot_tn(a, b):
    return lax.dot_general(a, b, (((0,), (0,)), ((), ())), preferred_element_type=F32)


def _split3(a):
    a0 = a.astype(BF16)
    r = a - a0.astype(F32)
    a1 = r.astype(BF16)
    a2 = (r - a1.astype(F32)).astype(BF16)
    return a0, a1, a2


def _dot_f32(a, b):
    a0, a1, a2 = _split3(a)
    b0, b1, b2 = _split3(b)
    return (_dot(a0, b0) + (_dot(a0, b1) + _dot(a1, b0)) + (_dot(a0, b2) + _dot(a1, b1) + _dot(a2, b0)))


def _sigmoid(x):
    return 1.0 / (1.0 + jnp.exp(-x))


def _half_lane_rms(x, gain):
    x2 = x * x
    lo = lax.broadcasted_iota(jnp.int32, x.shape, 1) < HEAD_DIM
    s_lo = jnp.sum(jnp.where(lo, x2, 0.0), axis=-1, keepdims=True)
    s_hi = jnp.sum(jnp.where(lo, 0.0, x2), axis=-1, keepdims=True)
    ms = jnp.where(lo, s_lo, s_hi) * (1.0 / HEAD_DIM)
    return x * lax.rsqrt(ms + EPS) * gain


def _proj_in_kernel(x_ref, ln_ref, wq_ref, wkv_ref, wg_ref, wm_ref, qg_ref, kg_ref, wa_ref, ba_ref,
                    q_out, kv_out, g_out, gate_out, la_out):
    x = x_ref[...]
    h = x * lax.rsqrt(jnp.mean(x * x, axis=-1, keepdims=True) + EPS) * ln_ref[...]
    hb = h.astype(BF16)

    q = _dot(hb, wq_ref[...])
    for j in range(Q_WIDTH // LANES):
        sl = slice(j * LANES, (j + 1) * LANES)
        q_out[:, sl] = _half_lane_rms(q[:, sl], qg_ref[...])

    kv = _dot(hb, wkv_ref[...])
    for s in range(6):
        sl = slice(s * KV_WIDTH, (s + 1) * KV_WIDTH)
        if s % 2 == 0:
            kv_out[:, sl] = _half_lane_rms(kv[:, sl], kg_ref[s // 2:s // 2 + 1, :])
        else:
            kv_out[:, sl] = kv[:, sl]

    g_out[...] = _dot(hb, wg_ref[...])

    m = _dot(hb, wm_ref[...])
    gate_out[...] = _sigmoid(m)
    z = _dot_f32(m, wa_ref[...]) + ba_ref[...]
    la_out[...] = (jnp.minimum(z, 0.0) - jnp.log1p(jnp.exp(-jnp.abs(z)))) * (1.0 / GLA_TAU)


def _proj_in(x, ln, wq, wkv, wg, wm, qg, kg, wa, ba, tm):
    n = x.shape[0]
    row = lambda w: pl.BlockSpec((tm, w), lambda i: (i, 0))
    full = lambda a: pl.BlockSpec(a.shape, lambda i: (0,) * a.ndim)
    return pl.pallas_call(
        _proj_in_kernel,
        grid=(n // tm,),
        in_specs=[row(D_MODEL), full(ln), full(wq), full(wkv), full(wg), full(wm), full(qg), full(kg), full(wa), full(ba)],
        out_specs=[row(Q_WIDTH), row(KV6_WIDTH), row(G_WIDTH), row(MISC_WIDTH), row(GLA_QK_WIDTH)],
        out_shape=[jax.ShapeDtypeStruct((n, w), F32) for w in (Q_WIDTH, KV6_WIDTH, G_WIDTH, MISC_WIDTH, GLA_QK_WIDTH)],
        compiler_params=pltpu.CompilerParams(dimension_semantics=("arbitrary",), vmem_limit_bytes=VMEM_LIMIT),
        name="proj_in",
    )(x, ln, wq, wkv, wg, wm, qg, kg, wa, ba)


def _stack_heads(q_tile, kvh):
    return jnp.concatenate([q_tile[:, (kvh * NSA_GROUP + g) * HEAD_DIM:(kvh * NSA_GROUP + g + 1) * HEAD_DIM]
                            for g in range(NSA_GROUP)], axis=0).astype(BF16)


def _attend(qs, k, v, dist, ok, kvh, tq):
    s_all = _dot_nt(qs, k.astype(BF16)) * SCALE
    ps, ls = [], []
    for g in range(NSA_GROUP):
        s = s_all[g * tq:(g + 1) * tq] - SLOPES[kvh][g] * dist
        s = jnp.where(ok, s, NEG_INF)
        p = jnp.exp(s - jnp.max(s, axis=-1, keepdims=True))
        ls.append(jnp.sum(p, axis=-1, keepdims=True))
        ps.append(p.astype(BF16))
    o = _dot(jnp.concatenate(ps, axis=0), v.astype(BF16))
    return o / jnp.concatenate(ls, axis=0)


def _compress_and_select(qs, kc, vc, qpos, kvh, tq, n_cmp, n_sel):
    w = kc.shape[0]
    lane = lax.broadcasted_iota(jnp.int32, (1, w), 1)
    dist = (qpos - ((lane + 1) * CMP_BLOCK - 1)).astype(F32)
    valid = (dist >= 0) & (lane < n_cmp)
    s_all = _dot_nt(qs, kc.astype(BF16)) * SCALE
    ps = []
    imp = jnp.zeros((tq, w), F32)
    for g in range(NSA_GROUP):
        s = s_all[g * tq:(g + 1) * tq] - SLOPES[kvh][g] * dist
        s = jnp.where(valid, s, NEG_INF)
        e = jnp.exp(s - jnp.max(s, axis=-1, keepdims=True))
        p = jnp.where(valid, e / jnp.sum(e, axis=-1, keepdims=True), 0.0)
        imp = imp + p
        ps.append(p.astype(BF16))
    o_c = _dot(jnp.concatenate(ps, axis=0), vc.astype(BF16))

    pair = imp + pltpu.roll(imp, w - 1, 1)
    blk = lane >> 1
    cur = qpos >> int(math.log2(SEL_BLOCK))
    is_blk = ((lane & 1) == 0) & (blk < n_sel)
    forced = (blk == 0) | (blk == cur) | (blk == cur - 1)
    score = jnp.where(forced, FORCE_SCORE, jnp.where(blk <= cur, pair, NEG_INF))
    score = jnp.where(is_blk, score, BELOW_ALL)
    rank = jnp.zeros((tq, w), jnp.int32)
    for i in range(n_sel):
        col = score[:, 2 * i:2 * i + 1]
        beats = (col > score) | ((col == score) & (lane > 2 * i))
        rank = rank + jnp.where(beats, 1, 0)
    sel = jnp.where((rank < min(TOP_K, n_sel)) & is_blk, 1.0, 0.0)
    return o_c, sel


def _selected_mask(sel, n_keys):
    w = sel.shape[1]
    li = lax.broadcasted_iota(jnp.int32, (w, 1), 0)
    ti = lax.broadcasted_iota(jnp.int32, (1, n_keys), 1)
    expand = jnp.where(((ti >> int(math.log2(SEL_BLOCK))) << 1) == li, 1.0, 0.0).astype(BF16)
    return _dot(sel.astype(BF16), expand) > 0.5


def _merge_heads(o_ref, gates, o_c, o_s, o_w, kvh, tq):
    for g in range(NSA_GROUP):
        h = kvh * NSA_GROUP + g
        rows = slice(g * tq, (g + 1) * tq)
        o = (gates[:, h:h + 1] * o_c[rows] + gates[:, NSA_HEADS + h:NSA_HEADS + h + 1] * o_s[rows]
             + gates[:, 2 * NSA_HEADS + h:2 * NSA_HEADS + h + 1] * o_w[rows])
        o_ref[:, h * HEAD_DIM:(h + 1) * HEAD_DIM] = o


def _block_summaries(rows, w_rows):
    n = rows.shape[0] // CMP_BLOCK
    return jnp.sum(rows.reshape(n, CMP_BLOCK, rows.shape[1]) * w_rows[None], axis=1)


def _nsa_prompt_kernel(q_ref, kv_ref, gate_ref, wpos_ref, o_ref, kc_ref, vc_ref, *, t_len, tq, key_step):
    qi = pl.program_id(1)
    n_cmp = t_len // CMP_BLOCK
    n_sel = -(-t_len // SEL_BLOCK)
    w = kc_ref.shape[0]

    @pl.when(qi == 0)
    def _():
        n_full = n_cmp * CMP_BLOCK
        kc_ref[0:n_cmp, :] = _block_summaries(kv_ref[0:n_full, 0:KV_WIDTH], wpos_ref[0])
        vc_ref[0:n_cmp, :] = _block_summaries(kv_ref[0:n_full, KV_WIDTH:2 * KV_WIDTH], wpos_ref[1])
        if w > n_cmp:
            kc_ref[n_cmp:w, :] = jnp.zeros((w - n_cmp, KV_WIDTH), F32)
            vc_ref[n_cmp:w, :] = jnp.zeros((w - n_cmp, KV_WIDTH), F32)

    q0 = qi * tq
    qpos = q0 + lax.broadcasted_iota(jnp.int32, (tq, 1), 0)
    n_win = min(WINDOW + tq, t_len)
    w_start = pl.multiple_of(jnp.maximum(q0 + tq - n_win, 0), SUBLANES)

    def body(n_keys):
        q_tile = q_ref[...]
        gates = gate_ref[...]
        kpos = lax.broadcasted_iota(jnp.int32, (1, n_keys), 1)
        dist_s = qpos - kpos
        wpos = w_start + lax.broadcasted_iota(jnp.int32, (1, n_win), 1)
        dist_w = qpos - wpos
        ok_w = (dist_w >= 0) & (dist_w < WINDOW)
        for kvh in range(NSA_KV_HEADS):
            hs = slice(kvh * HEAD_DIM, (kvh + 1) * HEAD_DIM)
            col = lambda stream: slice(stream * KV_WIDTH + kvh * HEAD_DIM, stream * KV_WIDTH + (kvh + 1) * HEAD_DIM)
            qs = _stack_heads(q_tile, kvh)
            o_c, sel = _compress_and_select(qs, kc_ref[:, hs], vc_ref[:, hs], qpos, kvh, tq, n_cmp, n_sel)
            ok_s = _selected_mask(sel, n_keys) & (dist_s >= 0)
            o_s = _attend(qs, kv_ref[0:n_keys, col(2)], kv_ref[0:n_keys, col(3)], dist_s.astype(F32), ok_s, kvh, tq)
            o_w = _attend(qs, kv_ref[pl.ds(w_start, n_win), col(4)], kv_ref[pl.ds(w_start, n_win), col(5)],
                          dist_w.astype(F32), ok_w, kvh, tq)
            _merge_heads(o_ref, gates, o_c, o_s, o_w, kvh, tq)

    n_classes = -(-t_len // key_step)
    for c in range(n_classes):
        n_keys = min((c + 1) * key_step, t_len)

        @pl.when((q0 + tq - 1) // key_step == c)
        def _(n_keys=n_keys):
            body(n_keys)


def _nsa_prompt(qn, kvn, gates, wpos_rows, bsz, t_len):
    tq = math.gcd(t_len, 128)
    key_step = math.gcd(t_len, 512)
    n_cmp = t_len // CMP_BLOCK
    w = -(-max(n_cmp, 2 * (-(-t_len // SEL_BLOCK))) // LANES) * LANES
    kern = functools.partial(_nsa_prompt_kernel, t_len=t_len, tq=tq, key_step=key_step)
    return pl.pallas_call(
        kern,
        grid=(bsz, t_len // tq),
        in_specs=[
            pl.BlockSpec((None, tq, Q_WIDTH), lambda b, i: (b, i, 0)),
            pl.BlockSpec((None, t_len, KV6_WIDTH), lambda b, i: (b, 0, 0)),
            pl.BlockSpec((None, tq, MISC_WIDTH), lambda b, i: (b, i, 0)),
            pl.BlockSpec(wpos_rows.shape, lambda b, i: (0, 0, 0)),
        ],
        out_specs=pl.BlockSpec((None, tq, Q_WIDTH), lambda b, i: (b, i, 0)),
        out_shape=jax.ShapeDtypeStruct((bsz, t_len, Q_WIDTH), F32),
        scratch_shapes=[pltpu.VMEM((w, KV_WIDTH), F32), pltpu.VMEM((w, KV_WIDTH), F32)],
        compiler_params=pltpu.CompilerParams(dimension_semantics=("arbitrary", "arbitrary"), vmem_limit_bytes=VMEM_LIMIT),
        name="nsa_prompt",
    )(qn.reshape(bsz, t_len, Q_WIDTH), kvn.reshape(bsz, t_len, KV6_WIDTH), gates.reshape(bsz, t_len, MISC_WIDTH), wpos_rows)


def _nsa_decode_kernel(pt_ref, q_ref, kv_ref, gate_ref, win_ref, wpos_ref, *rest, ts, past_len, page_size, pages_per_step):
    page_refs = rest[:pages_per_step]
    o_ref, win_out_ref, kc_ref, vc_ref, slc_ref = rest[pages_per_step:]
    del pt_ref
    i = pl.program_id(1)
    n_steps = pl.num_programs(1)
    blocks_per_page = page_size // CMP_BLOCK
    n_cmp = past_len // CMP_BLOCK
    n_sel = past_len // SEL_BLOCK + 1
    w = kc_ref.shape[0]
    n_keys = slc_ref.shape[0]

    @pl.when(i == 0)
    def _():
        kc_ref[n_cmp:w, :] = jnp.zeros((w - n_cmp, KV_WIDTH), F32)
        vc_ref[n_cmp:w, :] = jnp.zeros((w - n_cmp, KV_WIDTH), F32)
        slc_ref[past_len:n_keys, :] = jnp.zeros((n_keys - past_len, 2 * KV_WIDTH), F32)

    kcs, vcs = [], []
    for j, page in enumerate(page_refs):
        kcs.append(_block_summaries(page[:, 0:KV_WIDTH], wpos_ref[0]))
        vcs.append(_block_summaries(page[:, KV_WIDTH:2 * KV_WIDTH], wpos_ref[1]))
        row0 = pl.multiple_of((i * pages_per_step + j) * page_size, page_size)
        slc_ref[pl.ds(row0, page_size), :] = page[:, 2 * KV_WIDTH:4 * KV_WIDTH]
    blk0 = pl.multiple_of(i * pages_per_step * blocks_per_page, SUBLANES)
    kc_ref[pl.ds(blk0, pages_per_step * blocks_per_page), :] = jnp.concatenate(kcs, axis=0)
    vc_ref[pl.ds(blk0, pages_per_step * blocks_per_page), :] = jnp.concatenate(vcs, axis=0)

    @pl.when(i == n_steps - 1)
    def _():
        new = kv_ref[...]
        slc_ref[past_len:past_len + ts, :] = new[:, 2 * KV_WIDTH:4 * KV_WIDTH]
        buf = jnp.concatenate([win_ref[...], new[:, 4 * KV_WIDTH:6 * KV_WIDTH]], axis=0)
        n_buf = buf.shape[0]
        keep = min(WINDOW, n_buf)
        win_out_ref[...] = buf[n_buf - keep:]

        q_tile = q_ref[...]
        gates = gate_ref[...]
        qpos = past_len + lax.broadcasted_iota(jnp.int32, (ts, 1), 0)
        dist_s = qpos - lax.broadcasted_iota(jnp.int32, (1, n_keys), 1)
        dist_w = qpos - (past_len - win_ref.shape[0] + lax.broadcasted_iota(jnp.int32, (1, n_buf), 1))
        ok_w = (dist_w >= 0) & (dist_w < WINDOW)
        for kvh in range(NSA_KV_HEADS):
            hs = slice(kvh * HEAD_DIM, (kvh + 1) * HEAD_DIM)
            vs = slice(KV_WIDTH + kvh * HEAD_DIM, KV_WIDTH + (kvh + 1) * HEAD_DIM)
            qs = _stack_heads(q_tile, kvh)
            o_c, sel = _compress_and_select(qs, kc_ref[:, hs], vc_ref[:, hs], qpos, kvh, ts, n_cmp, n_sel)
            ok_s = _selected_mask(sel, n_keys) & (dist_s >= 0)
            o_s = _attend(qs, slc_ref[:, hs], slc_ref[:, vs], dist_s.astype(F32), ok_s, kvh, ts)
            o_w = _attend(qs, buf[:, hs], buf[:, vs], dist_w.astype(F32), ok_w, kvh, ts)
            _merge_heads(o_ref, gates, o_c, o_s, o_w, kvh, ts)


def _nsa_decode(qn, kvn, gates, win_state, cache, page_table, wpos_rows, layer, dbsz, ts):
    n_pages = page_table.shape[1]
    page_size = cache.shape[2]
    past_len = n_pages * page_size
    assert ts < CMP_BLOCK and past_len % SEL_BLOCK == 0 and page_size % CMP_BLOCK == 0 and ts % SUBLANES == 0
    pages_per_step = math.gcd(n_pages, 8)
    n_cmp = past_len // CMP_BLOCK
    n_sel = past_len // SEL_BLOCK + 1
    w = -(-max(n_cmp, 2 * n_sel) // LANES) * LANES
    n_keys = past_len + SEL_BLOCK
    n_win = win_state.shape[2]
    keep = min(WINDOW, n_win + ts)
    kern = functools.partial(_nsa_decode_kernel, ts=ts, past_len=past_len, page_size=page_size, pages_per_step=pages_per_step)

    def page_spec(j):
        return pl.BlockSpec((None, None, page_size, 4 * KV_WIDTH),
                            lambda b, i, pt: (layer, pt[b, i * pages_per_step + j], 0, 0))

    grid_spec = pltpu.PrefetchScalarGridSpec(
        num_scalar_prefetch=1,
        grid=(dbsz, n_pages // pages_per_step),
        in_specs=[
            pl.BlockSpec((None, ts, Q_WIDTH), lambda b, i, pt: (b, 0, 0)),
            pl.BlockSpec((None, ts, KV6_WIDTH), lambda b, i, pt: (b, 0, 0)),
            pl.BlockSpec((None, ts, MISC_WIDTH), lambda b, i, pt: (b, 0, 0)),
            pl.BlockSpec((None, None, n_win, 2 * KV_WIDTH), lambda b, i, pt: (layer, b, 0, 0)),
            pl.BlockSpec(wpos_rows.shape, lambda b, i, pt: (0, 0, 0)),
        ] + [page_spec(j) for j in range(pages_per_step)],
        out_specs=[
            pl.BlockSpec((None, ts, Q_WIDTH), lambda b, i, pt: (b, 0, 0)),
            pl.BlockSpec((None, keep, 2 * KV_WIDTH), lambda b, i, pt: (b, 0, 0)),
        ],
        scratch_shapes=[pltpu.VMEM((w, KV_WIDTH), F32), pltpu.VMEM((w, KV_WIDTH), F32),
                        pltpu.VMEM((n_keys, 2 * KV_WIDTH), F32)],
    )
    return pl.pallas_call(
        kern,
        grid_spec=grid_spec,
        out_shape=[jax.ShapeDtypeStruct((dbsz, ts, Q_WIDTH), F32), jax.ShapeDtypeStruct((dbsz, keep, 2 * KV_WIDTH), F32)],
        compiler_params=pltpu.CompilerParams(dimension_semantics=("arbitrary", "arbitrary"), vmem_limit_bytes=VMEM_LIMIT),
        name="nsa_decode",
    )(page_table, qn.reshape(dbsz, ts, Q_WIDTH), kvn.reshape(dbsz, ts, KV6_WIDTH), gates.reshape(dbsz, ts, MISC_WIDTH),
      win_state, wpos_rows, *([cache] * pages_per_step))


def _gla_kernel(*refs, t_len, chunk, has_s0):
    if has_s0:
        q_ref, k_ref, v_ref, gg_ref, la_ref, gn_ref, s0_ref, o_ref, s_out_ref, st_ref = refs
    else:
        q_ref, k_ref, v_ref, gg_ref, la_ref, gn_ref, o_ref, s_out_ref, st_ref = refs
    dk2, dv2 = 2 * GLA_DK, 2 * GLA_DV
    cp = max(chunk, LANES)
    pad = cp - chunk

    rr = lax.broadcasted_iota(jnp.int32, (dv2, dk2), 0) // GLA_DV
    cc = lax.broadcasted_iota(jnp.int32, (dv2, dk2), 1) // GLA_DK
    diag = rr == cc
    if has_s0:
        z = jnp.zeros((GLA_DK, GLA_DV), F32)
        s_full = jnp.concatenate([jnp.concatenate([s0_ref[0], z], axis=1), jnp.concatenate([z, s0_ref[1]], axis=1)], axis=0)
        st_ref[...] = s_full.T
    else:
        st_ref[...] = jnp.zeros((dv2, dk2), F32)

    lane_head = lax.broadcasted_iota(jnp.int32, (1, dk2), 1) // GLA_DK
    trow = lax.broadcasted_iota(jnp.int32, (chunk, cp), 0)
    tcol = lax.broadcasted_iota(jnp.int32, (chunk, cp), 1)
    causal = tcol <= trow
    tril = jnp.where(causal, 1.0, 0.0).astype(BF16)
    mid = chunk // 2
    gn = gn_ref[...]

    def pad_rows(a):
        return a if pad == 0 else jnp.concatenate([a, jnp.zeros((pad, a.shape[1]), a.dtype)], axis=0)

    def step(ci, carry):
        r0 = pl.multiple_of(ci * chunk, chunk)
        rows = pl.ds(r0, chunk)
        q = q_ref[rows, :] * (GLA_DK ** -0.5)
        k = k_ref[rows, :]
        v = v_ref[rows, :]
        la0, la1, la2 = _split3(pad_rows(la_ref[rows, :]))
        cum = _dot(tril, la0) + _dot(tril, la1) + _dot(tril, la2)
        m = cum[mid:mid + 1, :]
        last = cum[chunk - 1:chunk, :]
        st = st_ref[...]
        inter = _dot_nt((q * jnp.exp(cum)).astype(BF16), st.astype(BF16))
        qs = q * jnp.exp(cum - m)
        ks = pad_rows((k * jnp.exp(m - cum)).astype(BF16))
        vp = pad_rows(v.astype(BF16))
        intra = []
        for h in range(2):
            a = _dot_nt(jnp.where(lane_head == h, qs, 0.0).astype(BF16), ks)
            a = jnp.where(causal, a, 0.0).astype(BF16)
            intra.append(_dot(a, vp[:, h * GLA_DV:(h + 1) * GLA_DV]))
        o = inter + jnp.concatenate(intra, axis=1)
        kd = pad_rows((k * jnp.exp(last - cum)).astype(BF16))
        upd = _dot_tn(vp, kd)
        st_ref[...] = st * jnp.exp(last) + jnp.where(diag, upd, 0.0)

        gg = gg_ref[rows, :]
        outs = []
        for h in range(2):
            oh = o[:, h * GLA_DV:(h + 1) * GLA_DV]
            y = oh * lax.rsqrt(jnp.mean(oh * oh, axis=-1, keepdims=True) + EPS) * gn
            gh = gg[:, h * GLA_DV:(h + 1) * GLA_DV]
            outs.append(y * (gh * _sigmoid(gh)))
        o_ref[rows, :] = jnp.concatenate(outs, axis=1)
        return carry

    lax.fori_loop(0, t_len // chunk, step, 0)
    s_fin = st_ref[...].T
    s_out_ref[0] = s_fin[0:GLA_DK, 0:GLA_DV]
    s_out_ref[1] = s_fin[GLA_DK:dk2, GLA_DV:dv2]


def _gla(g, la, gnorm, s0, bsz, t_len):
    chunk = math.gcd(t_len, GLA_CHUNK)
    has_s0 = s0 is not None
    kern = functools.partial(_gla_kernel, t_len=t_len, chunk=chunk, has_s0=has_s0)
    qk_blk = lambda off: pl.BlockSpec((None, t_len, 2 * GLA_DK), lambda b, p: (b, 0, off + p))
    v_blk = lambda off: pl.BlockSpec((None, t_len, 2 * GLA_DV), lambda b, p: (b, 0, off + p))
    g3 = g.reshape(bsz, t_len, G_WIDTH)
    n_qk = GLA_QK_WIDTH // (2 * GLA_DK)
    in_specs = [qk_blk(0), qk_blk(n_qk), v_blk(n_qk), v_blk(n_qk + GLA_V_WIDTH // (2 * GLA_DV)),
                pl.BlockSpec((None, t_len, 2 * GLA_DK), lambda b, p: (b, 0, p)),
                pl.BlockSpec(gnorm.shape, lambda b, p: (0, 0))]
    args = [g3, g3, g3, g3, la.reshape(bsz, t_len, GLA_QK_WIDTH), gnorm]
    if has_s0:
        in_specs.append(pl.BlockSpec((None, 2, GLA_DK, GLA_DV), lambda b, p: (b, p, 0, 0)))
        args.append(s0)
    return pl.pallas_call(
        kern,
        grid=(bsz, GLA_HEADS // 2),
        in_specs=in_specs,
        out_specs=[pl.BlockSpec((None, t_len, 2 * GLA_DV), lambda b, p: (b, 0, p)),
                   pl.BlockSpec((None, 2, GLA_DK, GLA_DV), lambda b, p: (b, p, 0, 0))],
        out_shape=[jax.ShapeDtypeStruct((bsz, t_len, GLA_V_WIDTH), F32),
                   jax.ShapeDtypeStruct((bsz, GLA_HEADS, GLA_DK, GLA_DV), F32)],
        scratch_shapes=[pltpu.VMEM((2 * GLA_DV, 2 * GLA_DK), F32)],
        compiler_params=pltpu.CompilerParams(dimension_semantics=("arbitrary", "arbitrary"), vmem_limit_bytes=VMEM_LIMIT),
        name="gla",
    )(*args)


FF_STEPS = 2
FF_CHUNK = D_FF // FF_STEPS
assert FF_CHUNK * FF_STEPS == D_FF and FF_CHUNK % LANES == 0


def _out_ffn_kernel(x_ref, on_ref, og_ref, wo_n_ref, wo_g_ref, ln_ref, wg_ref, wu_ref, wd_ref, y_ref, h_ref):
    j = pl.program_id(1)

    @pl.when(j == 0)
    def _():
        x1 = x_ref[...] + _dot(on_ref[...].astype(BF16), wo_n_ref[...]) + _dot(og_ref[...].astype(BF16), wo_g_ref[...])
        h_ref[...] = (x1 * lax.rsqrt(jnp.mean(x1 * x1, axis=-1, keepdims=True) + EPS) * ln_ref[...]).astype(BF16)
        y_ref[...] = x1

    h = h_ref[...]
    gate = _dot(h, wg_ref[...])
    up = _dot(h, wu_ref[...])
    y_ref[...] += _dot((gate * _sigmoid(gate) * up).astype(BF16), wd_ref[...])


def _out_ffn(x, o_nsa, o_gla, wo_n, wo_g, ln, wg, wu, wd, tm):
    n = x.shape[0]
    row = lambda w: pl.BlockSpec((tm, w), lambda i, j: (i, 0))
    full = lambda a: pl.BlockSpec(a.shape, lambda i, j: (0,) * a.ndim)
    return pl.pallas_call(
        _out_ffn_kernel,
        grid=(n // tm, FF_STEPS),
        in_specs=[row(D_MODEL), row(Q_WIDTH), row(GLA_V_WIDTH), full(wo_n), full(wo_g), full(ln),
                  pl.BlockSpec((D_MODEL, FF_CHUNK), lambda i, j: (0, j)),
                  pl.BlockSpec((D_MODEL, FF_CHUNK), lambda i, j: (0, j)),
                  pl.BlockSpec((FF_CHUNK, D_MODEL), lambda i, j: (j, 0))],
        out_specs=row(D_MODEL),
        out_shape=jax.ShapeDtypeStruct((n, D_MODEL), F32),
        scratch_shapes=[pltpu.VMEM((tm, D_MODEL), BF16)],
        compiler_params=pltpu.CompilerParams(dimension_semantics=("arbitrary", "arbitrary"), vmem_limit_bytes=VMEM_LIMIT),
        name="out_ffn",
    )(x, o_nsa, o_gla, wo_n, wo_g, ln, wg, wu, wd)


def _layer_weights(ln_mix, w_in, q_norm, k_norm, cmp_pos_w, w_a2, b_a, gla_norm, w_out, ln_ffn, w_gate, w_up, w_down):
    o = 0
    parts = []
    for width in (Q_WIDTH, KV6_WIDTH, N_GATES, GLA_QK_WIDTH, GLA_QK_WIDTH, GLA_V_WIDTH, GLA_V_WIDTH, GLA_RANK):
        parts.append(w_in[:, o:o + width])
        o += width
    wq, wkv, wgate, wgq, wgk, wgv, wgg, wga = parts
    pad = jnp.zeros((D_MODEL, MISC_WIDTH - N_GATES - GLA_RANK), w_in.dtype)
    wa = jnp.zeros((MISC_WIDTH, GLA_QK_WIDTH), F32).at[N_GATES:N_GATES + GLA_RANK].set(w_a2)
    return dict(
        ln_mix=ln_mix.reshape(1, D_MODEL),
        wq=wq.astype(BF16), wkv=wkv.astype(BF16),
        wg=jnp.concatenate([wgq, wgk, wgv, wgg], axis=1).astype(BF16),
        wm=jnp.concatenate([wgate, wga, pad], axis=1).astype(BF16),
        qg=jnp.tile(q_norm, LANES // HEAD_DIM).reshape(1, LANES),
        kg=jnp.tile(k_norm, (1, LANES // HEAD_DIM)),
        wa=wa, ba=b_a.reshape(1, GLA_QK_WIDTH),
        wpos=jnp.broadcast_to(cmp_pos_w[:, :, None], (2, CMP_BLOCK, LANES)),
        gnorm=gla_norm.reshape(1, GLA_DV),
        wo_n=w_out[:Q_WIDTH].astype(BF16), wo_g=w_out[Q_WIDTH:].astype(BF16),
        ln_ffn=ln_ffn.reshape(1, D_MODEL),
        w_gate=w_gate.astype(BF16), w_up=w_up.astype(BF16), w_down=w_down.astype(BF16),
    )


def _row_tile(n):
    return math.gcd(n, 512)


def _mix_and_ffn(x, w, attn, s0, bsz, t_len):
    tm = _row_tile(x.shape[0])
    qn, kvn, g, gates, la = _proj_in(x, w["ln_mix"], w["wq"], w["wkv"], w["wg"], w["wm"], w["qg"], w["kg"], w["wa"], w["ba"], tm)
    o_nsa, extra = attn(qn, kvn, gates)
    o_gla, s_new = _gla(g, la, w["gnorm"], s0, bsz, t_len)
    y = _out_ffn(x, o_nsa.reshape(-1, Q_WIDTH), o_gla.reshape(-1, GLA_V_WIDTH), w["wo_n"], w["wo_g"], w["ln_ffn"],
                 w["w_gate"], w["w_up"], w["w_down"], tm)
    return y, kvn, s_new, extra


def kernel(x_prompt, x_sample, cache_nsa_kv, state_nsa_win, state_gla, page_table, ln_mix, w_in, q_norm, k_norm, cmp_pos_w,
           w_a2, b_a, gla_norm, w_out, ln_ffn, w_gate, w_up, w_down):
    bsz, t_len = x_prompt.shape[:2]
    dbsz, ts = x_sample.shape[:2]
    depth = w_in.shape[0]
    n_phys, page_size = cache_nsa_kv.shape[1:3]
    cache = cache_nsa_kv.reshape(depth, n_phys, page_size, 4 * KV_WIDTH)
    win_state = state_nsa_win.reshape(depth, dbsz, state_nsa_win.shape[2], 2 * KV_WIDTH)
    keep_p = min(WINDOW, t_len)

    yp = x_prompt.reshape(bsz * t_len, D_MODEL)
    ys = x_sample.reshape(dbsz * ts, D_MODEL)
    rows_p, win_p, gla_p, rows_s, win_s, gla_s = [], [], [], [], [], []
    for l in range(depth):
        w = _layer_weights(ln_mix[l], w_in[l], q_norm[l], k_norm[l], cmp_pos_w[l], w_a2[l], b_a[l], gla_norm[l], w_out[l],
                           ln_ffn[l], w_gate[l], w_up[l], w_down[l])

        def attn_prompt(qn, kvn, gates):
            return _nsa_prompt(qn, kvn, gates, w["wpos"], bsz, t_len), None

        def attn_sample(qn, kvn, gates):
            return _nsa_decode(qn, kvn, gates, win_state, cache, page_table, w["wpos"], l, dbsz, ts)

        yp, kvn_p, st_p, _ = _mix_and_ffn(yp, w, attn_prompt, None, bsz, t_len)
        ys, kvn_s, st_s, nw_s = _mix_and_ffn(ys, w, attn_sample, state_gla[l], dbsz, ts)

        kvn_p = kvn_p.reshape(bsz, t_len, 6, NSA_KV_HEADS, HEAD_DIM)
        rows_p.append(kvn_p[:, :, :4])
        win_p.append(kvn_p[:, t_len - keep_p:, 4:])
        gla_p.append(st_p.astype(state_gla.dtype))
        rows_s.append(kvn_s.reshape(dbsz, ts, 6, NSA_KV_HEADS, HEAD_DIM)[:, :, :4])
        win_s.append(nw_s.reshape(dbsz, -1, 2, NSA_KV_HEADS, HEAD_DIM))
        gla_s.append(st_s.astype(state_gla.dtype))
    return (yp.reshape(bsz, t_len, D_MODEL), ys.reshape(dbsz, ts, D_MODEL), jnp.stack(rows_p), jnp.stack(win_p),
            jnp.stack(gla_p), jnp.stack(rows_s), jnp.stack(win_s), jnp.stack(gla_s))
```

```python
import functools
import math

import jax
import jax.numpy as jnp
from jax import lax
from jax.experimental import pallas as pl
from jax.experimental.pallas import tpu as pltpu

F32 = jnp.float32
BF16 = jnp.bfloat16

D_MODEL = 1024
NSA_HEADS = 8
NSA_KV_HEADS = 2
NSA_GROUP = NSA_HEADS // NSA_KV_HEADS
HEAD_DIM = 64
CMP_BLOCK = 32
SEL_BLOCK = 64
TOP_K = 16
WINDOW = 512
GLA_HEADS = 4
GLA_DK = 64
GLA_DV = 128
GLA_RANK = 16
GLA_TAU = 16.0
GLA_CHUNK = 64
D_FF = -(-(8 * D_MODEL) // (3 * 256)) * 256
KV_WIDTH = NSA_KV_HEADS * HEAD_DIM
Q_WIDTH = NSA_HEADS * HEAD_DIM
N_GATES = 3 * NSA_HEADS
GLA_QK_WIDTH = GLA_HEADS * GLA_DK
GLA_V_WIDTH = GLA_HEADS * GLA_DV
EPS = 1e-6
NEG_INF = -1e30
FORCE_SCORE = 1e4
BELOW_ALL = -3e38
SCALE = HEAD_DIM ** -0.5
LOG2E = 1.4426950408889634
SLOPES = tuple(tuple(2.0 ** (-8.0 * (k * NSA_GROUP + g + 1) / NSA_HEADS) for g in range(NSA_GROUP))
               for k in range(NSA_KV_HEADS))
SLAB_HEADS = tuple(h for g in range(NSA_GROUP) for h in (g, NSA_GROUP + g))

LANES = 128
SUBLANES = 8
VMEM_LIMIT = 56 * 1024 * 1024
assert KV_WIDTH == LANES and NSA_KV_HEADS == 2

G_WIDTH = 2 * GLA_QK_WIDTH + 2 * GLA_V_WIDTH
MISC_WIDTH = LANES
KV6_WIDTH = 6 * KV_WIDTH


def _dot(a, b):
    return jnp.dot(a, b, preferred_element_type=F32)


def _dot_nt(a, b):
    return lax.dot_general(a, b, (((1,), (1,)), ((), ())), preferred_element_type=F32)


def _dot_tn(a, b):
    return lax.dot_general(a, b, (((0,), (0,)), ((), ())), preferred_element_type=F32)


def _split3(a):
    a0 = a.astype(BF16)
    r = a - a0.astype(F32)
    a1 = r.astype(BF16)
    a2 = (r - a1.astype(F32)).astype(BF16)
    return a0, a1, a2


def _dot_f32(a, b):
    a0, a1, a2 = _split3(a)
    b0, b1, b2 = _split3(b)
    return (_dot(a0, b0) + (_dot(a0, b1) + _dot(a1, b0)) + (_dot(a0, b2) + _dot(a1, b1) + _dot(a2, b0)))


def _sigmoid(x):
    return 1.0 / (1.0 + jnp.exp(-x))


def _low_half(width=LANES):
    return lax.broadcasted_iota(jnp.int32, (1, width), 1) < HEAD_DIM


def _half_lane_rms(x, gain):
    x2 = x * x
    lo = _low_half()
    s_lo = jnp.sum(jnp.where(lo, x2, 0.0), axis=-1, keepdims=True)
    s_hi = jnp.sum(jnp.where(lo, 0.0, x2), axis=-1, keepdims=True)
    ms = jnp.where(lo, s_lo, s_hi) * (1.0 / HEAD_DIM)
    return x * lax.rsqrt(ms + EPS) * gain


def _proj_in_kernel(x_ref, ln_ref, wq_ref, wkv_ref, wg_ref, wm_ref, qg_ref, kg_ref, wa_ref, ba_ref,
                    q_out, kv_out, g_out, gate_out, la_out, *, kv_transposed):
    x = x_ref[...]
    h = x * lax.rsqrt(jnp.mean(x * x, axis=-1, keepdims=True) + EPS) * ln_ref[...]
    hb = h.astype(BF16)

    q = _dot_nt(hb, wq_ref[...])
    for j in range(Q_WIDTH // LANES):
        sl = slice(j * LANES, (j + 1) * LANES)
        q_out[:, sl] = _half_lane_rms(q[:, sl], qg_ref[...])

    if kv_transposed:
        kv = _dot_nt(wkv_ref[...], hb)
        tm = kv.shape[1]
        for s in range(6):
            rows = slice(s * KV_WIDTH, (s + 1) * KV_WIDTH)
            if s % 2 == 0:
                k3 = kv[rows].reshape(NSA_KV_HEADS, HEAD_DIM, tm)
                ms = jnp.mean(k3 * k3, axis=1, keepdims=True)
                kv_out[rows, :] = (k3 * lax.rsqrt(ms + EPS)).reshape(KV_WIDTH, tm) * kg_ref[s // 2]
            else:
                kv_out[rows, :] = kv[rows]
    else:
        kv = _dot_nt(hb, wkv_ref[...])
        for s in range(6):
            sl = slice(s * KV_WIDTH, (s + 1) * KV_WIDTH)
            if s % 2 == 0:
                kv_out[:, sl] = _half_lane_rms(kv[:, sl], kg_ref[s // 2:s // 2 + 1, :])
            else:
                kv_out[:, sl] = kv[:, sl]

    g_out[...] = _dot_nt(hb, wg_ref[...])

    m = _dot_nt(hb, wm_ref[...])
    gate_out[...] = _sigmoid(m)
    z = _dot_f32(m, wa_ref[...]) + ba_ref[...]
    la_out[...] = (jnp.minimum(z, 0.0) - jnp.log1p(jnp.exp(-jnp.abs(z)))) * (1.0 / GLA_TAU)


def _proj_in(x, w, tm, tokens_per_seq, kv_transposed):
    n = x.shape[0]
    row = lambda width: pl.BlockSpec((tm, width), lambda i: (i, 0))
    full = lambda a: pl.BlockSpec(a.shape, lambda i: (0,) * a.ndim)
    if kv_transposed:
        tiles = tokens_per_seq // tm
        kv_spec = pl.BlockSpec((None, KV6_WIDTH, tm), lambda i: (i // tiles, 0, i % tiles))
        kv_shape = jax.ShapeDtypeStruct((n // tokens_per_seq, KV6_WIDTH, tokens_per_seq), F32)
        kg = w["kg_t"]
    else:
        kv_spec = row(KV6_WIDTH)
        kv_shape = jax.ShapeDtypeStruct((n, KV6_WIDTH), F32)
        kg = w["kg"]
    args = (x, w["ln_mix"], w["wq"], w["wkv"], w["wg"], w["wm"], w["qg"], kg, w["wa"], w["ba"])
    return pl.pallas_call(
        functools.partial(_proj_in_kernel, kv_transposed=kv_transposed),
        grid=(n // tm,),
        in_specs=[row(D_MODEL)] + [full(a) for a in args[1:]],
        out_specs=[row(Q_WIDTH), kv_spec, row(G_WIDTH), row(MISC_WIDTH), row(GLA_QK_WIDTH)],
        out_shape=[jax.ShapeDtypeStruct((n, Q_WIDTH), F32), kv_shape] +
                  [jax.ShapeDtypeStruct((n, width), F32) for width in (G_WIDTH, MISC_WIDTH, GLA_QK_WIDTH)],
        compiler_params=pltpu.CompilerParams(dimension_semantics=("arbitrary",), vmem_limit_bytes=VMEM_LIMIT),
        name="proj_in",
    )(*args)


def _group_queries(q_tile, kvh):
    keep = _low_half() if kvh == 0 else ~_low_half()
    return jnp.concatenate([jnp.where(keep, q_tile[:, g * LANES:(g + 1) * LANES] * (SCALE * LOG2E), 0.0)
                            for g in range(NSA_GROUP)], axis=0).astype(BF16)


def _masked_neg_dist(dist, ok):
    return jnp.where(ok, -dist.astype(F32), NEG_INF)


def _softmax_rows(s_rows, neg_dist, kvh, tq):
    ps, inv = [], []
    for g in range(NSA_GROUP):
        s = s_rows[g * tq:(g + 1) * tq] + (SLOPES[kvh][g] * LOG2E) * neg_dist
        p = jnp.exp2(s - jnp.max(s, axis=-1, keepdims=True))
        inv.append(1.0 / jnp.sum(p, axis=-1, keepdims=True))
        ps.append(p)
    return jnp.concatenate(ps, axis=0).astype(BF16), jnp.concatenate(inv, axis=0)


def _compressed_probs(s_rows, qpos, kvh, tq, n_cmp):
    w = s_rows.shape[1]
    lane = lax.broadcasted_iota(jnp.int32, (1, w), 1)
    dist = qpos - ((lane + 1) * CMP_BLOCK - 1)
    valid = (dist >= 0) & (lane < n_cmp)
    nd = _masked_neg_dist(dist, valid)
    ps = []
    imp = jnp.zeros((tq, w), F32)
    for g in range(NSA_GROUP):
        s = s_rows[g * tq:(g + 1) * tq] + (SLOPES[kvh][g] * LOG2E) * nd
        e = jnp.exp2(s - jnp.max(s, axis=-1, keepdims=True))
        p = jnp.where(valid, e / jnp.sum(e, axis=-1, keepdims=True), 0.0)
        imp = imp + p
        ps.append(p)
    return jnp.concatenate(ps, axis=0).astype(BF16), imp


def _select_blocks(imp, qpos, n_sel):
    tq, w = imp.shape
    lane = lax.broadcasted_iota(jnp.int32, (1, w), 1)
    pair = imp + pltpu.roll(imp, w - 1, 1)
    blk = lane >> 1
    cur = qpos >> int(math.log2(SEL_BLOCK))
    is_blk = ((lane & 1) == 0) & (blk < n_sel)
    forced = (blk == 0) | (blk == cur) | (blk == cur - 1)
    score = jnp.where(forced, FORCE_SCORE, jnp.where(blk <= cur, pair, NEG_INF))
    score = jnp.where(is_blk, score, BELOW_ALL)
    rank = jnp.zeros((tq, w), jnp.int32)
    for i in range(n_sel):
        col = score[:, 2 * i:2 * i + 1]
        beats = (col > score) | ((col == score) & (lane > 2 * i))
        rank = rank + jnp.where(beats, 1, 0)
    return jnp.where((rank < min(TOP_K, n_sel)) & is_blk, 1.0, 0.0)


def _gated_sum(gates, kvh, tq, branches):
    out = []
    for g in range(NSA_GROUP):
        h = kvh * NSA_GROUP + g
        rows = slice(g * tq, (g + 1) * tq)
        out.append(sum(gates[:, c * NSA_HEADS + h:c * NSA_HEADS + h + 1] * o[rows] for c, o in enumerate(branches)))
    return out


def _store_slabs(o_ref, per_group):
    lo = _low_half()
    for g in range(NSA_GROUP):
        o_ref[:, g * LANES:(g + 1) * LANES] = jnp.where(lo, per_group[0][g], per_group[1][g])


def _nsa_prompt_kernel(q_ref, kv_ref, gate_ref, wc_ref, o_ref, kc_ref, vc_ref, *, t_len, tq, key_step):
    qi = pl.program_id(1)
    n_cmp = t_len // CMP_BLOCK
    n_sel = -(-t_len // SEL_BLOCK)
    w = kc_ref.shape[1]

    @pl.when(qi == 0)
    def _():
        kc_ref[...] = _dot_f32(kv_ref[0:KV_WIDTH, :], wc_ref[0]).astype(BF16)
        vc_ref[...] = _dot_f32(kv_ref[KV_WIDTH:2 * KV_WIDTH, :], wc_ref[1]).astype(BF16)

    q0 = qi * tq
    qpos = q0 + lax.broadcasted_iota(jnp.int32, (tq, 1), 0)
    n_win = min(WINDOW + tq, t_len)
    w_start = pl.multiple_of(jnp.maximum(q0 + tq - n_win, 0), LANES)
    stream = lambda s: slice(s * KV_WIDTH, (s + 1) * KV_WIDTH)

    def body(n_keys):
        q_tile = q_ref[...]
        gates = gate_ref[...]
        dist_s = qpos - lax.broadcasted_iota(jnp.int32, (1, n_keys), 1)
        dist_w = qpos - (w_start + lax.broadcasted_iota(jnp.int32, (1, n_win), 1))
        nd_w = _masked_neg_dist(dist_w, (dist_w >= 0) & (dist_w < WINDOW))
        li = lax.broadcasted_iota(jnp.int32, (w, 1), 0)
        ti = lax.broadcasted_iota(jnp.int32, (1, n_keys), 1)
        expand = jnp.where(((ti >> int(math.log2(SEL_BLOCK))) << 1) == li, 1.0, 0.0).astype(BF16)
        k_s = kv_ref[stream(2), 0:n_keys].astype(BF16)
        v_s = kv_ref[stream(3), 0:n_keys].astype(BF16)
        k_w = kv_ref[stream(4), pl.ds(w_start, n_win)].astype(BF16)
        v_w = kv_ref[stream(5), pl.ds(w_start, n_win)].astype(BF16)
        per_group = []
        for kvh in range(NSA_KV_HEADS):
            qs = _group_queries(q_tile, kvh)
            p_c, imp = _compressed_probs(_dot(qs, kc_ref[...]), qpos, kvh, tq, n_cmp)
            o_c = _dot_nt(p_c, vc_ref[...])
            sel = _select_blocks(imp, qpos, n_sel)
            nd_s = _masked_neg_dist(dist_s, (_dot(sel.astype(BF16), expand) > 0.5) & (dist_s >= 0))
            p_s, inv_s = _softmax_rows(_dot(qs, k_s), nd_s, kvh, tq)
            o_s = _dot_nt(p_s, v_s) * inv_s
            p_w, inv_w = _softmax_rows(_dot(qs, k_w), nd_w, kvh, tq)
            o_w = _dot_nt(p_w, v_w) * inv_w
            per_group.append(_gated_sum(gates, kvh, tq, (o_c, o_s, o_w)))
        _store_slabs(o_ref, per_group)

    n_classes = -(-t_len // key_step)
    for c in range(n_classes):
        n_keys = min((c + 1) * key_step, t_len)

        @pl.when((q0 + tq - 1) // key_step == c)
        def _(n_keys=n_keys):
            body(n_keys)


def _nsa_prompt(qn, kv_t, gates, wc, bsz, t_len):
    tq = math.gcd(t_len, 128)
    key_step = math.gcd(t_len, 512)
    w = wc.shape[2]
    kern = functools.partial(_nsa_prompt_kernel, t_len=t_len, tq=tq, key_step=key_step)
    return pl.pallas_call(
        kern,
        grid=(bsz, t_len // tq),
        in_specs=[
            pl.BlockSpec((None, tq, Q_WIDTH), lambda b, i: (b, i, 0)),
            pl.BlockSpec((None, KV6_WIDTH, t_len), lambda b, i: (b, 0, 0)),
            pl.BlockSpec((None, tq, MISC_WIDTH), lambda b, i: (b, i, 0)),
            pl.BlockSpec(wc.shape, lambda b, i: (0, 0, 0)),
        ],
        out_specs=pl.BlockSpec((None, tq, Q_WIDTH), lambda b, i: (b, i, 0)),
        out_shape=jax.ShapeDtypeStruct((bsz, t_len, Q_WIDTH), F32),
        scratch_shapes=[pltpu.VMEM((KV_WIDTH, w), BF16), pltpu.VMEM((KV_WIDTH, w), BF16)],
        compiler_params=pltpu.CompilerParams(dimension_semantics=("arbitrary", "arbitrary"), vmem_limit_bytes=VMEM_LIMIT),
        name="nsa_prompt",
    )(qn.reshape(bsz, t_len, Q_WIDTH), kv_t, gates.reshape(bsz, t_len, MISC_WIDTH), wc)


def _pad_rows(a, n):
    return jnp.concatenate([a, jnp.zeros((n - a.shape[0], a.shape[1]), a.dtype)], axis=0)


def _nsa_decode_kernel(pt_ref, q_ref, kv_ref, gate_ref, win_ref, wpos_ref, *rest, ts, past_len, page_size, n_pages, w):
    page_refs = rest[:n_pages]
    o_ref, win_out_ref, s_ref, p_ref = rest[n_pages:]
    del pt_ref
    n_cmp = past_len // CMP_BLOCK
    n_sel = past_len // SEL_BLOCK + 1
    n_keys = past_len + page_size
    n_win = win_ref.shape[1]
    rows_q = NSA_GROUP * ts
    stream = lambda s: slice(s * KV_WIDTH, (s + 1) * KV_WIDTH)

    new = kv_ref[...]
    q_tile = q_ref[...]
    gates = gate_ref[...]
    qs = jnp.concatenate([_group_queries(q_tile, kvh) for kvh in range(NSA_KV_HEADS)], axis=0)
    qpos = past_len + lax.broadcasted_iota(jnp.int32, (ts, 1), 0)

    per_pair = 2 * page_size // CMP_BLOCK
    wk = jnp.concatenate([wpos_ref[0]] * per_pair, axis=0).reshape(per_pair, CMP_BLOCK, KV_WIDTH)
    wv = jnp.concatenate([wpos_ref[1]] * per_pair, axis=0).reshape(per_pair, CMP_BLOCK, KV_WIDTH)
    kcs, vcs = [], []
    for j in range(0, n_pages, 2):
        k2 = jnp.concatenate([page_refs[j][stream(0), :].T, page_refs[j + 1][stream(0), :].T], axis=0)
        v2 = jnp.concatenate([page_refs[j][stream(1), :].T, page_refs[j + 1][stream(1), :].T], axis=0)
        kcs.append(jnp.sum(k2.reshape(per_pair, CMP_BLOCK, KV_WIDTH) * wk, axis=1))
        vcs.append(jnp.sum(v2.reshape(per_pair, CMP_BLOCK, KV_WIDTH) * wv, axis=1))
    kc = _pad_rows(jnp.concatenate(kcs, axis=0), w).astype(BF16)
    vc = _pad_rows(jnp.concatenate(vcs, axis=0), w).astype(BF16)

    s_c = _dot_nt(qs, kc)
    p_cs, sels = [], []
    for kvh in range(NSA_KV_HEADS):
        p_c, imp = _compressed_probs(s_c[kvh * rows_q:(kvh + 1) * rows_q], qpos, kvh, ts, n_cmp)
        p_cs.append(p_c)
        sels.append(_select_blocks(imp, qpos, n_sel))
    o_c = _dot(jnp.concatenate(p_cs, axis=0), vc)

    new_k = _pad_rows(new[:, stream(2)], page_size).astype(BF16)
    new_v = _pad_rows(new[:, stream(3)], page_size).astype(BF16)
    for j in range(n_pages):
        s_ref[:, j * page_size:(j + 1) * page_size] = _dot(qs, page_refs[j][stream(2), :].astype(BF16))
    s_ref[:, past_len:n_keys] = _dot_nt(qs, new_k)
    dist_s = qpos - lax.broadcasted_iota(jnp.int32, (1, n_keys), 1)
    blocks_per_page = page_size // SEL_BLOCK
    page_lane_blk = lax.broadcasted_iota(jnp.int32, (1, page_size), 1) >> int(math.log2(SEL_BLOCK))
    invs = []
    for kvh in range(NSA_KV_HEADS):
        sel = sels[kvh]
        pieces = []
        for j in range(n_pages + 1):
            piece = jnp.zeros((ts, page_size), F32)
            for r in range(blocks_per_page):
                blk = j * blocks_per_page + r
                if blk < n_sel:
                    piece = jnp.where(page_lane_blk == r, sel[:, 2 * blk:2 * blk + 1], piece)
            pieces.append(piece)
        keymask = jnp.concatenate(pieces, axis=1)
        nd_s = _masked_neg_dist(dist_s, (keymask > 0.5) & (dist_s >= 0))
        p_s, inv_s = _softmax_rows(s_ref[kvh * rows_q:(kvh + 1) * rows_q, :], nd_s, kvh, ts)
        p_ref[kvh * rows_q:(kvh + 1) * rows_q, :] = p_s
        invs.append(inv_s)
    o_s = _dot(p_ref[:, past_len:n_keys], new_v)
    for j in range(n_pages):
        o_s = o_s + _dot_nt(p_ref[:, j * page_size:(j + 1) * page_size], page_refs[j][stream(3), :].astype(BF16))
    o_s = o_s * jnp.concatenate(invs, axis=0)

    new_kw = _pad_rows(new[:, stream(4)], LANES).astype(BF16)
    new_vw = _pad_rows(new[:, stream(5)], LANES).astype(BF16)
    s_w = jnp.concatenate([_dot(qs, win_ref[0:KV_WIDTH, :].astype(BF16)), _dot_nt(qs, new_kw)], axis=1)
    wpos = jnp.concatenate([past_len - n_win + lax.broadcasted_iota(jnp.int32, (1, n_win), 1),
                            past_len + lax.broadcasted_iota(jnp.int32, (1, LANES), 1)], axis=1)
    dist_w = qpos - wpos
    nd_w = _masked_neg_dist(dist_w, (dist_w >= 0) & (dist_w < WINDOW))
    p_ws, inv_ws = [], []
    for kvh in range(NSA_KV_HEADS):
        p_w, inv_w = _softmax_rows(s_w[kvh * rows_q:(kvh + 1) * rows_q], nd_w, kvh, ts)
        p_ws.append(p_w)
        inv_ws.append(inv_w)
    p_w = jnp.concatenate(p_ws, axis=0)
    o_w = (_dot_nt(p_w[:, 0:n_win], win_ref[KV_WIDTH:2 * KV_WIDTH, :].astype(BF16)) + _dot(p_w[:, n_win:], new_vw))
    o_w = o_w * jnp.concatenate(inv_ws, axis=0)

    per_group = [_gated_sum(gates, kvh, ts, [o[kvh * rows_q:(kvh + 1) * rows_q] for o in (o_c, o_s, o_w)])
                 for kvh in range(NSA_KV_HEADS)]
    _store_slabs(o_ref, per_group)

    new_t = _pad_rows(new[:, 4 * KV_WIDTH:6 * KV_WIDTH], LANES).T
    new_t = pltpu.roll(new_t, LANES - ts, 1)
    shifted = pltpu.roll(win_ref[...], n_win - ts, 1)
    tail_lane = lax.broadcasted_iota(jnp.int32, (1, LANES), 1)
    win_out_ref[:, 0:n_win - LANES] = shifted[:, 0:n_win - LANES]
    win_out_ref[:, n_win - LANES:n_win] = jnp.where(tail_lane >= LANES - ts, new_t, shifted[:, n_win - LANES:n_win])


def _nsa_decode(qn, kvn, gates, win_t, cache_t, page_table, wpos_rows, layer, dbsz, ts):
    n_pages = page_table.shape[1]
    page_size = cache_t.shape[3]
    past_len = n_pages * page_size
    n_win = win_t.shape[3]
    assert ts < CMP_BLOCK and ts % SUBLANES == 0 and page_size == LANES and n_pages % 2 == 0
    assert n_win == WINDOW and past_len >= WINDOW
    n_cmp = past_len // CMP_BLOCK
    n_sel = past_len // SEL_BLOCK + 1
    w = -(-max(n_cmp, 2 * n_sel) // LANES) * LANES
    n_keys = past_len + page_size
    kern = functools.partial(_nsa_decode_kernel, ts=ts, past_len=past_len, page_size=page_size, n_pages=n_pages, w=w)

    def page_spec(j):
        return pl.BlockSpec((None, None, 4 * KV_WIDTH, page_size), lambda b, pt: (layer, pt[b, j], 0, 0))

    grid_spec = pltpu.PrefetchScalarGridSpec(
        num_scalar_prefetch=1,
        grid=(dbsz,),
        in_specs=[
            pl.BlockSpec((None, ts, Q_WIDTH), lambda b, pt: (b, 0, 0)),
            pl.BlockSpec((None, ts, KV6_WIDTH), lambda b, pt: (b, 0, 0)),
            pl.BlockSpec((None, ts, MISC_WIDTH), lambda b, pt: (b, 0, 0)),
            pl.BlockSpec((None, None, 2 * KV_WIDTH, n_win), lambda b, pt: (layer, b, 0, 0)),
            pl.BlockSpec(wpos_rows.shape, lambda b, pt: (0, 0, 0)),
        ] + [page_spec(j) for j in range(n_pages)],
        out_specs=[
            pl.BlockSpec((None, ts, Q_WIDTH), lambda b, pt: (b, 0, 0)),
            pl.BlockSpec((None, 2 * KV_WIDTH, n_win), lambda b, pt: (b, 0, 0)),
        ],
        scratch_shapes=[pltpu.VMEM((NSA_HEADS * ts, n_keys), F32), pltpu.VMEM((NSA_HEADS * ts, n_keys), BF16)],
    )
    return pl.pallas_call(
        kern,
        grid_spec=grid_spec,
        out_shape=[jax.ShapeDtypeStruct((dbsz, ts, Q_WIDTH), F32), jax.ShapeDtypeStruct((dbsz, 2 * KV_WIDTH, n_win), F32)],
        compiler_params=pltpu.CompilerParams(dimension_semantics=("arbitrary",), vmem_limit_bytes=VMEM_LIMIT),
        name="nsa_decode",
    )(page_table, qn.reshape(dbsz, ts, Q_WIDTH), kvn.reshape(dbsz, ts, KV6_WIDTH), gates.reshape(dbsz, ts, MISC_WIDTH),
      win_t, wpos_rows, *([cache_t] * n_pages))


def _gla_kernel(*refs, t_len, chunk, has_s0):
    if has_s0:
        q_ref, k_ref, v_ref, gg_ref, la_ref, gn_ref, s0_ref, o_ref, s_out_ref, st_ref = refs
    else:
        q_ref, k_ref, v_ref, gg_ref, la_ref, gn_ref, o_ref, s_out_ref, st_ref = refs
    dk2, dv2 = 2 * GLA_DK, 2 * GLA_DV
    cp = max(chunk, LANES)
    pad = cp - chunk

    rr = lax.broadcasted_iota(jnp.int32, (dv2, dk2), 0) // GLA_DV
    cc = lax.broadcasted_iota(jnp.int32, (dv2, dk2), 1) // GLA_DK
    diag = rr == cc
    if has_s0:
        z = jnp.zeros((GLA_DK, GLA_DV), F32)
        s_full = jnp.concatenate([jnp.concatenate([s0_ref[0], z], axis=1), jnp.concatenate([z, s0_ref[1]], axis=1)], axis=0)
        st_ref[...] = s_full.T
    else:
        st_ref[...] = jnp.zeros((dv2, dk2), F32)

    lane_head = lax.broadcasted_iota(jnp.int32, (1, dk2), 1) // GLA_DK
    trow = lax.broadcasted_iota(jnp.int32, (chunk, cp), 0)
    tcol = lax.broadcasted_iota(jnp.int32, (chunk, cp), 1)
    causal = tcol <= trow
    tril = jnp.where(causal, 1.0, 0.0).astype(BF16)
    mid = chunk // 2
    gn = gn_ref[...]

    def pad_rows(a):
        return a if pad == 0 else jnp.concatenate([a, jnp.zeros((pad, a.shape[1]), a.dtype)], axis=0)

    def step(ci, carry):
        r0 = pl.multiple_of(ci * chunk, chunk)
        rows = pl.ds(r0, chunk)
        q = q_ref[rows, :] * (GLA_DK ** -0.5)
        k = k_ref[rows, :]
        v = v_ref[rows, :]
        la0, la1, la2 = _split3(pad_rows(la_ref[rows, :]))
        cum = _dot(tril, la0) + _dot(tril, la1) + _dot(tril, la2)
        m = cum[mid:mid + 1, :]
        last = cum[chunk - 1:chunk, :]
        st = st_ref[...]
        inter = _dot_nt((q * jnp.exp(cum)).astype(BF16), st.astype(BF16))
        qs = q * jnp.exp(cum - m)
        ks = pad_rows((k * jnp.exp(m - cum)).astype(BF16))
        vp = pad_rows(v.astype(BF16))
        intra = []
        for h in range(2):
            a = _dot_nt(jnp.where(lane_head == h, qs, 0.0).astype(BF16), ks)
            a = jnp.where(causal, a, 0.0).astype(BF16)
            intra.append(_dot(a, vp[:, h * GLA_DV:(h + 1) * GLA_DV]))
        o = inter + jnp.concatenate(intra, axis=1)
        kd = pad_rows((k * jnp.exp(last - cum)).astype(BF16))
        upd = _dot_tn(vp, kd)
        st_ref[...] = st * jnp.exp(last) + jnp.where(diag, upd, 0.0)

        gg = gg_ref[rows, :]
        outs = []
        for h in range(2):
            oh = o[:, h * GLA_DV:(h + 1) * GLA_DV]
            y = oh * lax.rsqrt(jnp.mean(oh * oh, axis=-1, keepdims=True) + EPS) * gn
            gh = gg[:, h * GLA_DV:(h + 1) * GLA_DV]
            outs.append(y * (gh * _sigmoid(gh)))
        o_ref[rows, :] = jnp.concatenate(outs, axis=1)
        return carry

    lax.fori_loop(0, t_len // chunk, step, 0)
    s_fin = st_ref[...].T
    s_out_ref[0] = s_fin[0:GLA_DK, 0:GLA_DV]
    s_out_ref[1] = s_fin[GLA_DK:dk2, GLA_DV:dv2]


def _gla(g, la, gnorm, s0, bsz, t_len):
    chunk = math.gcd(t_len, GLA_CHUNK)
    has_s0 = s0 is not None
    kern = functools.partial(_gla_kernel, t_len=t_len, chunk=chunk, has_s0=has_s0)
    qk_blk = lambda off: pl.BlockSpec((None, t_len, 2 * GLA_DK), lambda b, p: (b, 0, off + p))
    v_blk = lambda off: pl.BlockSpec((None, t_len, 2 * GLA_DV), lambda b, p: (b, 0, off + p))
    g3 = g.reshape(bsz, t_len, G_WIDTH)
    n_qk = GLA_QK_WIDTH // (2 * GLA_DK)
    in_specs = [qk_blk(0), qk_blk(n_qk), v_blk(n_qk), v_blk(n_qk + GLA_V_WIDTH // (2 * GLA_DV)),
                pl.BlockSpec((None, t_len, 2 * GLA_DK), lambda b, p: (b, 0, p)),
                pl.BlockSpec(gnorm.shape, lambda b, p: (0, 0))]
    args = [g3, g3, g3, g3, la.reshape(bsz, t_len, GLA_QK_WIDTH), gnorm]
    if has_s0:
        in_specs.append(pl.BlockSpec((None, 2, GLA_DK, GLA_DV), lambda b, p: (b, p, 0, 0)))
        args.append(s0)
    return pl.pallas_call(
        kern,
        grid=(bsz, GLA_HEADS // 2),
        in_specs=in_specs,
        out_specs=[pl.BlockSpec((None, t_len, 2 * GLA_DV), lambda b, p: (b, 0, p)),
                   pl.BlockSpec((None, 2, GLA_DK, GLA_DV), lambda b, p: (b, p, 0, 0))],
        out_shape=[jax.ShapeDtypeStruct((bsz, t_len, GLA_V_WIDTH), F32),
                   jax.ShapeDtypeStruct((bsz, GLA_HEADS, GLA_DK, GLA_DV), F32)],
        scratch_shapes=[pltpu.VMEM((2 * GLA_DV, 2 * GLA_DK), F32)],
        compiler_params=pltpu.CompilerParams(dimension_semantics=("arbitrary", "arbitrary"), vmem_limit_bytes=VMEM_LIMIT),
        name="gla",
    )(*args)


FF_STEPS = 2
FF_CHUNK = D_FF // FF_STEPS
assert FF_CHUNK * FF_STEPS == D_FF and FF_CHUNK % LANES == 0


def _out_ffn_kernel(x_ref, on_ref, og_ref, wo_n_ref, wo_g_ref, ln_ref, wg_ref, wu_ref, wd_ref, y_ref, h_ref):
    j = pl.program_id(1)

    @pl.when(j == 0)
    def _():
        x1 = x_ref[...] + _dot(on_ref[...].astype(BF16), wo_n_ref[...]) + _dot(og_ref[...].astype(BF16), wo_g_ref[...])
        h_ref[...] = (x1 * lax.rsqrt(jnp.mean(x1 * x1, axis=-1, keepdims=True) + EPS) * ln_ref[...]).astype(BF16)
        y_ref[...] = x1

    h = h_ref[...]
    gate = _dot(h, wg_ref[...])
    up = _dot(h, wu_ref[...])
    y_ref[...] += _dot((gate * _sigmoid(gate) * up).astype(BF16), wd_ref[...])


def _out_ffn(x, o_nsa, o_gla, wo_n, wo_g, ln, wg, wu, wd, tm):
    n = x.shape[0]
    row = lambda w: pl.BlockSpec((tm, w), lambda i, j: (i, 0))
    full = lambda a: pl.BlockSpec(a.shape, lambda i, j: (0,) * a.ndim)
    return pl.pallas_call(
        _out_ffn_kernel,
        grid=(n // tm, FF_STEPS),
        in_specs=[row(D_MODEL), row(Q_WIDTH), row(GLA_V_WIDTH), full(wo_n), full(wo_g), full(ln),
                  pl.BlockSpec((D_MODEL, FF_CHUNK), lambda i, j: (0, j)),
                  pl.BlockSpec((D_MODEL, FF_CHUNK), lambda i, j: (0, j)),
                  pl.BlockSpec((FF_CHUNK, D_MODEL), lambda i, j: (j, 0))],
        out_specs=row(D_MODEL),
        out_shape=jax.ShapeDtypeStruct((n, D_MODEL), F32),
        scratch_shapes=[pltpu.VMEM((tm, D_MODEL), BF16)],
        compiler_params=pltpu.CompilerParams(dimension_semantics=("arbitrary", "arbitrary"), vmem_limit_bytes=VMEM_LIMIT),
        name="out_ffn",
    )(x, o_nsa, o_gla, wo_n, wo_g, ln, wg, wu, wd)


def _layer_weights(ln_mix, w_in, q_norm, k_norm, cmp_pos_w, w_a2, b_a, gla_norm, w_out, ln_ffn, w_gate, w_up, w_down,
                   tm_prompt, t_len):
    w_in_t = w_in.T
    o = 0
    parts = []
    for width in (Q_WIDTH, KV6_WIDTH, N_GATES, GLA_QK_WIDTH, GLA_QK_WIDTH, GLA_V_WIDTH, GLA_V_WIDTH, GLA_RANK):
        parts.append(w_in_t[o:o + width])
        o += width
    wq, wkv, wgate, wgq, wgk, wgv, wgg, wga = parts
    wq = jnp.concatenate([wq[h * HEAD_DIM:(h + 1) * HEAD_DIM] for h in SLAB_HEADS], axis=0)
    pad = jnp.zeros((MISC_WIDTH - N_GATES - GLA_RANK, D_MODEL), w_in.dtype)
    wa = jnp.zeros((MISC_WIDTH, GLA_QK_WIDTH), F32).at[N_GATES:N_GATES + GLA_RANK].set(w_a2)
    n_cmp = t_len // CMP_BLOCK
    w_lanes = -(-max(n_cmp, 2 * (-(-t_len // SEL_BLOCK))) // LANES) * LANES
    tok = jnp.arange(t_len)
    in_blk = (tok[:, None] // CMP_BLOCK == jnp.arange(w_lanes)[None, :]) & (tok[:, None] < n_cmp * CMP_BLOCK)
    wc = jnp.where(in_blk[None], cmp_pos_w[:, tok % CMP_BLOCK][:, :, None], 0.0)
    kg_rows = jnp.tile(k_norm, (1, NSA_KV_HEADS))
    return dict(
        ln_mix=ln_mix.reshape(1, D_MODEL),
        wq=wq.astype(BF16), wkv=wkv.astype(BF16),
        wg=jnp.concatenate([wgq, wgk, wgv, wgg], axis=0).astype(BF16),
        wm=jnp.concatenate([wgate, wga, pad], axis=0).astype(BF16),
        qg=jnp.tile(q_norm, LANES // HEAD_DIM).reshape(1, LANES),
        kg=kg_rows, kg_t=jnp.broadcast_to(kg_rows[:, :, None], (3, KV_WIDTH, tm_prompt)),
        wa=wa, ba=b_a.reshape(1, GLA_QK_WIDTH),
        wpos=jnp.broadcast_to(cmp_pos_w[:, :, None], (2, CMP_BLOCK, LANES)), wc=wc,
        gnorm=gla_norm.reshape(1, GLA_DV),
        wo_n=jnp.concatenate([w_out[h * HEAD_DIM:(h + 1) * HEAD_DIM] for h in SLAB_HEADS], axis=0).astype(BF16),
        wo_g=w_out[Q_WIDTH:].astype(BF16),
        ln_ffn=ln_ffn.reshape(1, D_MODEL),
        w_gate=w_gate.astype(BF16), w_up=w_up.astype(BF16), w_down=w_down.astype(BF16),
    )


def _row_tile(n):
    return math.gcd(n, 512)


def _mix_and_ffn(x, w, attn, s0, bsz, t_len, kv_transposed):
    tm = _row_tile(x.shape[0])
    qn, kv, g, gates, la = _proj_in(x, w, tm, t_len, kv_transposed)
    o_nsa, extra = attn(qn, kv, gates)
    o_gla, s_new = _gla(g, la, w["gnorm"], s0, bsz, t_len)
    y = _out_ffn(x, o_nsa.reshape(-1, Q_WIDTH), o_gla.reshape(-1, GLA_V_WIDTH), w["wo_n"], w["wo_g"], w["ln_ffn"],
                 w["w_gate"], w["w_up"], w["w_down"], tm)
    return y, kv, s_new, extra


def _token_major(a_t, lead):
    n_lead = len(lead)
    a = a_t.reshape(*lead, -1, NSA_KV_HEADS, HEAD_DIM, a_t.shape[-1])
    return a.transpose(*range(n_lead), n_lead + 3, n_lead, n_lead + 1, n_lead + 2)


def kernel(x_prompt, x_sample, cache_nsa_kv, state_nsa_win, state_gla, page_table, ln_mix, w_in, q_norm, k_norm, cmp_pos_w,
           w_a2, b_a, gla_norm, w_out, ln_ffn, w_gate, w_up, w_down):
    bsz, t_len = x_prompt.shape[:2]
    dbsz, ts = x_sample.shape[:2]
    depth = w_in.shape[0]
    n_phys, page_size = cache_nsa_kv.shape[1:3]
    n_win = state_nsa_win.shape[2]
    cache_t = cache_nsa_kv.transpose(0, 1, 3, 4, 5, 2).reshape(depth, n_phys, 4 * KV_WIDTH, page_size)
    win_t = state_nsa_win.transpose(0, 1, 3, 4, 5, 2).reshape(depth, dbsz, 2 * KV_WIDTH, n_win)
    keep_p = min(WINDOW, t_len)
    tm_prompt = _row_tile(bsz * t_len)
    assert t_len % tm_prompt == 0

    yp = x_prompt.reshape(bsz * t_len, D_MODEL)
    ys = x_sample.reshape(dbsz * ts, D_MODEL)
    rows_p, win_p, gla_p, rows_s, win_s, gla_s = [], [], [], [], [], []
    for l in range(depth):
        w = _layer_weights(ln_mix[l], w_in[l], q_norm[l], k_norm[l], cmp_pos_w[l], w_a2[l], b_a[l], gla_norm[l], w_out[l],
                           ln_ffn[l], w_gate[l], w_up[l], w_down[l], tm_prompt, t_len)

        def attn_prompt(qn, kv_t, gates):
            return _nsa_prompt(qn, kv_t, gates, w["wc"], bsz, t_len), None

        def attn_sample(qn, kvn, gates):
            return _nsa_decode(qn, kvn, gates, win_t, cache_t, page_table, w["wpos"], l, dbsz, ts)

        yp, kv_t, st_p, _ = _mix_and_ffn(yp, w, attn_prompt, None, bsz, t_len, True)
        ys, kvn_s, st_s, nw_s = _mix_and_ffn(ys, w, attn_sample, state_gla[l], dbsz, ts, False)

        rows_p.append(kv_t[:, :4 * KV_WIDTH])
        win_p.append(kv_t[:, 4 * KV_WIDTH:, t_len - keep_p:])
        gla_p.append(st_p.astype(state_gla.dtype))
        rows_s.append(kvn_s.reshape(dbsz, ts, 6, NSA_KV_HEADS, HEAD_DIM)[:, :, :4])
        win_s.append(nw_s)
        gla_s.append(st_s.astype(state_gla.dtype))
    lead_p, lead_s = (depth, bsz), (depth, dbsz)
    return (yp.reshape(bsz, t_len, D_MODEL), ys.reshape(dbsz, ts, D_MODEL),
            _token_major(jnp.stack(rows_p), lead_p), _token_major(jnp.stack(win_p), lead_p), jnp.stack(gla_p),
            jnp.stack(rows_s), _token_major(jnp.stack(win_s), lead_s), jnp.stack(gla_s))
```

```python
import functools
import math

import jax
import jax.numpy as jnp
from jax import lax
from jax.experimental import pallas as pl
from jax.experimental.pallas import tpu as pltpu

F32 = jnp.float32
BF16 = jnp.bfloat16

D_MODEL = 1024
NSA_HEADS = 8
NSA_KV_HEADS = 2
NSA_GROUP = NSA_HEADS // NSA_KV_HEADS
HEAD_DIM = 64
CMP_BLOCK = 32
SEL_BLOCK = 64
TOP_K = 16
WINDOW = 512
GLA_HEADS = 4
GLA_DK = 64
GLA_DV = 128
GLA_RANK = 16
GLA_TAU = 16.0
GLA_CHUNK = 64
D_FF = -(-(8 * D_MODEL) // (3 * 256)) * 256
KV_WIDTH = NSA_KV_HEADS * HEAD_DIM
Q_WIDTH = NSA_HEADS * HEAD_DIM
N_GATES = 3 * NSA_HEADS
GLA_QK_WIDTH = GLA_HEADS * GLA_DK
GLA_V_WIDTH = GLA_HEADS * GLA_DV
EPS = 1e-6
NEG_INF = -1e30
FORCE_SCORE = 1e4
BELOW_ALL = -3e38
SCALE = HEAD_DIM ** -0.5
LOG2E = 1.4426950408889634
SLOPES = tuple(tuple(2.0 ** (-8.0 * (k * NSA_GROUP + g + 1) / NSA_HEADS) for g in range(NSA_GROUP))
               for k in range(NSA_KV_HEADS))
SLAB_HEADS = tuple(h for g in range(NSA_GROUP) for h in (g, NSA_GROUP + g))

LANES = 128
SUBLANES = 8
VMEM_LIMIT = 56 * 1024 * 1024
assert KV_WIDTH == LANES and NSA_KV_HEADS == 2

G_WIDTH = 2 * GLA_QK_WIDTH + 2 * GLA_V_WIDTH
MISC_WIDTH = LANES
KV6_WIDTH = 6 * KV_WIDTH


def _dot(a, b):
    return jnp.dot(a, b, preferred_element_type=F32)


def _dot_nt(a, b):
    return lax.dot_general(a, b, (((1,), (1,)), ((), ())), preferred_element_type=F32)


def _dot_tn(a, b):
    return lax.dot_general(a, b, (((0,), (0,)), ((), ())), preferred_element_type=F32)


def _split3(a):
    a0 = a.astype(BF16)
    r = a - a0.astype(F32)
    a1 = r.astype(BF16)
    a2 = (r - a1.astype(F32)).astype(BF16)
    return a0, a1, a2


def _dot_f32(a, b):
    a0, a1, a2 = _split3(a)
    b0, b1, b2 = _split3(b)
    return (_dot(a0, b0) + (_dot(a0, b1) + _dot(a1, b0)) + (_dot(a0, b2) + _dot(a1, b1) + _dot(a2, b0)))


def _sigmoid(x):
    return 1.0 / (1.0 + jnp.exp(-x))


def _low_half(width=LANES):
    return lax.broadcasted_iota(jnp.int32, (1, width), 1) < HEAD_DIM


def _half_lane_rms(x, gain):
    x2 = x * x
    lo = _low_half()
    s_lo = jnp.sum(jnp.where(lo, x2, 0.0), axis=-1, keepdims=True)
    s_hi = jnp.sum(jnp.where(lo, 0.0, x2), axis=-1, keepdims=True)
    ms = jnp.where(lo, s_lo, s_hi) * (1.0 / HEAD_DIM)
    return x * lax.rsqrt(ms + EPS) * gain


def _proj_in_kernel(x_ref, ln_ref, wq_ref, wkv_ref, wg_ref, wm_ref, qg_ref, kg_ref, wa_ref, ba_ref, *rest, kv_transposed):
    if kv_transposed:
        _, _, q_out, rows_out, win_out, g_out, gate_out, la_out = rest
    else:
        q_out, kv_out, g_out, gate_out, la_out = rest
    x = x_ref[...]
    h = x * lax.rsqrt(jnp.mean(x * x, axis=-1, keepdims=True) + EPS) * ln_ref[...]
    hb = h.astype(BF16)

    q = _dot_nt(hb, wq_ref[...])
    for j in range(Q_WIDTH // LANES):
        sl = slice(j * LANES, (j + 1) * LANES)
        q_out[:, sl] = _half_lane_rms(q[:, sl], qg_ref[...])

    if kv_transposed:
        kv = _dot_nt(wkv_ref[...], hb)
        tm = kv.shape[1]
        for s in range(6):
            rows = slice(s * KV_WIDTH, (s + 1) * KV_WIDTH)
            out, s_out = (rows_out, s) if s < 4 else (win_out, s - 4)
            dst = slice(s_out * KV_WIDTH, (s_out + 1) * KV_WIDTH)
            if s % 2 == 0:
                k3 = kv[rows].reshape(NSA_KV_HEADS, HEAD_DIM, tm)
                ms = jnp.mean(k3 * k3, axis=1, keepdims=True)
                out[dst, :] = (k3 * lax.rsqrt(ms + EPS)).reshape(KV_WIDTH, tm) * kg_ref[s // 2]
            else:
                out[dst, :] = kv[rows]
    else:
        kv = _dot_nt(hb, wkv_ref[...])
        for s in range(6):
            sl = slice(s * KV_WIDTH, (s + 1) * KV_WIDTH)
            if s % 2 == 0:
                kv_out[:, sl] = _half_lane_rms(kv[:, sl], kg_ref[s // 2:s // 2 + 1, :])
            else:
                kv_out[:, sl] = kv[:, sl]

    g_out[...] = _dot_nt(hb, wg_ref[...])

    m = _dot_nt(hb, wm_ref[...])
    gate_out[...] = _sigmoid(m)
    z = _dot_f32(m, wa_ref[...]) + ba_ref[...]
    la_out[...] = (jnp.minimum(z, 0.0) - jnp.log1p(jnp.exp(-jnp.abs(z)))) * (1.0 / GLA_TAU)


def _proj_in(x, w, tm, kv_bufs=None, layer=None):
    n = x.shape[0]
    row = lambda width: pl.BlockSpec((tm, width), lambda i: (i, 0))
    full = lambda a: pl.BlockSpec(a.shape, lambda i: (0,) * a.ndim)
    args = [x, w["ln_mix"], w["wq"], w["wkv"], w["wg"], w["wm"], w["qg"], w["kg"] if kv_bufs is None else w["kg_t"],
            w["wa"], w["ba"]]
    in_specs = [row(D_MODEL)] + [full(a) for a in args[1:]]
    tail_specs = [row(G_WIDTH), row(MISC_WIDTH), row(GLA_QK_WIDTH)]
    tail_shapes = [jax.ShapeDtypeStruct((n, width), F32) for width in (G_WIDTH, MISC_WIDTH, GLA_QK_WIDTH)]
    q_shape = jax.ShapeDtypeStruct((n, Q_WIDTH), F32)
    if kv_bufs is None:
        kv_specs = [row(KV6_WIDTH)]
        kv_shapes = [jax.ShapeDtypeStruct((n, KV6_WIDTH), F32)]
        aliases = {}
    else:
        tiles = kv_bufs[0].shape[3] // tm
        kv_specs = [pl.BlockSpec((None, None, b.shape[2], tm), lambda i: (layer, i // tiles, 0, i % tiles)) for b in kv_bufs]
        kv_shapes = [jax.ShapeDtypeStruct(b.shape, b.dtype) for b in kv_bufs]
        aliases = {len(args): 1, len(args) + 1: 2}
        in_specs += [pl.BlockSpec(memory_space=pl.ANY)] * 2
        args += list(kv_bufs)
    return pl.pallas_call(
        functools.partial(_proj_in_kernel, kv_transposed=kv_bufs is not None),
        grid=(n // tm,),
        in_specs=in_specs,
        out_specs=[row(Q_WIDTH)] + kv_specs + tail_specs,
        out_shape=[q_shape] + kv_shapes + tail_shapes,
        input_output_aliases=aliases,
        compiler_params=pltpu.CompilerParams(dimension_semantics=("arbitrary",), vmem_limit_bytes=VMEM_LIMIT),
        name="proj_in",
    )(*args)


def _group_queries(q_tile, kvh):
    keep = _low_half() if kvh == 0 else ~_low_half()
    return jnp.concatenate([jnp.where(keep, q_tile[:, g * LANES:(g + 1) * LANES] * (SCALE * LOG2E), 0.0)
                            for g in range(NSA_GROUP)], axis=0).astype(BF16)


def _masked_neg_dist(dist, ok):
    return jnp.where(ok, -dist.astype(F32), NEG_INF)


def _softmax_rows(s_rows, neg_dist, kvh, tq):
    ps, inv = [], []
    for g in range(NSA_GROUP):
        s = s_rows[g * tq:(g + 1) * tq] + (SLOPES[kvh][g] * LOG2E) * neg_dist
        p = jnp.exp2(s - jnp.max(s, axis=-1, keepdims=True))
        inv.append(1.0 / jnp.sum(p, axis=-1, keepdims=True))
        ps.append(p)
    return jnp.concatenate(ps, axis=0).astype(BF16), jnp.concatenate(inv, axis=0)


def _split_order_block(w):
    lane = lax.broadcasted_iota(jnp.int32, (1, w), 1)
    return jnp.where(lane < w // 2, 2 * lane, 2 * (lane - w // 2) + 1)


def _compressed_probs(s_rows, qpos, kvh, tq, n_cmp, blk_of_lane=None):
    w = s_rows.shape[1]
    lane = lax.broadcasted_iota(jnp.int32, (1, w), 1) if blk_of_lane is None else blk_of_lane
    dist = qpos - ((lane + 1) * CMP_BLOCK - 1)
    valid = (dist >= 0) & (lane < n_cmp)
    nd = _masked_neg_dist(dist, valid)
    ps = []
    imp = jnp.zeros((tq, w), F32)
    for g in range(NSA_GROUP):
        s = s_rows[g * tq:(g + 1) * tq] + (SLOPES[kvh][g] * LOG2E) * nd
        e = jnp.exp2(s - jnp.max(s, axis=-1, keepdims=True))
        p = jnp.where(valid, e / jnp.sum(e, axis=-1, keepdims=True), 0.0)
        imp = imp + p
        ps.append(p)
    return jnp.concatenate(ps, axis=0).astype(BF16), imp


def _select_blocks(imp, qpos, n_sel):
    tq, w = imp.shape
    lane = lax.broadcasted_iota(jnp.int32, (1, w), 1)
    pair = imp + pltpu.roll(imp, w - 1, 1)
    blk = lane >> 1
    cur = qpos >> int(math.log2(SEL_BLOCK))
    is_blk = ((lane & 1) == 0) & (blk < n_sel)
    forced = (blk == 0) | (blk == cur) | (blk == cur - 1)
    score = jnp.where(forced, FORCE_SCORE, jnp.where(blk <= cur, pair, NEG_INF))
    score = jnp.where(is_blk, score, BELOW_ALL)
    rank = jnp.zeros((tq, w), jnp.int32)
    for i in range(n_sel):
        col = score[:, 2 * i:2 * i + 1]
        beats = (col > score) | ((col == score) & (lane > 2 * i))
        rank = rank + jnp.where(beats, 1, 0)
    return jnp.where((rank < min(TOP_K, n_sel)) & is_blk, 1.0, 0.0)


def _select_blocks_t(imp, qpos_row, n_sel):
    tq, w = imp.shape
    pair_t = (imp + pltpu.roll(imp, w // 2, 1)).T
    nb = -(-n_sel // SUBLANES) * SUBLANES
    blk = lax.broadcasted_iota(jnp.int32, (nb, 1), 0)
    cur = qpos_row >> int(math.log2(SEL_BLOCK))
    forced = (blk == 0) | (blk == cur) | (blk == cur - 1)
    score = jnp.where(forced, FORCE_SCORE, jnp.where(blk <= cur, pair_t[0:nb], NEG_INF))
    score = jnp.where(blk < n_sel, score, BELOW_ALL)
    rank = jnp.zeros((nb, tq), jnp.int32)
    for i in range(n_sel):
        row = score[i:i + 1, :]
        beats = (row > score) | ((row == score) & (blk > i))
        rank = rank + jnp.where(beats, 1, 0)
    sel_t = jnp.where((rank < min(TOP_K, n_sel)) & (blk < n_sel), 1.0, 0.0)
    return _pad_rows(sel_t, w).T


def _gated_sum(gates, kvh, tq, branches):
    out = []
    for g in range(NSA_GROUP):
        h = kvh * NSA_GROUP + g
        rows = slice(g * tq, (g + 1) * tq)
        out.append(sum(gates[:, c * NSA_HEADS + h:c * NSA_HEADS + h + 1] * o[rows] for c, o in enumerate(branches)))
    return out


def _store_slabs(o_ref, per_group):
    lo = _low_half()
    for g in range(NSA_GROUP):
        o_ref[:, g * LANES:(g + 1) * LANES] = jnp.where(lo, per_group[0][g], per_group[1][g])


def _nsa_prompt_kernel(q_ref, kv_ref, kvw_ref, gate_ref, wc_ref, o_ref, kc_ref, vc_ref, *, t_len, tq, key_step):
    qi = pl.program_id(1)
    n_cmp = t_len // CMP_BLOCK
    n_sel = -(-t_len // SEL_BLOCK)
    w = kc_ref.shape[1]

    @pl.when(qi == 0)
    def _():
        kc_ref[...] = _dot_f32(kv_ref[0:KV_WIDTH, :], wc_ref[0]).astype(BF16)
        vc_ref[...] = _dot_f32(kv_ref[KV_WIDTH:2 * KV_WIDTH, :], wc_ref[1]).astype(BF16)

    q0 = qi * tq
    qpos = q0 + lax.broadcasted_iota(jnp.int32, (tq, 1), 0)
    qpos_row = q0 + lax.broadcasted_iota(jnp.int32, (1, tq), 1)
    blk_of_lane = _split_order_block(w)
    n_win = min(WINDOW + tq, t_len)
    w_start = pl.multiple_of(jnp.maximum(q0 + tq - n_win, 0), LANES)
    stream = lambda s: slice(s * KV_WIDTH, (s + 1) * KV_WIDTH)

    def body(n_keys):
        q_tile = q_ref[...]
        gates = gate_ref[...]
        dist_s = qpos - lax.broadcasted_iota(jnp.int32, (1, n_keys), 1)
        dist_w = qpos - (w_start + lax.broadcasted_iota(jnp.int32, (1, n_win), 1))
        nd_w = _masked_neg_dist(dist_w, (dist_w >= 0) & (dist_w < WINDOW))
        li = lax.broadcasted_iota(jnp.int32, (w, 1), 0)
        ti = lax.broadcasted_iota(jnp.int32, (1, n_keys), 1)
        expand = jnp.where((ti >> int(math.log2(SEL_BLOCK))) == li, 1.0, 0.0).astype(BF16)
        k_s = kv_ref[stream(2), 0:n_keys].astype(BF16)
        v_s = kv_ref[stream(3), 0:n_keys].astype(BF16)
        k_w = kvw_ref[stream(0), pl.ds(w_start, n_win)].astype(BF16)
        v_w = kvw_ref[stream(1), pl.ds(w_start, n_win)].astype(BF16)
        per_group = []
        for kvh in range(NSA_KV_HEADS):
            qs = _group_queries(q_tile, kvh)
            p_c, imp = _compressed_probs(_dot(qs, kc_ref[...]), qpos, kvh, tq, n_cmp, blk_of_lane)
            o_c = _dot_nt(p_c, vc_ref[...])
            sel = _select_blocks_t(imp, qpos_row, n_sel)
            nd_s = _masked_neg_dist(dist_s, (_dot(sel.astype(BF16), expand) > 0.5) & (dist_s >= 0))
            p_s, inv_s = _softmax_rows(_dot(qs, k_s), nd_s, kvh, tq)
            o_s = _dot_nt(p_s, v_s) * inv_s
            p_w, inv_w = _softmax_rows(_dot(qs, k_w), nd_w, kvh, tq)
            o_w = _dot_nt(p_w, v_w) * inv_w
            per_group.append(_gated_sum(gates, kvh, tq, (o_c, o_s, o_w)))
        _store_slabs(o_ref, per_group)

    n_classes = -(-t_len // key_step)
    for c in range(n_classes):
        n_keys = min((c + 1) * key_step, t_len)

        @pl.when((q0 + tq - 1) // key_step == c)
        def _(n_keys=n_keys):
            body(n_keys)


def _nsa_prompt(qn, rows_t, win_t, gates, wc, layer, bsz, t_len):
    tq = math.gcd(t_len, 128)
    key_step = math.gcd(t_len, 512)
    w = wc.shape[2]
    assert tq == LANES and w == LANES
    kern = functools.partial(_nsa_prompt_kernel, t_len=t_len, tq=tq, key_step=key_step)
    return pl.pallas_call(
        kern,
        grid=(bsz, t_len // tq),
        in_specs=[
            pl.BlockSpec((None, tq, Q_WIDTH), lambda b, i: (b, i, 0)),
            pl.BlockSpec((None, None, 4 * KV_WIDTH, t_len), lambda b, i: (layer, b, 0, 0)),
            pl.BlockSpec((None, None, 2 * KV_WIDTH, t_len), lambda b, i: (layer, b, 0, 0)),
            pl.BlockSpec((None, tq, MISC_WIDTH), lambda b, i: (b, i, 0)),
            pl.BlockSpec(wc.shape, lambda b, i: (0, 0, 0)),
        ],
        out_specs=pl.BlockSpec((None, tq, Q_WIDTH), lambda b, i: (b, i, 0)),
        out_shape=jax.ShapeDtypeStruct((bsz, t_len, Q_WIDTH), F32),
        scratch_shapes=[pltpu.VMEM((KV_WIDTH, w), BF16), pltpu.VMEM((KV_WIDTH, w), BF16)],
        compiler_params=pltpu.CompilerParams(dimension_semantics=("arbitrary", "arbitrary"), vmem_limit_bytes=VMEM_LIMIT),
        name="nsa_prompt",
    )(qn.reshape(bsz, t_len, Q_WIDTH), rows_t, win_t, gates.reshape(bsz, t_len, MISC_WIDTH), wc)


def _pad_rows(a, n):
    return jnp.concatenate([a, jnp.zeros((n - a.shape[0], a.shape[1]), a.dtype)], axis=0)


def _nsa_decode_kernel(pt_ref, q_ref, kv_ref, gate_ref, win_ref, wpos_ref, *rest, ts, past_len, page_size, n_pages, w):
    page_refs = rest[:n_pages]
    o_ref, win_out_ref, s_ref, p_ref = rest[n_pages:]
    del pt_ref
    n_cmp = past_len // CMP_BLOCK
    n_sel = past_len // SEL_BLOCK + 1
    n_keys = past_len + page_size
    n_win = win_ref.shape[1]
    rows_q = NSA_GROUP * ts
    stream = lambda s: slice(s * KV_WIDTH, (s + 1) * KV_WIDTH)

    new = kv_ref[...]
    q_tile = q_ref[...]
    gates = gate_ref[...]
    qs = jnp.concatenate([_group_queries(q_tile, kvh) for kvh in range(NSA_KV_HEADS)], axis=0)
    qpos = past_len + lax.broadcasted_iota(jnp.int32, (ts, 1), 0)

    per_pair = 2 * page_size // CMP_BLOCK
    wk = jnp.concatenate([wpos_ref[0]] * per_pair, axis=0).reshape(per_pair, CMP_BLOCK, KV_WIDTH)
    wv = jnp.concatenate([wpos_ref[1]] * per_pair, axis=0).reshape(per_pair, CMP_BLOCK, KV_WIDTH)
    kcs, vcs = [], []
    for j in range(0, n_pages, 2):
        k2 = jnp.concatenate([page_refs[j][stream(0), :].T, page_refs[j + 1][stream(0), :].T], axis=0)
        v2 = jnp.concatenate([page_refs[j][stream(1), :].T, page_refs[j + 1][stream(1), :].T], axis=0)
        kcs.append(jnp.sum(k2.reshape(per_pair, CMP_BLOCK, KV_WIDTH) * wk, axis=1))
        vcs.append(jnp.sum(v2.reshape(per_pair, CMP_BLOCK, KV_WIDTH) * wv, axis=1))
    kc = _pad_rows(jnp.concatenate(kcs, axis=0), w).astype(BF16)
    vc = _pad_rows(jnp.concatenate(vcs, axis=0), w).astype(BF16)

    s_c = _dot_nt(qs, kc)
    p_cs, sels = [], []
    for kvh in range(NSA_KV_HEADS):
        p_c, imp = _compressed_probs(s_c[kvh * rows_q:(kvh + 1) * rows_q], qpos, kvh, ts, n_cmp)
        p_cs.append(p_c)
        sels.append(_select_blocks(imp, qpos, n_sel))
    o_c = _dot(jnp.concatenate(p_cs, axis=0), vc)

    new_k = _pad_rows(new[:, stream(2)], page_size).astype(BF16)
    new_v = _pad_rows(new[:, stream(3)], page_size).astype(BF16)
    for j in range(n_pages):
        s_ref[:, j * page_size:(j + 1) * page_size] = _dot(qs, page_refs[j][stream(2), :].astype(BF16))
    s_ref[:, past_len:n_keys] = _dot_nt(qs, new_k)
    dist_s = qpos - lax.broadcasted_iota(jnp.int32, (1, n_keys), 1)
    blocks_per_page = page_size // SEL_BLOCK
    page_lane_blk = lax.broadcasted_iota(jnp.int32, (1, page_size), 1) >> int(math.log2(SEL_BLOCK))
    invs = []
    for kvh in range(NSA_KV_HEADS):
        sel = sels[kvh]
        pieces = []
        for j in range(n_pages + 1):
            piece = jnp.zeros((ts, page_size), F32)
            for r in range(blocks_per_page):
                blk = j * blocks_per_page + r
                if blk < n_sel:
                    piece = jnp.where(page_lane_blk == r, sel[:, 2 * blk:2 * blk + 1], piece)
            pieces.append(piece)
        keymask = jnp.concatenate(pieces, axis=1)
        nd_s = _masked_neg_dist(dist_s, (keymask > 0.5) & (dist_s >= 0))
        p_s, inv_s = _softmax_rows(s_ref[kvh * rows_q:(kvh + 1) * rows_q, :], nd_s, kvh, ts)
        p_ref[kvh * rows_q:(kvh + 1) * rows_q, :] = p_s
        invs.append(inv_s)
    o_s = _dot(p_ref[:, past_len:n_keys], new_v)
    for j in range(n_pages):
        o_s = o_s + _dot_nt(p_ref[:, j * page_size:(j + 1) * page_size], page_refs[j][stream(3), :].astype(BF16))
    o_s = o_s * jnp.concatenate(invs, axis=0)

    new_kw = _pad_rows(new[:, stream(4)], LANES).astype(BF16)
    new_vw = _pad_rows(new[:, stream(5)], LANES).astype(BF16)
    s_w = jnp.concatenate([_dot(qs, win_ref[0:KV_WIDTH, :].astype(BF16)), _dot_nt(qs, new_kw)], axis=1)
    wpos = jnp.concatenate([past_len - n_win + lax.broadcasted_iota(jnp.int32, (1, n_win), 1),
                            past_len + lax.broadcasted_iota(jnp.int32, (1, LANES), 1)], axis=1)
    dist_w = qpos - wpos
    nd_w = _masked_neg_dist(dist_w, (dist_w >= 0) & (dist_w < WINDOW))
    p_ws, inv_ws = [], []
    for kvh in range(NSA_KV_HEADS):
        p_w, inv_w = _softmax_rows(s_w[kvh * rows_q:(kvh + 1) * rows_q], nd_w, kvh, ts)
        p_ws.append(p_w)
        inv_ws.append(inv_w)
    p_w = jnp.concatenate(p_ws, axis=0)
    o_w = (_dot_nt(p_w[:, 0:n_win], win_ref[KV_WIDTH:2 * KV_WIDTH, :].astype(BF16)) + _dot(p_w[:, n_win:], new_vw))
    o_w = o_w * jnp.concatenate(inv_ws, axis=0)

    per_group = [_gated_sum(gates, kvh, ts, [o[kvh * rows_q:(kvh + 1) * rows_q] for o in (o_c, o_s, o_w)])
                 for kvh in range(NSA_KV_HEADS)]
    _store_slabs(o_ref, per_group)

    new_t = _pad_rows(new[:, 4 * KV_WIDTH:6 * KV_WIDTH], LANES).T
    new_t = pltpu.roll(new_t, LANES - ts, 1)
    shifted = pltpu.roll(win_ref[...], n_win - ts, 1)
    tail_lane = lax.broadcasted_iota(jnp.int32, (1, LANES), 1)
    win_out_ref[:, 0:n_win - LANES] = shifted[:, 0:n_win - LANES]
    win_out_ref[:, n_win - LANES:n_win] = jnp.where(tail_lane >= LANES - ts, new_t, shifted[:, n_win - LANES:n_win])


def _nsa_decode(qn, kvn, gates, win_t, cache_t, page_table, wpos_rows, layer, dbsz, ts):
    n_pages = page_table.shape[1]
    page_size = cache_t.shape[3]
    past_len = n_pages * page_size
    n_win = win_t.shape[3]
    assert ts < CMP_BLOCK and ts % SUBLANES == 0 and page_size == LANES and n_pages % 2 == 0
    assert n_win == WINDOW and past_len >= WINDOW
    n_cmp = past_len // CMP_BLOCK
    n_sel = past_len // SEL_BLOCK + 1
    w = -(-max(n_cmp, 2 * n_sel) // LANES) * LANES
    n_keys = past_len + page_size
    kern = functools.partial(_nsa_decode_kernel, ts=ts, past_len=past_len, page_size=page_size, n_pages=n_pages, w=w)

    def page_spec(j):
        return pl.BlockSpec((None, None, 4 * KV_WIDTH, page_size), lambda b, pt: (layer, pt[b, j], 0, 0))

    grid_spec = pltpu.PrefetchScalarGridSpec(
        num_scalar_prefetch=1,
        grid=(dbsz,),
        in_specs=[
            pl.BlockSpec((None, ts, Q_WIDTH), lambda b, pt: (b, 0, 0)),
            pl.BlockSpec((None, ts, KV6_WIDTH), lambda b, pt: (b, 0, 0)),
            pl.BlockSpec((None, ts, MISC_WIDTH), lambda b, pt: (b, 0, 0)),
            pl.BlockSpec((None, None, 2 * KV_WIDTH, n_win), lambda b, pt: (layer, b, 0, 0)),
            pl.BlockSpec(wpos_rows.shape, lambda b, pt: (0, 0, 0)),
        ] + [page_spec(j) for j in range(n_pages)],
        out_specs=[
            pl.BlockSpec((None, ts, Q_WIDTH), lambda b, pt: (b, 0, 0)),
            pl.BlockSpec((None, 2 * KV_WIDTH, n_win), lambda b, pt: (b, 0, 0)),
        ],
        scratch_shapes=[pltpu.VMEM((NSA_HEADS * ts, n_keys), F32), pltpu.VMEM((NSA_HEADS * ts, n_keys), BF16)],
    )
    return pl.pallas_call(
        kern,
        grid_spec=grid_spec,
        out_shape=[jax.ShapeDtypeStruct((dbsz, ts, Q_WIDTH), F32), jax.ShapeDtypeStruct((dbsz, 2 * KV_WIDTH, n_win), F32)],
        compiler_params=pltpu.CompilerParams(dimension_semantics=("arbitrary",), vmem_limit_bytes=VMEM_LIMIT),
        name="nsa_decode",
    )(page_table, qn.reshape(dbsz, ts, Q_WIDTH), kvn.reshape(dbsz, ts, KV6_WIDTH), gates.reshape(dbsz, ts, MISC_WIDTH),
      win_t, wpos_rows, *([cache_t] * n_pages))


def _gla_kernel(*refs, t_len, chunk, has_s0):
    if has_s0:
        q_ref, k_ref, v_ref, gg_ref, la_ref, gn_ref, s0_ref, o_ref, s_out_ref, st_ref, u_ref, d_ref, sb_ref, qe_ref = refs
    else:
        q_ref, k_ref, v_ref, gg_ref, la_ref, gn_ref, o_ref, s_out_ref, st_ref, u_ref, d_ref, sb_ref, qe_ref = refs
    dk2, dv2 = 2 * GLA_DK, 2 * GLA_DV
    n_chunks = t_len // chunk
    per_group = math.gcd(n_chunks, 4)
    rows_g = per_group * chunk
    n_groups = n_chunks // per_group
    rp = max(rows_g, LANES)
    pad = rp - rows_g

    rr = lax.broadcasted_iota(jnp.int32, (dv2, dk2), 0) // GLA_DV
    cc = lax.broadcasted_iota(jnp.int32, (dv2, dk2), 1) // GLA_DK
    diag = rr == cc
    if has_s0:
        z = jnp.zeros((GLA_DK, GLA_DV), F32)
        s_full = jnp.concatenate([jnp.concatenate([s0_ref[0], z], axis=1), jnp.concatenate([z, s0_ref[1]], axis=1)], axis=0)
        st_ref[...] = s_full.T
    else:
        st_ref[...] = jnp.zeros((dv2, dk2), F32)

    lane_head = lax.broadcasted_iota(jnp.int32, (1, dk2), 1) // GLA_DK
    trow = lax.broadcasted_iota(jnp.int32, (rows_g, rp), 0)
    tcol = lax.broadcasted_iota(jnp.int32, (rows_g, rp), 1)
    causal = (tcol <= trow) & (tcol // chunk == trow // chunk)
    tril = jnp.where(causal, 1.0, 0.0).astype(BF16)
    prow_chunk = lax.broadcasted_iota(jnp.int32, (rp, 1), 0) // chunk
    mid = chunk // 2
    gn = gn_ref[...]

    def pad_rows(a):
        return a if pad == 0 else jnp.concatenate([a, jnp.zeros((pad, a.shape[1]), a.dtype)], axis=0)

    def per_chunk_row(cum, r):
        return jnp.concatenate([jnp.broadcast_to(cum[c * chunk + r:c * chunk + r + 1, :], (chunk, dk2))
                                for c in range(per_group)], axis=0)

    def group_local(gi, carry):
        r0 = pl.multiple_of(gi * rows_g, rows_g)
        rows = pl.ds(r0, rows_g)
        q = q_ref[rows, :] * (GLA_DK ** -0.5)
        k = k_ref[rows, :]
        la0, la1, la2 = _split3(pad_rows(la_ref[rows, :]))
        cum = _dot(tril, la0) + _dot(tril, la1) + _dot(tril, la2)
        m = per_chunk_row(cum, mid)
        last = per_chunk_row(cum, chunk - 1)
        qe_ref[rows, :] = q * jnp.exp(cum)
        qs = q * jnp.exp(cum - m)
        ks = pad_rows((k * jnp.exp(m - cum)).astype(BF16))
        vp = pad_rows(v_ref[rows, :].astype(BF16))
        intra = []
        for h in range(2):
            a = _dot_nt(jnp.where(lane_head == h, qs, 0.0).astype(BF16), ks)
            a = jnp.where(causal, a, 0.0).astype(BF16)
            intra.append(_dot(a, vp[:, h * GLA_DV:(h + 1) * GLA_DV]))
        o_ref[rows, :] = jnp.concatenate(intra, axis=1)
        kd = pad_rows(k * jnp.exp(last - cum))
        for c in range(per_group):
            ci = gi * per_group + c
            kd_c = jnp.where(prow_chunk == c, kd, 0.0).astype(BF16)
            u_ref[ci] = jnp.where(diag, _dot_tn(vp, kd_c), 0.0)
            d_ref[ci] = jnp.exp(last[c * chunk:c * chunk + SUBLANES, :])
        return carry

    def chunk_state(ci, carry):
        st = st_ref[...]
        sb_ref[ci] = st.astype(BF16)
        st_ref[...] = st * d_ref[ci][0:1, :] + u_ref[ci]
        return carry

    def group_output(gi, carry):
        r0 = pl.multiple_of(gi * rows_g, rows_g)
        rows = pl.ds(r0, rows_g)
        inter = [_dot_nt(qe_ref[pl.ds(pl.multiple_of(r0 + c * chunk, chunk), chunk), :].astype(BF16),
                         sb_ref[gi * per_group + c]) for c in range(per_group)]
        o = o_ref[rows, :] + jnp.concatenate(inter, axis=0)
        gg = gg_ref[rows, :]
        outs = []
        for h in range(2):
            oh = o[:, h * GLA_DV:(h + 1) * GLA_DV]
            y = oh * lax.rsqrt(jnp.mean(oh * oh, axis=-1, keepdims=True) + EPS) * gn
            gh = gg[:, h * GLA_DV:(h + 1) * GLA_DV]
            outs.append(y * (gh * _sigmoid(gh)))
        o_ref[rows, :] = jnp.concatenate(outs, axis=1)
        return carry

    lax.fori_loop(0, n_groups, group_local, 0)
    lax.fori_loop(0, n_chunks, chunk_state, 0)
    lax.fori_loop(0, n_groups, group_output, 0)
    s_fin = st_ref[...].T
    s_out_ref[0] = s_fin[0:GLA_DK, 0:GLA_DV]
    s_out_ref[1] = s_fin[GLA_DK:dk2, GLA_DV:dv2]


def _gla(g, la, gnorm, s0, bsz, t_len):
    chunk = math.gcd(t_len, GLA_CHUNK)
    has_s0 = s0 is not None
    kern = functools.partial(_gla_kernel, t_len=t_len, chunk=chunk, has_s0=has_s0)
    qk_blk = lambda off: pl.BlockSpec((None, t_len, 2 * GLA_DK), lambda b, p: (b, 0, off + p))
    v_blk = lambda off: pl.BlockSpec((None, t_len, 2 * GLA_DV), lambda b, p: (b, 0, off + p))
    g3 = g.reshape(bsz, t_len, G_WIDTH)
    n_qk = GLA_QK_WIDTH // (2 * GLA_DK)
    in_specs = [qk_blk(0), qk_blk(n_qk), v_blk(n_qk), v_blk(n_qk + GLA_V_WIDTH // (2 * GLA_DV)),
                pl.BlockSpec((None, t_len, 2 * GLA_DK), lambda b, p: (b, 0, p)),
                pl.BlockSpec(gnorm.shape, lambda b, p: (0, 0))]
    args = [g3, g3, g3, g3, la.reshape(bsz, t_len, GLA_QK_WIDTH), gnorm]
    if has_s0:
        in_specs.append(pl.BlockSpec((None, 2, GLA_DK, GLA_DV), lambda b, p: (b, p, 0, 0)))
        args.append(s0)
    return pl.pallas_call(
        kern,
        grid=(bsz, GLA_HEADS // 2),
        in_specs=in_specs,
        out_specs=[pl.BlockSpec((None, t_len, 2 * GLA_DV), lambda b, p: (b, 0, p)),
                   pl.BlockSpec((None, 2, GLA_DK, GLA_DV), lambda b, p: (b, p, 0, 0))],
        out_shape=[jax.ShapeDtypeStruct((bsz, t_len, GLA_V_WIDTH), F32),
                   jax.ShapeDtypeStruct((bsz, GLA_HEADS, GLA_DK, GLA_DV), F32)],
        scratch_shapes=[pltpu.VMEM((2 * GLA_DV, 2 * GLA_DK), F32),
                        pltpu.VMEM((t_len // chunk, 2 * GLA_DV, 2 * GLA_DK), F32),
                        pltpu.VMEM((t_len // chunk, SUBLANES, 2 * GLA_DK), F32),
                        pltpu.VMEM((t_len // chunk, 2 * GLA_DV, 2 * GLA_DK), BF16),
                        pltpu.VMEM((t_len, 2 * GLA_DK), F32)],
        compiler_params=pltpu.CompilerParams(dimension_semantics=("arbitrary", "arbitrary"), vmem_limit_bytes=VMEM_LIMIT),
        name="gla",
    )(*args)


FF_STEPS = 2
FF_CHUNK = D_FF // FF_STEPS
assert FF_CHUNK * FF_STEPS == D_FF and FF_CHUNK % LANES == 0


def _out_ffn_kernel(x_ref, on_ref, og_ref, wo_n_ref, wo_g_ref, ln_ref, wg_ref, wu_ref, wd_ref, y_ref, h_ref):
    j = pl.program_id(1)

    @pl.when(j == 0)
    def _():
        x1 = x_ref[...] + _dot(on_ref[...].astype(BF16), wo_n_ref[...]) + _dot(og_ref[...].astype(BF16), wo_g_ref[...])
        h_ref[...] = (x1 * lax.rsqrt(jnp.mean(x1 * x1, axis=-1, keepdims=True) + EPS) * ln_ref[...]).astype(BF16)
        y_ref[...] = x1

    h = h_ref[...]
    gate = _dot(h, wg_ref[...])
    up = _dot(h, wu_ref[...])
    y_ref[...] += _dot((gate * _sigmoid(gate) * up).astype(BF16), wd_ref[...])


def _out_ffn(x, o_nsa, o_gla, wo_n, wo_g, ln, wg, wu, wd, tm):
    n = x.shape[0]
    row = lambda w: pl.BlockSpec((tm, w), lambda i, j: (i, 0))
    full = lambda a: pl.BlockSpec(a.shape, lambda i, j: (0,) * a.ndim)
    return pl.pallas_call(
        _out_ffn_kernel,
        grid=(n // tm, FF_STEPS),
        in_specs=[row(D_MODEL), row(Q_WIDTH), row(GLA_V_WIDTH), full(wo_n), full(wo_g), full(ln),
                  pl.BlockSpec((D_MODEL, FF_CHUNK), lambda i, j: (0, j)),
                  pl.BlockSpec((D_MODEL, FF_CHUNK), lambda i, j: (0, j)),
                  pl.BlockSpec((FF_CHUNK, D_MODEL), lambda i, j: (j, 0))],
        out_specs=row(D_MODEL),
        out_shape=jax.ShapeDtypeStruct((n, D_MODEL), F32),
        scratch_shapes=[pltpu.VMEM((tm, D_MODEL), BF16)],
        compiler_params=pltpu.CompilerParams(dimension_semantics=("arbitrary", "arbitrary"), vmem_limit_bytes=VMEM_LIMIT),
        name="out_ffn",
    )(x, o_nsa, o_gla, wo_n, wo_g, ln, wg, wu, wd)


def _layer_weights(ln_mix, w_in, q_norm, k_norm, cmp_pos_w, w_a2, b_a, gla_norm, w_out, ln_ffn, w_gate, w_up, w_down,
                   tm_prompt, t_len):
    w_in_t = w_in.T
    o = 0
    parts = []
    for width in (Q_WIDTH, KV6_WIDTH, N_GATES, GLA_QK_WIDTH, GLA_QK_WIDTH, GLA_V_WIDTH, GLA_V_WIDTH, GLA_RANK):
        parts.append(w_in_t[o:o + width])
        o += width
    wq, wkv, wgate, wgq, wgk, wgv, wgg, wga = parts
    wq = jnp.concatenate([wq[h * HEAD_DIM:(h + 1) * HEAD_DIM] for h in SLAB_HEADS], axis=0)
    pad = jnp.zeros((MISC_WIDTH - N_GATES - GLA_RANK, D_MODEL), w_in.dtype)
    wa = jnp.zeros((MISC_WIDTH, GLA_QK_WIDTH), F32).at[N_GATES:N_GATES + GLA_RANK].set(w_a2)
    n_cmp = t_len // CMP_BLOCK
    w_lanes = -(-max(n_cmp, 2 * (-(-t_len // SEL_BLOCK))) // LANES) * LANES
    tok = jnp.arange(t_len)
    lane = jnp.arange(w_lanes)
    blk_of_lane = jnp.where(lane < w_lanes // 2, 2 * lane, 2 * (lane - w_lanes // 2) + 1)
    in_blk = (tok[:, None] // CMP_BLOCK == blk_of_lane[None, :]) & (tok[:, None] < n_cmp * CMP_BLOCK)
    wc = jnp.where(in_blk[None], cmp_pos_w[:, tok % CMP_BLOCK][:, :, None], 0.0)
    kg_rows = jnp.tile(k_norm, (1, NSA_KV_HEADS))
    return dict(
        ln_mix=ln_mix.reshape(1, D_MODEL),
        wq=wq.astype(BF16), wkv=wkv.astype(BF16),
        wg=jnp.concatenate([wgq, wgk, wgv, wgg], axis=0).astype(BF16),
        wm=jnp.concatenate([wgate, wga, pad], axis=0).astype(BF16),
        qg=jnp.tile(q_norm, LANES // HEAD_DIM).reshape(1, LANES),
        kg=kg_rows, kg_t=jnp.broadcast_to(kg_rows[:, :, None], (3, KV_WIDTH, tm_prompt)),
        wa=wa, ba=b_a.reshape(1, GLA_QK_WIDTH),
        wpos=jnp.broadcast_to(cmp_pos_w[:, :, None], (2, CMP_BLOCK, LANES)), wc=wc,
        gnorm=gla_norm.reshape(1, GLA_DV),
        wo_n=jnp.concatenate([w_out[h * HEAD_DIM:(h + 1) * HEAD_DIM] for h in SLAB_HEADS], axis=0).astype(BF16),
        wo_g=w_out[Q_WIDTH:].astype(BF16),
        ln_ffn=ln_ffn.reshape(1, D_MODEL),
        w_gate=w_gate.astype(BF16), w_up=w_up.astype(BF16), w_down=w_down.astype(BF16),
    )


def _row_tile(n):
    return math.gcd(n, 512)


def _mix_and_ffn(x, w, attn, s0, bsz, t_len, kv_bufs=None, layer=None):
    tm = _row_tile(x.shape[0])
    qn, *kv, g, gates, la = _proj_in(x, w, tm, kv_bufs, layer)
    o_nsa, extra = attn(qn, kv, gates)
    o_gla, s_new = _gla(g, la, w["gnorm"], s0, bsz, t_len)
    y = _out_ffn(x, o_nsa.reshape(-1, Q_WIDTH), o_gla.reshape(-1, GLA_V_WIDTH), w["wo_n"], w["wo_g"], w["ln_ffn"],
                 w["w_gate"], w["w_up"], w["w_down"], tm)
    return y, kv, s_new, extra


def _token_major(a_t, lead):
    n_lead = len(lead)
    a = a_t.reshape(*lead, -1, NSA_KV_HEADS, HEAD_DIM, a_t.shape[-1])
    return a.transpose(*range(n_lead), n_lead + 3, n_lead, n_lead + 1, n_lead + 2)


def kernel(x_prompt, x_sample, cache_nsa_kv, state_nsa_win, state_gla, page_table, ln_mix, w_in, q_norm, k_norm, cmp_pos_w,
           w_a2, b_a, gla_norm, w_out, ln_ffn, w_gate, w_up, w_down):
    bsz, t_len = x_prompt.shape[:2]
    dbsz, ts = x_sample.shape[:2]
    depth = w_in.shape[0]
    n_phys, page_size = cache_nsa_kv.shape[1:3]
    n_win = state_nsa_win.shape[2]
    cache_t = cache_nsa_kv.transpose(0, 1, 3, 4, 5, 2).reshape(depth, n_phys, 4 * KV_WIDTH, page_size)
    win_t = state_nsa_win.transpose(0, 1, 3, 4, 5, 2).reshape(depth, dbsz, 2 * KV_WIDTH, n_win)
    keep_p = min(WINDOW, t_len)
    tm_prompt = _row_tile(bsz * t_len)
    assert t_len % tm_prompt == 0

    yp = x_prompt.reshape(bsz * t_len, D_MODEL)
    ys = x_sample.reshape(dbsz * ts, D_MODEL)
    kv_bufs = [jnp.zeros((depth, bsz, 4 * KV_WIDTH, t_len), F32), jnp.zeros((depth, bsz, 2 * KV_WIDTH, t_len), F32)]
    gla_p, rows_s, win_s, gla_s = [], [], [], []
    for l in range(depth):
        w = _layer_weights(ln_mix[l], w_in[l], q_norm[l], k_norm[l], cmp_pos_w[l], w_a2[l], b_a[l], gla_norm[l], w_out[l],
                           ln_ffn[l], w_gate[l], w_up[l], w_down[l], tm_prompt, t_len)

        def attn_prompt(qn, kv, gates):
            return _nsa_prompt(qn, kv[0], kv[1], gates, w["wc"], l, bsz, t_len), None

        def attn_sample(qn, kv, gates):
            return _nsa_decode(qn, kv[0], gates, win_t, cache_t, page_table, w["wpos"], l, dbsz, ts)

        yp, kv_bufs, st_p, _ = _mix_and_ffn(yp, w, attn_prompt, None, bsz, t_len, kv_bufs, l)
        ys, (kvn_s,), st_s, nw_s = _mix_and_ffn(ys, w, attn_sample, state_gla[l], dbsz, ts)

        gla_p.append(st_p.astype(state_gla.dtype))
        rows_s.append(kvn_s.reshape(dbsz, ts, 6, NSA_KV_HEADS, HEAD_DIM)[:, :, :4])
        win_s.append(nw_s)
        gla_s.append(st_s.astype(state_gla.dtype))
    lead_p, lead_s = (depth, bsz), (depth, dbsz)
    return (yp.reshape(bsz, t_len, D_MODEL), ys.reshape(dbsz, ts, D_MODEL),
            _token_major(kv_bufs[0], lead_p), _token_major(kv_bufs[1][:, :, :, t_len - keep_p:], lead_p), jnp.stack(gla_p),
            jnp.stack(rows_s), _token_major(jnp.stack(win_s), lead_s), jnp.stack(gla_s))
```

```python
import functools
import math

import jax
import jax.numpy as jnp
import numpy as np
from jax import lax
from jax.experimental import pallas as pl
from jax.experimental.pallas import tpu as pltpu

F32 = jnp.float32
BF16 = jnp.bfloat16

D_MODEL = 1024
NSA_HEADS = 8
NSA_KV_HEADS = 2
NSA_GROUP = NSA_HEADS // NSA_KV_HEADS
HEAD_DIM = 64
CMP_BLOCK = 32
SEL_BLOCK = 64
TOP_K = 16
WINDOW = 512
GLA_HEADS = 4
GLA_DK = 64
GLA_DV = 128
GLA_RANK = 16
GLA_TAU = 16.0
GLA_CHUNK = 64
D_FF = -(-(8 * D_MODEL) // (3 * 256)) * 256
KV_WIDTH = NSA_KV_HEADS * HEAD_DIM
Q_WIDTH = NSA_HEADS * HEAD_DIM
N_GATES = 3 * NSA_HEADS
GLA_QK_WIDTH = GLA_HEADS * GLA_DK
GLA_V_WIDTH = GLA_HEADS * GLA_DV
EPS = 1e-6
NEG_INF = -1e30
FORCE_SCORE = 1e4
BELOW_ALL = -3e38
SCALE = HEAD_DIM ** -0.5
LOG2E = 1.4426950408889634
SLOPES = tuple(tuple(2.0 ** (-8.0 * (k * NSA_GROUP + g + 1) / NSA_HEADS) for g in range(NSA_GROUP))
               for k in range(NSA_KV_HEADS))
SLAB_HEADS = tuple(h for g in range(NSA_GROUP) for h in (g, NSA_GROUP + g))

LANES = 128
SUBLANES = 8
VMEM_LIMIT = 56 * 1024 * 1024
assert KV_WIDTH == LANES and NSA_KV_HEADS == 2

G_WIDTH = 2 * GLA_QK_WIDTH + 2 * GLA_V_WIDTH
MISC_WIDTH = LANES
KV6_WIDTH = 6 * KV_WIDTH


def _dot(a, b):
    return jnp.dot(a, b, preferred_element_type=F32)


def _dot_nt(a, b):
    return lax.dot_general(a, b, (((1,), (1,)), ((), ())), preferred_element_type=F32)


def _dot_tn(a, b):
    return lax.dot_general(a, b, (((0,), (0,)), ((), ())), preferred_element_type=F32)


def _split3(a):
    a0 = a.astype(BF16)
    r = a - a0.astype(F32)
    a1 = r.astype(BF16)
    a2 = (r - a1.astype(F32)).astype(BF16)
    return a0, a1, a2


def _dot_f32(a, b):
    a0, a1, a2 = _split3(a)
    b0, b1, b2 = _split3(b)
    return (_dot(a0, b0) + (_dot(a0, b1) + _dot(a1, b0)) + (_dot(a0, b2) + _dot(a1, b1) + _dot(a2, b0)))


def _sigmoid(x):
    return 1.0 / (1.0 + jnp.exp(-x))


def _low_half(width=LANES):
    return lax.broadcasted_iota(jnp.int32, (1, width), 1) < HEAD_DIM


def _half_lane_rms(x, gain):
    x2 = x * x
    lo = _low_half()
    s_lo = jnp.sum(jnp.where(lo, x2, 0.0), axis=-1, keepdims=True)
    s_hi = jnp.sum(jnp.where(lo, 0.0, x2), axis=-1, keepdims=True)
    ms = jnp.where(lo, s_lo, s_hi) * (1.0 / HEAD_DIM)
    return x * lax.rsqrt(ms + EPS) * gain


def _proj_in_kernel(x_ref, ln_ref, wq_ref, wkv_ref, wg_ref, wm_ref, qg_ref, kg_ref, wa_ref, ba_ref, *rest, kv_transposed):
    if kv_transposed:
        _, _, q_out, rows_out, win_out, g_out, gate_out, la_out = rest
    else:
        q_out, kv_out, g_out, gate_out, la_out = rest
    x = x_ref[...]
    h = x * lax.rsqrt(jnp.mean(x * x, axis=-1, keepdims=True) + EPS) * ln_ref[...]
    hb = h.astype(BF16)

    q = _dot_nt(hb, wq_ref[...])
    for j in range(Q_WIDTH // LANES):
        sl = slice(j * LANES, (j + 1) * LANES)
        q_out[:, sl] = _half_lane_rms(q[:, sl], qg_ref[...])

    if kv_transposed:
        kv = _dot_nt(wkv_ref[...], hb)
        tm = kv.shape[1]
        for s in range(6):
            rows = slice(s * KV_WIDTH, (s + 1) * KV_WIDTH)
            out, s_out = (rows_out, s) if s < 4 else (win_out, s - 4)
            dst = slice(s_out * KV_WIDTH, (s_out + 1) * KV_WIDTH)
            if s % 2 == 0:
                k3 = kv[rows].reshape(NSA_KV_HEADS, HEAD_DIM, tm)
                ms = jnp.mean(k3 * k3, axis=1, keepdims=True)
                out[dst, :] = (k3 * lax.rsqrt(ms + EPS)).reshape(KV_WIDTH, tm) * kg_ref[s // 2]
            else:
                out[dst, :] = kv[rows]
    else:
        kv = _dot_nt(hb, wkv_ref[...])
        for s in range(6):
            sl = slice(s * KV_WIDTH, (s + 1) * KV_WIDTH)
            if s % 2 == 0:
                kv_out[:, sl] = _half_lane_rms(kv[:, sl], kg_ref[s // 2:s // 2 + 1, :])
            else:
                kv_out[:, sl] = kv[:, sl]

    g_out[...] = _dot_nt(hb, wg_ref[...])

    m = _dot_nt(hb, wm_ref[...])
    gate_out[...] = _sigmoid(m)
    z = _dot_f32(m, wa_ref[...]) + ba_ref[...]
    la_out[...] = (jnp.minimum(z, 0.0) - jnp.log1p(jnp.exp(-jnp.abs(z)))) * (1.0 / GLA_TAU)


def _proj_in(x, w, tm, kv_bufs=None, layer=None):
    n = x.shape[0]
    row = lambda width: pl.BlockSpec((tm, width), lambda i: (i, 0))
    full = lambda a: pl.BlockSpec(a.shape, lambda i: (0,) * a.ndim)
    args = [x, w["ln_mix"], w["wq"], w["wkv"], w["wg"], w["wm"], w["qg"], w["kg"] if kv_bufs is None else w["kg_t"],
            w["wa"], w["ba"]]
    in_specs = [row(D_MODEL)] + [full(a) for a in args[1:]]
    tail_specs = [row(G_WIDTH), row(MISC_WIDTH), row(GLA_QK_WIDTH)]
    tail_shapes = [jax.ShapeDtypeStruct((n, width), F32) for width in (G_WIDTH, MISC_WIDTH, GLA_QK_WIDTH)]
    q_shape = jax.ShapeDtypeStruct((n, Q_WIDTH), F32)
    if kv_bufs is None:
        kv_specs = [row(KV6_WIDTH)]
        kv_shapes = [jax.ShapeDtypeStruct((n, KV6_WIDTH), F32)]
        aliases = {}
    else:
        tiles = kv_bufs[0].shape[3] // tm
        kv_specs = [pl.BlockSpec((None, None, b.shape[2], tm), lambda i: (layer, i // tiles, 0, i % tiles)) for b in kv_bufs]
        kv_shapes = [jax.ShapeDtypeStruct(b.shape, b.dtype) for b in kv_bufs]
        aliases = {len(args): 1, len(args) + 1: 2}
        in_specs += [pl.BlockSpec(memory_space=pl.ANY)] * 2
        args += list(kv_bufs)
    return pl.pallas_call(
        functools.partial(_proj_in_kernel, kv_transposed=kv_bufs is not None),
        grid=(n // tm,),
        in_specs=in_specs,
        out_specs=[row(Q_WIDTH)] + kv_specs + tail_specs,
        out_shape=[q_shape] + kv_shapes + tail_shapes,
        input_output_aliases=aliases,
        compiler_params=pltpu.CompilerParams(dimension_semantics=("arbitrary",), vmem_limit_bytes=VMEM_LIMIT),
        name="proj_in",
    )(*args)


def _group_queries(q_tile, kvh):
    keep = _low_half() if kvh == 0 else ~_low_half()
    return jnp.concatenate([jnp.where(keep, q_tile[:, g * LANES:(g + 1) * LANES] * (SCALE * LOG2E), 0.0)
                            for g in range(NSA_GROUP)], axis=0).astype(BF16)


def _masked_neg_dist(dist, ok):
    return jnp.where(ok, -dist.astype(F32), NEG_INF)


def _softmax_rows(s_rows, neg_dist, kvh, tq):
    ps, inv = [], []
    for g in range(NSA_GROUP):
        s = s_rows[g * tq:(g + 1) * tq] + (SLOPES[kvh][g] * LOG2E) * neg_dist
        p = jnp.exp2(s - jnp.max(s, axis=-1, keepdims=True))
        inv.append(1.0 / jnp.sum(p, axis=-1, keepdims=True))
        ps.append(p)
    return jnp.concatenate(ps, axis=0).astype(BF16), jnp.concatenate(inv, axis=0)


N_SEL_SLOTS = 32
ALIBI_SLOT0 = N_SEL_SLOTS
MASK_BIG = 2.0 ** 100


def _bf16_terms(c):
    out = []
    for _ in range(3):
        t = float(np.float32(c).astype(BF16).astype(np.float32))
        out.append(t)
        c = c - t
    return out


ALIBI_TERMS = tuple(tuple(_bf16_terms(s * LOG2E) for s in row) for row in SLOPES)


def _slot_base(kvh):
    return HEAD_DIM if kvh == 0 else 0


def _augmented_keys(k_t, kvh, with_selection):
    n = k_t.shape[1]
    row = lax.broadcasted_iota(jnp.int32, (KV_WIDTH, 1), 0)
    kpos = lax.broadcasted_iota(jnp.int32, (1, n), 1)
    e = row - _slot_base(kvh)
    mine = (e < 0) | (e >= HEAD_DIM)
    k_hi = ((kpos >> 7) << 7).astype(F32)
    k_lo = (kpos & (LANES - 1)).astype(F32)
    extra = jnp.where((e >= ALIBI_SLOT0) & (e < ALIBI_SLOT0 + 3), k_hi,
                      jnp.where((e >= ALIBI_SLOT0 + 3) & (e < ALIBI_SLOT0 + 6), k_lo, 0.0))
    if with_selection:
        extra = jnp.where((e >= 0) & (e < N_SEL_SLOTS) & ((kpos >> int(math.log2(SEL_BLOCK))) == e), 1.0, extra)
    return jnp.where(mine, k_t, extra).astype(BF16)


def _augmented_queries(q_tile, kvh, sel):
    lane = lax.broadcasted_iota(jnp.int32, (1, LANES), 1)
    e = lane - _slot_base(kvh)
    keep = (e < 0) | (e >= HEAD_DIM)
    if sel is None:
        sel_extra = 0.0
    else:
        sel_at = sel if _slot_base(kvh) == 0 else pltpu.roll(sel, HEAD_DIM, 1)
        sel_extra = jnp.where((e >= 0) & (e < N_SEL_SLOTS), (sel_at - 1.0) * MASK_BIG, 0.0)
    rows = []
    for g in range(NSA_GROUP):
        alibi = jnp.zeros((1, LANES), F32)
        for i, term in enumerate(ALIBI_TERMS[kvh][g] * 2):
            alibi = jnp.where(e == ALIBI_SLOT0 + i, term, alibi)
        rows.append(jnp.where(keep, q_tile[:, g * LANES:(g + 1) * LANES] * (SCALE * LOG2E), sel_extra + alibi))
    return jnp.concatenate(rows, axis=0).astype(BF16)


def _attend_group(s_rows, bias, v_t, kvh, tq):
    mine = (lax.broadcasted_iota(jnp.int32, (KV_WIDTH, 1), 0) < HEAD_DIM) == (kvh == 0)
    v_aug = jnp.where(mine, v_t, jnp.ones_like(v_t))
    ps = []
    for g in range(NSA_GROUP):
        s = s_rows[g * tq:(g + 1) * tq] + bias
        ps.append(jnp.exp2((s - jnp.max(s, axis=-1, keepdims=True)).astype(BF16)))
    o = _dot_nt(jnp.concatenate(ps, axis=0), v_aug)
    denom_lane = HEAD_DIM if kvh == 0 else 0
    return o * (1.0 / o[:, denom_lane:denom_lane + 1])


def _split_order_block(w):
    lane = lax.broadcasted_iota(jnp.int32, (1, w), 1)
    return jnp.where(lane < w // 2, 2 * lane, 2 * (lane - w // 2) + 1)


def _compressed_probs(s_rows, qpos, kvh, tq, n_cmp, blk_of_lane=None):
    w = s_rows.shape[1]
    lane = lax.broadcasted_iota(jnp.int32, (1, w), 1) if blk_of_lane is None else blk_of_lane
    dist = qpos - ((lane + 1) * CMP_BLOCK - 1)
    valid = (dist >= 0) & (lane < n_cmp)
    nd = _masked_neg_dist(dist, valid)
    ps = []
    imp = jnp.zeros((tq, w), F32)
    for g in range(NSA_GROUP):
        s = s_rows[g * tq:(g + 1) * tq] + (SLOPES[kvh][g] * LOG2E) * nd
        e = jnp.exp2(s - jnp.max(s, axis=-1, keepdims=True))
        p = jnp.where(valid, e / jnp.sum(e, axis=-1, keepdims=True), 0.0)
        imp = imp + p
        ps.append(p)
    return jnp.concatenate(ps, axis=0).astype(BF16), imp


def _select_blocks(imp, qpos, n_sel):
    tq, w = imp.shape
    lane = lax.broadcasted_iota(jnp.int32, (1, w), 1)
    pair = imp + pltpu.roll(imp, w - 1, 1)
    blk = lane >> 1
    cur = qpos >> int(math.log2(SEL_BLOCK))
    is_blk = ((lane & 1) == 0) & (blk < n_sel)
    forced = (blk == 0) | (blk == cur) | (blk == cur - 1)
    score = jnp.where(forced, FORCE_SCORE, jnp.where(blk <= cur, pair, NEG_INF))
    score = jnp.where(is_blk, score, BELOW_ALL)
    rank = jnp.zeros((tq, w), jnp.int32)
    for i in range(n_sel):
        col = score[:, 2 * i:2 * i + 1]
        beats = (col > score) | ((col == score) & (lane > 2 * i))
        rank = rank + jnp.where(beats, 1, 0)
    return jnp.where((rank < min(TOP_K, n_sel)) & is_blk, 1.0, 0.0)


def _select_blocks_t(imp, qpos_row, n_sel):
    tq, w = imp.shape
    pair_t = (imp + pltpu.roll(imp, w // 2, 1)).T
    nb = -(-n_sel // SUBLANES) * SUBLANES
    blk = lax.broadcasted_iota(jnp.int32, (nb, 1), 0)
    cur = qpos_row >> int(math.log2(SEL_BLOCK))
    forced = (blk == 0) | (blk == cur) | (blk == cur - 1)
    score = jnp.where(forced, FORCE_SCORE, jnp.where(blk <= cur, pair_t[0:nb], NEG_INF))
    score = jnp.where(blk < n_sel, score, BELOW_ALL)
    rank = jnp.zeros((nb, tq), jnp.int32)
    for i in range(n_sel):
        row = score[i:i + 1, :]
        beats = (row > score) | ((row == score) & (blk > i))
        rank = rank + jnp.where(beats, 1, 0)
    sel_t = jnp.where((rank < min(TOP_K, n_sel)) & (blk < n_sel), 1.0, 0.0)
    return _pad_rows(sel_t, w).T


def _gated_sum(gates, kvh, tq, branches):
    out = []
    for g in range(NSA_GROUP):
        h = kvh * NSA_GROUP + g
        rows = slice(g * tq, (g + 1) * tq)
        out.append(sum(gates[:, c * NSA_HEADS + h:c * NSA_HEADS + h + 1] * o[rows] for c, o in enumerate(branches)))
    return out


def _store_slabs(o_ref, per_group):
    lo = _low_half()
    for g in range(NSA_GROUP):
        o_ref[:, g * LANES:(g + 1) * LANES] = jnp.where(lo, per_group[0][g], per_group[1][g])


def _nsa_prompt_kernel(q_ref, kv_ref, kvw_ref, gate_ref, wc_ref, o_ref, kc_ref, vc_ref, ks_ref, kw_ref, *, t_len, tq, key_step):
    qi = pl.program_id(1)
    n_cmp = t_len // CMP_BLOCK
    n_sel = -(-t_len // SEL_BLOCK)
    w = kc_ref.shape[1]
    stream = lambda s: slice(s * KV_WIDTH, (s + 1) * KV_WIDTH)

    @pl.when(qi == 0)
    def _():
        kc_ref[...] = _dot_f32(kv_ref[stream(0), :], wc_ref[0]).astype(BF16)
        vc_ref[...] = _dot_f32(kv_ref[stream(1), :], wc_ref[1]).astype(BF16)
        for kvh in range(NSA_KV_HEADS):
            ks_ref[kvh] = _augmented_keys(kv_ref[stream(2), :], kvh, True)
            kw_ref[kvh] = _augmented_keys(kvw_ref[stream(0), :], kvh, False)

    q0 = qi * tq
    qpos = q0 + lax.broadcasted_iota(jnp.int32, (tq, 1), 0)
    qpos_row = q0 + lax.broadcasted_iota(jnp.int32, (1, tq), 1)
    blk_of_lane = _split_order_block(w)
    n_win = min(WINDOW + tq, t_len)
    w_start = pl.multiple_of(jnp.maximum(q0 + tq - n_win, 0), LANES)

    def body(n_keys):
        q_tile = q_ref[...]
        gates = gate_ref[...]
        dist_s = qpos - lax.broadcasted_iota(jnp.int32, (1, n_keys), 1)
        dist_w = qpos - (w_start + lax.broadcasted_iota(jnp.int32, (1, n_win), 1))
        bias_s = jnp.where(dist_s >= 0, 0.0, NEG_INF)
        bias_w = jnp.where((dist_w >= 0) & (dist_w < WINDOW), 0.0, NEG_INF)
        v_s = kv_ref[stream(3), 0:n_keys].astype(BF16)
        v_w = kvw_ref[stream(1), pl.ds(w_start, n_win)].astype(BF16)
        per_group = []
        for kvh in range(NSA_KV_HEADS):
            qs = _group_queries(q_tile, kvh)
            p_c, imp = _compressed_probs(_dot(qs, kc_ref[...]), qpos, kvh, tq, n_cmp, blk_of_lane)
            o_c = _dot_nt(p_c, vc_ref[...])
            sel = _select_blocks_t(imp, qpos_row, n_sel)
            s_s = _dot(_augmented_queries(q_tile, kvh, sel), ks_ref[kvh, :, 0:n_keys])
            o_s = _attend_group(s_s, bias_s, v_s, kvh, tq)
            s_w = _dot(_augmented_queries(q_tile, kvh, None), kw_ref[kvh, :, pl.ds(w_start, n_win)])
            o_w = _attend_group(s_w, bias_w, v_w, kvh, tq)
            per_group.append(_gated_sum(gates, kvh, tq, (o_c, o_s, o_w)))
        _store_slabs(o_ref, per_group)

    n_classes = -(-t_len // key_step)
    for c in range(n_classes):
        n_keys = min((c + 1) * key_step, t_len)

        @pl.when((q0 + tq - 1) // key_step == c)
        def _(n_keys=n_keys):
            body(n_keys)


def _nsa_prompt(qn, rows_t, win_t, gates, wc, layer, bsz, t_len):
    tq = math.gcd(t_len, 128)
    key_step = math.gcd(t_len, 256)
    w = wc.shape[2]
    assert tq == LANES and w == LANES
    assert -(-t_len // SEL_BLOCK) <= N_SEL_SLOTS
    kern = functools.partial(_nsa_prompt_kernel, t_len=t_len, tq=tq, key_step=key_step)
    return pl.pallas_call(
        kern,
        grid=(bsz, t_len // tq),
        in_specs=[
            pl.BlockSpec((None, tq, Q_WIDTH), lambda b, i: (b, i, 0)),
            pl.BlockSpec((None, None, 4 * KV_WIDTH, t_len), lambda b, i: (layer, b, 0, 0)),
            pl.BlockSpec((None, None, 2 * KV_WIDTH, t_len), lambda b, i: (layer, b, 0, 0)),
            pl.BlockSpec((None, tq, MISC_WIDTH), lambda b, i: (b, i, 0)),
            pl.BlockSpec(wc.shape, lambda b, i: (0, 0, 0)),
        ],
        out_specs=pl.BlockSpec((None, tq, Q_WIDTH), lambda b, i: (b, i, 0)),
        out_shape=jax.ShapeDtypeStruct((bsz, t_len, Q_WIDTH), F32),
        scratch_shapes=[pltpu.VMEM((KV_WIDTH, w), BF16), pltpu.VMEM((KV_WIDTH, w), BF16),
                        pltpu.VMEM((NSA_KV_HEADS, KV_WIDTH, t_len), BF16), pltpu.VMEM((NSA_KV_HEADS, KV_WIDTH, t_len), BF16)],
        compiler_params=pltpu.CompilerParams(dimension_semantics=("arbitrary", "arbitrary"), vmem_limit_bytes=VMEM_LIMIT),
        name="nsa_prompt",
    )(qn.reshape(bsz, t_len, Q_WIDTH), rows_t, win_t, gates.reshape(bsz, t_len, MISC_WIDTH), wc)


def _pad_rows(a, n):
    return jnp.concatenate([a, jnp.zeros((n - a.shape[0], a.shape[1]), a.dtype)], axis=0)


def _nsa_decode_kernel(pt_ref, q_ref, kv_ref, gate_ref, win_ref, wpos_ref, *rest, ts, past_len, page_size, n_pages, w):
    page_refs = rest[:n_pages]
    o_ref, win_out_ref, s_ref, p_ref = rest[n_pages:]
    del pt_ref
    n_cmp = past_len // CMP_BLOCK
    n_sel = past_len // SEL_BLOCK + 1
    n_keys = past_len + page_size
    n_win = win_ref.shape[1]
    rows_q = NSA_GROUP * ts
    stream = lambda s: slice(s * KV_WIDTH, (s + 1) * KV_WIDTH)

    new = kv_ref[...]
    q_tile = q_ref[...]
    gates = gate_ref[...]
    qs = jnp.concatenate([_group_queries(q_tile, kvh) for kvh in range(NSA_KV_HEADS)], axis=0)
    qpos = past_len + lax.broadcasted_iota(jnp.int32, (ts, 1), 0)

    per_pair = 2 * page_size // CMP_BLOCK
    wk = jnp.concatenate([wpos_ref[0]] * per_pair, axis=0).reshape(per_pair, CMP_BLOCK, KV_WIDTH)
    wv = jnp.concatenate([wpos_ref[1]] * per_pair, axis=0).reshape(per_pair, CMP_BLOCK, KV_WIDTH)
    kcs, vcs = [], []
    for j in range(0, n_pages, 2):
        k2 = jnp.concatenate([page_refs[j][stream(0), :].T, page_refs[j + 1][stream(0), :].T], axis=0)
        v2 = jnp.concatenate([page_refs[j][stream(1), :].T, page_refs[j + 1][stream(1), :].T], axis=0)
        kcs.append(jnp.sum(k2.reshape(per_pair, CMP_BLOCK, KV_WIDTH) * wk, axis=1))
        vcs.append(jnp.sum(v2.reshape(per_pair, CMP_BLOCK, KV_WIDTH) * wv, axis=1))
    kc = _pad_rows(jnp.concatenate(kcs, axis=0), w).astype(BF16)
    vc = _pad_rows(jnp.concatenate(vcs, axis=0), w).astype(BF16)

    s_c = _dot_nt(qs, kc)
    p_cs, sels = [], []
    for kvh in range(NSA_KV_HEADS):
        p_c, imp = _compressed_probs(s_c[kvh * rows_q:(kvh + 1) * rows_q], qpos, kvh, ts, n_cmp)
        p_cs.append(p_c)
        sels.append(_select_blocks(imp, qpos, n_sel))
    o_c = _dot(jnp.concatenate(p_cs, axis=0), vc)

    new_k = _pad_rows(new[:, stream(2)], page_size).astype(BF16)
    new_v = _pad_rows(new[:, stream(3)], page_size).astype(BF16)
    for j in range(n_pages):
        s_ref[:, j * page_size:(j + 1) * page_size] = _dot(qs, page_refs[j][stream(2), :].astype(BF16))
    s_ref[:, past_len:n_keys] = _dot_nt(qs, new_k)
    dist_s = qpos - lax.broadcasted_iota(jnp.int32, (1, n_keys), 1)
    blocks_per_page = page_size // SEL_BLOCK
    page_lane_blk = lax.broadcasted_iota(jnp.int32, (1, page_size), 1) >> int(math.log2(SEL_BLOCK))
    invs = []
    for kvh in range(NSA_KV_HEADS):
        sel = sels[kvh]
        pieces = []
        for j in range(n_pages + 1):
            piece = jnp.zeros((ts, page_size), F32)
            for r in range(blocks_per_page):
                blk = j * blocks_per_page + r
                if blk < n_sel:
                    piece = jnp.where(page_lane_blk == r, sel[:, 2 * blk:2 * blk + 1], piece)
            pieces.append(piece)
        keymask = jnp.concatenate(pieces, axis=1)
        nd_s = _masked_neg_dist(dist_s, (keymask > 0.5) & (dist_s >= 0))
        p_s, inv_s = _softmax_rows(s_ref[kvh * rows_q:(kvh + 1) * rows_q, :], nd_s, kvh, ts)
        p_ref[kvh * rows_q:(kvh + 1) * rows_q, :] = p_s
        invs.append(inv_s)
    o_s = _dot(p_ref[:, past_len:n_keys], new_v)
    for j in range(n_pages):
        o_s = o_s + _dot_nt(p_ref[:, j * page_size:(j + 1) * page_size], page_refs[j][stream(3), :].astype(BF16))
    o_s = o_s * jnp.concatenate(invs, axis=0)

    new_kw = _pad_rows(new[:, stream(4)], LANES).astype(BF16)
    new_vw = _pad_rows(new[:, stream(5)], LANES).astype(BF16)
    s_w = jnp.concatenate([_dot(qs, win_ref[0:KV_WIDTH, :].astype(BF16)), _dot_nt(qs, new_kw)], axis=1)
    wpos = jnp.concatenate([past_len - n_win + lax.broadcasted_iota(jnp.int32, (1, n_win), 1),
                            past_len + lax.broadcasted_iota(jnp.int32, (1, LANES), 1)], axis=1)
    dist_w = qpos - wpos
    nd_w = _masked_neg_dist(dist_w, (dist_w >= 0) & (dist_w < WINDOW))
    p_ws, inv_ws = [], []
    for kvh in range(NSA_KV_HEADS):
        p_w, inv_w = _softmax_rows(s_w[kvh * rows_q:(kvh + 1) * rows_q], nd_w, kvh, ts)
        p_ws.append(p_w)
        inv_ws.append(inv_w)
    p_w = jnp.concatenate(p_ws, axis=0)
    o_w = (_dot_nt(p_w[:, 0:n_win], win_ref[KV_WIDTH:2 * KV_WIDTH, :].astype(BF16)) + _dot(p_w[:, n_win:], new_vw))
    o_w = o_w * jnp.concatenate(inv_ws, axis=0)

    per_group = [_gated_sum(gates, kvh, ts, [o[kvh * rows_q:(kvh + 1) * rows_q] for o in (o_c, o_s, o_w)])
                 for kvh in range(NSA_KV_HEADS)]
    _store_slabs(o_ref, per_group)

    new_t = _pad_rows(new[:, 4 * KV_WIDTH:6 * KV_WIDTH], LANES).T
    new_t = pltpu.roll(new_t, LANES - ts, 1)
    shifted = pltpu.roll(win_ref[...], n_win - ts, 1)
    tail_lane = lax.broadcasted_iota(jnp.int32, (1, LANES), 1)
    win_out_ref[:, 0:n_win - LANES] = shifted[:, 0:n_win - LANES]
    win_out_ref[:, n_win - LANES:n_win] = jnp.where(tail_lane >= LANES - ts, new_t, shifted[:, n_win - LANES:n_win])


def _nsa_decode(qn, kvn, gates, win_t, cache_t, page_table, wpos_rows, layer, dbsz, ts):
    n_pages = page_table.shape[1]
    page_size = cache_t.shape[3]
    past_len = n_pages * page_size
    n_win = win_t.shape[3]
    assert ts < CMP_BLOCK and ts % SUBLANES == 0 and page_size == LANES and n_pages % 2 == 0
    assert n_win == WINDOW and past_len >= WINDOW
    n_cmp = past_len // CMP_BLOCK
    n_sel = past_len // SEL_BLOCK + 1
    w = -(-max(n_cmp, 2 * n_sel) // LANES) * LANES
    n_keys = past_len + page_size
    kern = functools.partial(_nsa_decode_kernel, ts=ts, past_len=past_len, page_size=page_size, n_pages=n_pages, w=w)

    def page_spec(j):
        return pl.BlockSpec((None, None, 4 * KV_WIDTH, page_size), lambda b, pt: (layer, pt[b, j], 0, 0))

    grid_spec = pltpu.PrefetchScalarGridSpec(
        num_scalar_prefetch=1,
        grid=(dbsz,),
        in_specs=[
            pl.BlockSpec((None, ts, Q_WIDTH), lambda b, pt: (b, 0, 0)),
            pl.BlockSpec((None, ts, KV6_WIDTH), lambda b, pt: (b, 0, 0)),
            pl.BlockSpec((None, ts, MISC_WIDTH), lambda b, pt: (b, 0, 0)),
            pl.BlockSpec((None, None, 2 * KV_WIDTH, n_win), lambda b, pt: (layer, b, 0, 0)),
            pl.BlockSpec(wpos_rows.shape, lambda b, pt: (0, 0, 0)),
        ] + [page_spec(j) for j in range(n_pages)],
        out_specs=[
            pl.BlockSpec((None, ts, Q_WIDTH), lambda b, pt: (b, 0, 0)),
            pl.BlockSpec((None, 2 * KV_WIDTH, n_win), lambda b, pt: (b, 0, 0)),
        ],
        scratch_shapes=[pltpu.VMEM((NSA_HEADS * ts, n_keys), F32), pltpu.VMEM((NSA_HEADS * ts, n_keys), BF16)],
    )
    return pl.pallas_call(
        kern,
        grid_spec=grid_spec,
        out_shape=[jax.ShapeDtypeStruct((dbsz, ts, Q_WIDTH), F32), jax.ShapeDtypeStruct((dbsz, 2 * KV_WIDTH, n_win), F32)],
        compiler_params=pltpu.CompilerParams(dimension_semantics=("arbitrary",), vmem_limit_bytes=VMEM_LIMIT),
        name="nsa_decode",
    )(page_table, qn.reshape(dbsz, ts, Q_WIDTH), kvn.reshape(dbsz, ts, KV6_WIDTH), gates.reshape(dbsz, ts, MISC_WIDTH),
      win_t, wpos_rows, *([cache_t] * n_pages))


def _gla_kernel(*refs, t_len, chunk, has_s0):
    if has_s0:
        q_ref, k_ref, v_ref, gg_ref, la_ref, gn_ref, s0_ref, o_ref, s_out_ref, st_ref, u_ref, d_ref, sb_ref, qe_ref = refs
    else:
        q_ref, k_ref, v_ref, gg_ref, la_ref, gn_ref, o_ref, s_out_ref, st_ref, u_ref, d_ref, sb_ref, qe_ref = refs
    dk2, dv2 = 2 * GLA_DK, 2 * GLA_DV
    n_chunks = t_len // chunk
    per_group = math.gcd(n_chunks, 4)
    rows_g = per_group * chunk
    n_groups = n_chunks // per_group
    rp = max(rows_g, LANES)
    pad = rp - rows_g

    rr = lax.broadcasted_iota(jnp.int32, (dv2, dk2), 0) // GLA_DV
    cc = lax.broadcasted_iota(jnp.int32, (dv2, dk2), 1) // GLA_DK
    diag = rr == cc
    if has_s0:
        z = jnp.zeros((GLA_DK, GLA_DV), F32)
        s_full = jnp.concatenate([jnp.concatenate([s0_ref[0], z], axis=1), jnp.concatenate([z, s0_ref[1]], axis=1)], axis=0)
        st_ref[...] = s_full.T
    else:
        st_ref[...] = jnp.zeros((dv2, dk2), F32)

    lane_head = lax.broadcasted_iota(jnp.int32, (1, dk2), 1) // GLA_DK
    trow = lax.broadcasted_iota(jnp.int32, (rows_g, rp), 0)
    tcol = lax.broadcasted_iota(jnp.int32, (rows_g, rp), 1)
    causal = (tcol <= trow) & (tcol // chunk == trow // chunk)
    tril = jnp.where(causal, 1.0, 0.0).astype(BF16)
    prow_chunk = lax.broadcasted_iota(jnp.int32, (rp, 1), 0) // chunk
    mid = chunk // 2
    gn = gn_ref[...]

    def pad_rows(a):
        return a if pad == 0 else jnp.concatenate([a, jnp.zeros((pad, a.shape[1]), a.dtype)], axis=0)

    def per_chunk_row(cum, r):
        return jnp.concatenate([jnp.broadcast_to(cum[c * chunk + r:c * chunk + r + 1, :], (chunk, dk2))
                                for c in range(per_group)], axis=0)

    def group_local(gi, carry):
        r0 = pl.multiple_of(gi * rows_g, rows_g)
        rows = pl.ds(r0, rows_g)
        q = q_ref[rows, :] * (GLA_DK ** -0.5)
        k = k_ref[rows, :]
        la0, la1, la2 = _split3(pad_rows(la_ref[rows, :]))
        cum = _dot(tril, la0) + _dot(tril, la1) + _dot(tril, la2)
        m = per_chunk_row(cum, mid)
        last = per_chunk_row(cum, chunk - 1)
        qe_ref[rows, :] = q * jnp.exp(cum)
        qs = q * jnp.exp(cum - m)
        ks = pad_rows((k * jnp.exp(m - cum)).astype(BF16))
        vp = pad_rows(v_ref[rows, :].astype(BF16))
        intra = []
        for h in range(2):
            a = _dot_nt(jnp.where(lane_head == h, qs, 0.0).astype(BF16), ks)
            a = jnp.where(causal, a, 0.0).astype(BF16)
            intra.append(_dot(a, vp[:, h * GLA_DV:(h + 1) * GLA_DV]))
        o_ref[rows, :] = jnp.concatenate(intra, axis=1)
        kd = pad_rows(k * jnp.exp(last - cum))
        for c in range(per_group):
            ci = gi * per_group + c
            kd_c = jnp.where(prow_chunk == c, kd, 0.0).astype(BF16)
            u_ref[ci] = jnp.where(diag, _dot_tn(vp, kd_c), 0.0)
            d_ref[ci] = jnp.exp(last[c * chunk:c * chunk + SUBLANES, :])
        return carry

    def chunk_state(ci, carry):
        st = st_ref[...]
        sb_ref[ci] = st.astype(BF16)
        st_ref[...] = st * d_ref[ci][0:1, :] + u_ref[ci]
        return carry

    def group_output(gi, carry):
        r0 = pl.multiple_of(gi * rows_g, rows_g)
        rows = pl.ds(r0, rows_g)
        inter = [_dot_nt(qe_ref[pl.ds(pl.multiple_of(r0 + c * chunk, chunk), chunk), :].astype(BF16),
                         sb_ref[gi * per_group + c]) for c in range(per_group)]
        o = o_ref[rows, :] + jnp.concatenate(inter, axis=0)
        gg = gg_ref[rows, :]
        outs = []
        for h in range(2):
            oh = o[:, h * GLA_DV:(h + 1) * GLA_DV]
            y = oh * lax.rsqrt(jnp.mean(oh * oh, axis=-1, keepdims=True) + EPS) * gn
            gh = gg[:, h * GLA_DV:(h + 1) * GLA_DV]
            outs.append(y * (gh * _sigmoid(gh)))
        o_ref[rows, :] = jnp.concatenate(outs, axis=1)
        return carry

    lax.fori_loop(0, n_groups, group_local, 0)
    lax.fori_loop(0, n_chunks, chunk_state, 0)
    lax.fori_loop(0, n_groups, group_output, 0)
    s_fin = st_ref[...].T
    s_out_ref[0] = s_fin[0:GLA_DK, 0:GLA_DV]
    s_out_ref[1] = s_fin[GLA_DK:dk2, GLA_DV:dv2]


def _gla(g, la, gnorm, s0, bsz, t_len):
    chunk = math.gcd(t_len, GLA_CHUNK)
    has_s0 = s0 is not None
    kern = functools.partial(_gla_kernel, t_len=t_len, chunk=chunk, has_s0=has_s0)
    qk_blk = lambda off: pl.BlockSpec((None, t_len, 2 * GLA_DK), lambda b, p: (b, 0, off + p))
    v_blk = lambda off: pl.BlockSpec((None, t_len, 2 * GLA_DV), lambda b, p: (b, 0, off + p))
    g3 = g.reshape(bsz, t_len, G_WIDTH)
    n_qk = GLA_QK_WIDTH // (2 * GLA_DK)
    in_specs = [qk_blk(0), qk_blk(n_qk), v_blk(n_qk), v_blk(n_qk + GLA_V_WIDTH // (2 * GLA_DV)),
                pl.BlockSpec((None, t_len, 2 * GLA_DK), lambda b, p: (b, 0, p)),
                pl.BlockSpec(gnorm.shape, lambda b, p: (0, 0))]
    args = [g3, g3, g3, g3, la.reshape(bsz, t_len, GLA_QK_WIDTH), gnorm]
    if has_s0:
        in_specs.append(pl.BlockSpec((None, 2, GLA_DK, GLA_DV), lambda b, p: (b, p, 0, 0)))
        args.append(s0)
    return pl.pallas_call(
        kern,
        grid=(bsz, GLA_HEADS // 2),
        in_specs=in_specs,
        out_specs=[pl.BlockSpec((None, t_len, 2 * GLA_DV), lambda b, p: (b, 0, p)),
                   pl.BlockSpec((None, 2, GLA_DK, GLA_DV), lambda b, p: (b, p, 0, 0))],
        out_shape=[jax.ShapeDtypeStruct((bsz, t_len, GLA_V_WIDTH), F32),
                   jax.ShapeDtypeStruct((bsz, GLA_HEADS, GLA_DK, GLA_DV), F32)],
        scratch_shapes=[pltpu.VMEM((2 * GLA_DV, 2 * GLA_DK), F32),
                        pltpu.VMEM((t_len // chunk, 2 * GLA_DV, 2 * GLA_DK), F32),
                        pltpu.VMEM((t_len // chunk, SUBLANES, 2 * GLA_DK), F32),
                        pltpu.VMEM((t_len // chunk, 2 * GLA_DV, 2 * GLA_DK), BF16),
                        pltpu.VMEM((t_len, 2 * GLA_DK), F32)],
        compiler_params=pltpu.CompilerParams(dimension_semantics=("arbitrary", "arbitrary"), vmem_limit_bytes=VMEM_LIMIT),
        name="gla",
    )(*args)


FF_STEPS = 2
FF_CHUNK = D_FF // FF_STEPS
assert FF_CHUNK * FF_STEPS == D_FF and FF_CHUNK % LANES == 0


def _out_ffn_kernel(x_ref, on_ref, og_ref, wo_n_ref, wo_g_ref, ln_ref, wg_ref, wu_ref, wd_ref, y_ref, h_ref):
    j = pl.program_id(1)

    @pl.when(j == 0)
    def _():
        x1 = x_ref[...] + _dot(on_ref[...].astype(BF16), wo_n_ref[...]) + _dot(og_ref[...].astype(BF16), wo_g_ref[...])
        h_ref[...] = (x1 * lax.rsqrt(jnp.mean(x1 * x1, axis=-1, keepdims=True) + EPS) * ln_ref[...]).astype(BF16)
        y_ref[...] = x1

    h = h_ref[...]
    gate = _dot(h, wg_ref[...])
    up = _dot(h, wu_ref[...])
    y_ref[...] += _dot((gate * _sigmoid(gate) * up).astype(BF16), wd_ref[...])


def _out_ffn(x, o_nsa, o_gla, wo_n, wo_g, ln, wg, wu, wd, tm):
    n = x.shape[0]
    row = lambda w: pl.BlockSpec((tm, w), lambda i, j: (i, 0))
    full = lambda a: pl.BlockSpec(a.shape, lambda i, j: (0,) * a.ndim)
    return pl.pallas_call(
        _out_ffn_kernel,
        grid=(n // tm, FF_STEPS),
        in_specs=[row(D_MODEL), row(Q_WIDTH), row(GLA_V_WIDTH), full(wo_n), full(wo_g), full(ln),
                  pl.BlockSpec((D_MODEL, FF_CHUNK), lambda i, j: (0, j)),
                  pl.BlockSpec((D_MODEL, FF_CHUNK), lambda i, j: (0, j)),
                  pl.BlockSpec((FF_CHUNK, D_MODEL), lambda i, j: (j, 0))],
        out_specs=row(D_MODEL),
        out_shape=jax.ShapeDtypeStruct((n, D_MODEL), F32),
        scratch_shapes=[pltpu.VMEM((tm, D_MODEL), BF16)],
        compiler_params=pltpu.CompilerParams(dimension_semantics=("arbitrary", "arbitrary"), vmem_limit_bytes=VMEM_LIMIT),
        name="out_ffn",
    )(x, o_nsa, o_gla, wo_n, wo_g, ln, wg, wu, wd)


def _layer_weights(ln_mix, w_in, q_norm, k_norm, cmp_pos_w, w_a2, b_a, gla_norm, w_out, ln_ffn, w_gate, w_up, w_down,
                   tm_prompt, t_len):
    w_in_t = w_in.T
    o = 0
    parts = []
    for width in (Q_WIDTH, KV6_WIDTH, N_GATES, GLA_QK_WIDTH, GLA_QK_WIDTH, GLA_V_WIDTH, GLA_V_WIDTH, GLA_RANK):
        parts.append(w_in_t[o:o + width])
        o += width
    wq, wkv, wgate, wgq, wgk, wgv, wgg, wga = parts
    wq = jnp.concatenate([wq[h * HEAD_DIM:(h + 1) * HEAD_DIM] for h in SLAB_HEADS], axis=0)
    pad = jnp.zeros((MISC_WIDTH - N_GATES - GLA_RANK, D_MODEL), w_in.dtype)
    wa = jnp.zeros((MISC_WIDTH, GLA_QK_WIDTH), F32).at[N_GATES:N_GATES + GLA_RANK].set(w_a2)
    n_cmp = t_len // CMP_BLOCK
    w_lanes = -(-max(n_cmp, 2 * (-(-t_len // SEL_BLOCK))) // LANES) * LANES
    tok = jnp.arange(t_len)
    lane = jnp.arange(w_lanes)
    blk_of_lane = jnp.where(lane < w_lanes // 2, 2 * lane, 2 * (lane - w_lanes // 2) + 1)
    in_blk = (tok[:, None] // CMP_BLOCK == blk_of_lane[None, :]) & (tok[:, None] < n_cmp * CMP_BLOCK)
    wc = jnp.where(in_blk[None], cmp_pos_w[:, tok % CMP_BLOCK][:, :, None], 0.0)
    kg_rows = jnp.tile(k_norm, (1, NSA_KV_HEADS))
    return dict(
        ln_mix=ln_mix.reshape(1, D_MODEL),
        wq=wq.astype(BF16), wkv=wkv.astype(BF16),
        wg=jnp.concatenate([wgq, wgk, wgv, wgg], axis=0).astype(BF16),
        wm=jnp.concatenate([wgate, wga, pad], axis=0).astype(BF16),
        qg=jnp.tile(q_norm, LANES // HEAD_DIM).reshape(1, LANES),
        kg=kg_rows, kg_t=jnp.broadcast_to(kg_rows[:, :, None], (3, KV_WIDTH, tm_prompt)),
        wa=wa, ba=b_a.reshape(1, GLA_QK_WIDTH),
        wpos=jnp.broadcast_to(cmp_pos_w[:, :, None], (2, CMP_BLOCK, LANES)), wc=wc,
        gnorm=gla_norm.reshape(1, GLA_DV),
        wo_n=jnp.concatenate([w_out[h * HEAD_DIM:(h + 1) * HEAD_DIM] for h in SLAB_HEADS], axis=0).astype(BF16),
        wo_g=w_out[Q_WIDTH:].astype(BF16),
        ln_ffn=ln_ffn.reshape(1, D_MODEL),
        w_gate=w_gate.astype(BF16), w_up=w_up.astype(BF16), w_down=w_down.astype(BF16),
    )


def _row_tile(n):
    return math.gcd(n, 512)


def _mix_and_ffn(x, w, attn, s0, bsz, t_len, kv_bufs=None, layer=None):
    tm = _row_tile(x.shape[0])
    qn, *kv, g, gates, la = _proj_in(x, w, tm, kv_bufs, layer)
    o_nsa, extra = attn(qn, kv, gates)
    o_gla, s_new = _gla(g, la, w["gnorm"], s0, bsz, t_len)
    y = _out_ffn(x, o_nsa.reshape(-1, Q_WIDTH), o_gla.reshape(-1, GLA_V_WIDTH), w["wo_n"], w["wo_g"], w["ln_ffn"],
                 w["w_gate"], w["w_up"], w["w_down"], tm)
    return y, kv, s_new, extra


def _token_major(a_t, lead):
    n_lead = len(lead)
    a = a_t.reshape(*lead, -1, NSA_KV_HEADS, HEAD_DIM, a_t.shape[-1])
    return a.transpose(*range(n_lead), n_lead + 3, n_lead, n_lead + 1, n_lead + 2)


def kernel(x_prompt, x_sample, cache_nsa_kv, state_nsa_win, state_gla, page_table, ln_mix, w_in, q_norm, k_norm, cmp_pos_w,
           w_a2, b_a, gla_norm, w_out, ln_ffn, w_gate, w_up, w_down):
    bsz, t_len = x_prompt.shape[:2]
    dbsz, ts = x_sample.shape[:2]
    depth = w_in.shape[0]
    n_phys, page_size = cache_nsa_kv.shape[1:3]
    n_win = state_nsa_win.shape[2]
    cache_t = cache_nsa_kv.transpose(0, 1, 3, 4, 5, 2).reshape(depth, n_phys, 4 * KV_WIDTH, page_size)
    win_t = state_nsa_win.transpose(0, 1, 3, 4, 5, 2).reshape(depth, dbsz, 2 * KV_WIDTH, n_win)
    keep_p = min(WINDOW, t_len)
    tm_prompt = _row_tile(bsz * t_len)
    assert t_len % tm_prompt == 0

    yp = x_prompt.reshape(bsz * t_len, D_MODEL)
    ys = x_sample.reshape(dbsz * ts, D_MODEL)
    kv_bufs = [jnp.zeros((depth, bsz, 4 * KV_WIDTH, t_len), F32), jnp.zeros((depth, bsz, 2 * KV_WIDTH, t_len), F32)]
    gla_p, rows_s, win_s, gla_s = [], [], [], []
    for l in range(depth):
        w = _layer_weights(ln_mix[l], w_in[l], q_norm[l], k_norm[l], cmp_pos_w[l], w_a2[l], b_a[l], gla_norm[l], w_out[l],
                           ln_ffn[l], w_gate[l], w_up[l], w_down[l], tm_prompt, t_len)

        def attn_prompt(qn, kv, gates):
            return _nsa_prompt(qn, kv[0], kv[1], gates, w["wc"], l, bsz, t_len), None

        def attn_sample(qn, kv, gates):
            return _nsa_decode(qn, kv[0], gates, win_t, cache_t, page_table, w["wpos"], l, dbsz, ts)

        yp, kv_bufs, st_p, _ = _mix_and_ffn(yp, w, attn_prompt, None, bsz, t_len, kv_bufs, l)
        ys, (kvn_s,), st_s, nw_s = _mix_and_ffn(ys, w, attn_sample, state_gla[l], dbsz, ts)

        gla_p.append(st_p.astype(state_gla.dtype))
        rows_s.append(kvn_s.reshape(dbsz, ts, 6, NSA_KV_HEADS, HEAD_DIM)[:, :, :4])
        win_s.append(nw_s)
        gla_s.append(st_s.astype(state_gla.dtype))
    lead_p, lead_s = (depth, bsz), (depth, dbsz)
    return (yp.reshape(bsz, t_len, D_MODEL), ys.reshape(dbsz, ts, D_MODEL),
            _token_major(kv_bufs[0], lead_p), _token_major(kv_bufs[1][:, :, :, t_len - keep_p:], lead_p), jnp.stack(gla_p),
            jnp.stack(rows_s), _token_major(jnp.stack(win_s), lead_s), jnp.stack(gla_s))
```

```python
import functools
import math

import jax
import jax.numpy as jnp
import numpy as np
from jax import lax
from jax.experimental import pallas as pl
from jax.experimental.pallas import tpu as pltpu

F32 = jnp.float32
BF16 = jnp.bfloat16

D_MODEL = 1024
NSA_HEADS = 8
NSA_KV_HEADS = 2
NSA_GROUP = NSA_HEADS // NSA_KV_HEADS
HEAD_DIM = 64
CMP_BLOCK = 32
SEL_BLOCK = 64
TOP_K = 16
WINDOW = 512
GLA_HEADS = 4
GLA_DK = 64
GLA_DV = 128
GLA_RANK = 16
GLA_TAU = 16.0
GLA_CHUNK = 64
D_FF = -(-(8 * D_MODEL) // (3 * 256)) * 256
KV_WIDTH = NSA_KV_HEADS * HEAD_DIM
Q_WIDTH = NSA_HEADS * HEAD_DIM
N_GATES = 3 * NSA_HEADS
GLA_QK_WIDTH = GLA_HEADS * GLA_DK
GLA_V_WIDTH = GLA_HEADS * GLA_DV
EPS = 1e-6
NEG_INF = -1e30
FORCE_SCORE = 1e4
BELOW_ALL = -3e38
SCALE = HEAD_DIM ** -0.5
LOG2E = 1.4426950408889634
SLOPES = tuple(tuple(2.0 ** (-8.0 * (k * NSA_GROUP + g + 1) / NSA_HEADS) for g in range(NSA_GROUP))
               for k in range(NSA_KV_HEADS))
SLAB_HEADS = tuple(h for g in range(NSA_GROUP) for h in (g, NSA_GROUP + g))

LANES = 128
SUBLANES = 8
VMEM_LIMIT = 56 * 1024 * 1024
assert KV_WIDTH == LANES and NSA_KV_HEADS == 2

G_WIDTH = 2 * GLA_QK_WIDTH + 2 * GLA_V_WIDTH
MISC_WIDTH = LANES
KV6_WIDTH = 6 * KV_WIDTH


def _dot(a, b):
    return jnp.dot(a, b, preferred_element_type=F32)


def _dot_nt(a, b):
    return lax.dot_general(a, b, (((1,), (1,)), ((), ())), preferred_element_type=F32)


def _dot_tn(a, b):
    return lax.dot_general(a, b, (((0,), (0,)), ((), ())), preferred_element_type=F32)


def _split3(a):
    a0 = a.astype(BF16)
    r = a - a0.astype(F32)
    a1 = r.astype(BF16)
    a2 = (r - a1.astype(F32)).astype(BF16)
    return a0, a1, a2


def _dot_f32(a, b):
    a0, a1, a2 = _split3(a)
    b0, b1, b2 = _split3(b)
    return (_dot(a0, b0) + (_dot(a0, b1) + _dot(a1, b0)) + (_dot(a0, b2) + _dot(a1, b1) + _dot(a2, b0)))


def _sigmoid(x):
    return 1.0 / (1.0 + jnp.exp(-x))


def _low_half(width=LANES):
    return lax.broadcasted_iota(jnp.int32, (1, width), 1) < HEAD_DIM


def _half_lane_rms(x, gain):
    x2 = x * x
    lo = _low_half()
    s_lo = jnp.sum(jnp.where(lo, x2, 0.0), axis=-1, keepdims=True)
    s_hi = jnp.sum(jnp.where(lo, 0.0, x2), axis=-1, keepdims=True)
    ms = jnp.where(lo, s_lo, s_hi) * (1.0 / HEAD_DIM)
    return x * lax.rsqrt(ms + EPS) * gain


def _proj_in_kernel(x_ref, ln_ref, wq_ref, wkv_ref, wg_ref, wm_ref, qg_ref, kg_ref, wa_ref, ba_ref, *rest, kv_transposed):
    if kv_transposed:
        _, _, q_out, rows_out, win_out, g_out, gate_out, la_out = rest
    else:
        q_out, kv_out, g_out, gate_out, la_out = rest
    x = x_ref[...]
    h = x * lax.rsqrt(jnp.mean(x * x, axis=-1, keepdims=True) + EPS) * ln_ref[...]
    hb = h.astype(BF16)

    q = _dot_nt(hb, wq_ref[...])
    for j in range(Q_WIDTH // LANES):
        sl = slice(j * LANES, (j + 1) * LANES)
        q_out[:, sl] = _half_lane_rms(q[:, sl], qg_ref[...])

    if kv_transposed:
        kv = _dot_nt(wkv_ref[...], hb)
        tm = kv.shape[1]
        for s in range(6):
            rows = slice(s * KV_WIDTH, (s + 1) * KV_WIDTH)
            out, s_out = (rows_out, s) if s < 4 else (win_out, s - 4)
            dst = slice(s_out * KV_WIDTH, (s_out + 1) * KV_WIDTH)
            if s % 2 == 0:
                k3 = kv[rows].reshape(NSA_KV_HEADS, HEAD_DIM, tm)
                ms = jnp.mean(k3 * k3, axis=1, keepdims=True)
                out[dst, :] = (k3 * lax.rsqrt(ms + EPS)).reshape(KV_WIDTH, tm) * kg_ref[s // 2]
            else:
                out[dst, :] = kv[rows]
    else:
        kv = _dot_nt(hb, wkv_ref[...])
        for s in range(6):
            sl = slice(s * KV_WIDTH, (s + 1) * KV_WIDTH)
            if s % 2 == 0:
                kv_out[:, sl] = _half_lane_rms(kv[:, sl], kg_ref[s // 2:s // 2 + 1, :])
            else:
                kv_out[:, sl] = kv[:, sl]

    g_out[...] = _dot_nt(hb, wg_ref[...])

    m = _dot_nt(hb, wm_ref[...])
    gate_out[...] = _sigmoid(m)
    z = _dot_f32(m, wa_ref[...]) + ba_ref[...]
    la_out[...] = (jnp.minimum(z, 0.0) - jnp.log1p(jnp.exp(-jnp.abs(z)))) * (1.0 / GLA_TAU)


def _proj_in(x, w, tm, kv_bufs=None, layer=None):
    n = x.shape[0]
    row = lambda width: pl.BlockSpec((tm, width), lambda i: (i, 0))
    full = lambda a: pl.BlockSpec(a.shape, lambda i: (0,) * a.ndim)
    args = [x, w["ln_mix"], w["wq"], w["wkv"], w["wg"], w["wm"], w["qg"], w["kg"] if kv_bufs is None else w["kg_t"],
            w["wa"], w["ba"]]
    in_specs = [row(D_MODEL)] + [full(a) for a in args[1:]]
    tail_specs = [row(G_WIDTH), row(MISC_WIDTH), row(GLA_QK_WIDTH)]
    tail_shapes = [jax.ShapeDtypeStruct((n, width), F32) for width in (G_WIDTH, MISC_WIDTH, GLA_QK_WIDTH)]
    q_shape = jax.ShapeDtypeStruct((n, Q_WIDTH), F32)
    if kv_bufs is None:
        kv_specs = [row(KV6_WIDTH)]
        kv_shapes = [jax.ShapeDtypeStruct((n, KV6_WIDTH), F32)]
        aliases = {}
    else:
        tiles = kv_bufs[0].shape[3] // tm
        kv_specs = [pl.BlockSpec((None, None, b.shape[2], tm), lambda i: (layer, i // tiles, 0, i % tiles)) for b in kv_bufs]
        kv_shapes = [jax.ShapeDtypeStruct(b.shape, b.dtype) for b in kv_bufs]
        aliases = {len(args): 1, len(args) + 1: 2}
        in_specs += [pl.BlockSpec(memory_space=pl.ANY)] * 2
        args += list(kv_bufs)
    return pl.pallas_call(
        functools.partial(_proj_in_kernel, kv_transposed=kv_bufs is not None),
        grid=(n // tm,),
        in_specs=in_specs,
        out_specs=[row(Q_WIDTH)] + kv_specs + tail_specs,
        out_shape=[q_shape] + kv_shapes + tail_shapes,
        input_output_aliases=aliases,
        compiler_params=pltpu.CompilerParams(dimension_semantics=("arbitrary",), vmem_limit_bytes=VMEM_LIMIT),
        name="proj_in",
    )(*args)


def _group_queries(q_tile, kvh):
    keep = _low_half() if kvh == 0 else ~_low_half()
    return jnp.concatenate([jnp.where(keep, q_tile[:, g * LANES:(g + 1) * LANES] * (SCALE * LOG2E), 0.0)
                            for g in range(NSA_GROUP)], axis=0).astype(BF16)


def _masked_neg_dist(dist, ok):
    return jnp.where(ok, -dist.astype(F32), NEG_INF)


def _softmax_rows(s_rows, neg_dist, kvh, tq):
    ps, inv = [], []
    for g in range(NSA_GROUP):
        s = s_rows[g * tq:(g + 1) * tq] + (SLOPES[kvh][g] * LOG2E) * neg_dist
        p = jnp.exp2(s - jnp.max(s, axis=-1, keepdims=True))
        inv.append(1.0 / jnp.sum(p, axis=-1, keepdims=True))
        ps.append(p)
    return jnp.concatenate(ps, axis=0).astype(BF16), jnp.concatenate(inv, axis=0)


N_SEL_SLOTS = 32
ALIBI_SLOT0 = N_SEL_SLOTS
MASK_BIG = 2.0 ** 100


def _bf16_terms(c):
    out = []
    for _ in range(3):
        t = float(np.float32(c).astype(BF16).astype(np.float32))
        out.append(t)
        c = c - t
    return out


ALIBI_TERMS = tuple(tuple(_bf16_terms(s * LOG2E) for s in row) for row in SLOPES)


def _slot_base(kvh):
    return HEAD_DIM if kvh == 0 else 0


def _augmented_keys(k_t, kvh, with_selection):
    n = k_t.shape[1]
    row = lax.broadcasted_iota(jnp.int32, (KV_WIDTH, 1), 0)
    kpos = lax.broadcasted_iota(jnp.int32, (1, n), 1)
    e = row - _slot_base(kvh)
    mine = (e < 0) | (e >= HEAD_DIM)
    k_hi = ((kpos >> 7) << 7).astype(F32)
    k_lo = (kpos & (LANES - 1)).astype(F32)
    extra = jnp.where((e >= ALIBI_SLOT0) & (e < ALIBI_SLOT0 + 3), k_hi,
                      jnp.where((e >= ALIBI_SLOT0 + 3) & (e < ALIBI_SLOT0 + 6), k_lo, 0.0))
    if with_selection:
        extra = jnp.where((e >= 0) & (e < N_SEL_SLOTS) & ((kpos >> int(math.log2(SEL_BLOCK))) == e), 1.0, extra)
    return jnp.where(mine, k_t, extra).astype(BF16)


def _augmented_queries(q_tile, kvh, sel):
    lane = lax.broadcasted_iota(jnp.int32, (1, LANES), 1)
    e = lane - _slot_base(kvh)
    keep = (e < 0) | (e >= HEAD_DIM)
    if sel is None:
        sel_extra = 0.0
    else:
        sel_at = sel if _slot_base(kvh) == 0 else pltpu.roll(sel, HEAD_DIM, 1)
        sel_extra = jnp.where((e >= 0) & (e < N_SEL_SLOTS), (sel_at - 1.0) * MASK_BIG, 0.0)
    rows = []
    for g in range(NSA_GROUP):
        alibi = jnp.zeros((1, LANES), F32)
        for i, term in enumerate(ALIBI_TERMS[kvh][g] * 2):
            alibi = jnp.where(e == ALIBI_SLOT0 + i, term, alibi)
        rows.append(jnp.where(keep, q_tile[:, g * LANES:(g + 1) * LANES] * (SCALE * LOG2E), sel_extra + alibi))
    return jnp.concatenate(rows, axis=0).astype(BF16)


def _augmented_values(v_t, kvh):
    mine = (lax.broadcasted_iota(jnp.int32, (KV_WIDTH, 1), 0) < HEAD_DIM) == (kvh == 0)
    return jnp.where(mine, v_t, 1.0).astype(BF16)


def _attend_group(s_rows, bias, v_aug, kvh, tq):
    ps = []
    for g in range(NSA_GROUP):
        s = s_rows[g * tq:(g + 1) * tq] + bias
        ps.append(jnp.exp2((s - jnp.max(s, axis=-1, keepdims=True)).astype(BF16)))
    o = _dot_nt(jnp.concatenate(ps, axis=0), v_aug)
    denom_lane = HEAD_DIM if kvh == 0 else 0
    return o, 1.0 / o[:, denom_lane:denom_lane + 1]


def _split_order_block(w):
    lane = lax.broadcasted_iota(jnp.int32, (1, w), 1)
    return jnp.where(lane < w // 2, 2 * lane, 2 * (lane - w // 2) + 1)


def _compressed_probs(s_rows, qpos, kvh, tq, n_cmp, blk_of_lane=None):
    w = s_rows.shape[1]
    lane = lax.broadcasted_iota(jnp.int32, (1, w), 1) if blk_of_lane is None else blk_of_lane
    dist = qpos - ((lane + 1) * CMP_BLOCK - 1)
    valid = (dist >= 0) & (lane < n_cmp)
    nd = _masked_neg_dist(dist, valid)
    ps = []
    imp = jnp.zeros((tq, w), F32)
    for g in range(NSA_GROUP):
        s = s_rows[g * tq:(g + 1) * tq] + (SLOPES[kvh][g] * LOG2E) * nd
        e = jnp.exp2(s - jnp.max(s, axis=-1, keepdims=True))
        p = jnp.where(valid, e / jnp.sum(e, axis=-1, keepdims=True), 0.0)
        imp = imp + p
        ps.append(p)
    return jnp.concatenate(ps, axis=0).astype(BF16), imp


def _select_blocks(imp, qpos, n_sel):
    tq, w = imp.shape
    lane = lax.broadcasted_iota(jnp.int32, (1, w), 1)
    pair = imp + pltpu.roll(imp, w - 1, 1)
    blk = lane >> 1
    cur = qpos >> int(math.log2(SEL_BLOCK))
    is_blk = ((lane & 1) == 0) & (blk < n_sel)
    forced = (blk == 0) | (blk == cur) | (blk == cur - 1)
    score = jnp.where(forced, FORCE_SCORE, jnp.where(blk <= cur, pair, NEG_INF))
    score = jnp.where(is_blk, score, BELOW_ALL)
    rank = jnp.zeros((tq, w), jnp.int32)
    for i in range(n_sel):
        col = score[:, 2 * i:2 * i + 1]
        beats = (col > score) | ((col == score) & (lane > 2 * i))
        rank = rank + jnp.where(beats, 1, 0)
    return jnp.where((rank < min(TOP_K, n_sel)) & is_blk, 1.0, 0.0)


def _select_blocks_t(imp, qpos_row, n_sel):
    tq, w = imp.shape
    pair_t = (imp + pltpu.roll(imp, w // 2, 1)).T
    nb = -(-n_sel // SUBLANES) * SUBLANES
    blk = lax.broadcasted_iota(jnp.int32, (nb, 1), 0)
    cur = qpos_row >> int(math.log2(SEL_BLOCK))
    forced = (blk == 0) | (blk == cur) | (blk == cur - 1)
    score = jnp.where(forced, FORCE_SCORE, jnp.where(blk <= cur, pair_t[0:nb], NEG_INF))
    score = jnp.where(blk < n_sel, score, BELOW_ALL)
    rank = jnp.zeros((nb, tq), jnp.int32)
    for i in range(n_sel):
        row = score[i:i + 1, :]
        beats = (row > score) | ((row == score) & (blk > i))
        rank = rank + jnp.where(beats, 1, 0)
    sel_t = jnp.where((rank < min(TOP_K, n_sel)) & (blk < n_sel), 1.0, 0.0)
    return _pad_rows(sel_t, w).T


def _gated_sum(gates, kvh, tq, branches):
    out = []
    for g in range(NSA_GROUP):
        h = kvh * NSA_GROUP + g
        rows = slice(g * tq, (g + 1) * tq)
        out.append(sum(gates[:, c * NSA_HEADS + h:c * NSA_HEADS + h + 1] * o[rows] for c, o in enumerate(branches)))
    return out


def _gated_sum_scaled(gates, kvh, tq, branches):
    total = None
    for c, (o, scale) in enumerate(branches):
        col = jnp.concatenate([gates[:, c * NSA_HEADS + kvh * NSA_GROUP + g:c * NSA_HEADS + kvh * NSA_GROUP + g + 1]
                               for g in range(NSA_GROUP)], axis=0)
        term = o * (col if scale is None else col * scale)
        total = term if total is None else total + term
    return [total[g * tq:(g + 1) * tq] for g in range(NSA_GROUP)]


def _store_slabs(o_ref, per_group):
    lo = _low_half()
    for g in range(NSA_GROUP):
        o_ref[:, g * LANES:(g + 1) * LANES] = jnp.where(lo, per_group[0][g], per_group[1][g])


def _nsa_prompt_kernel(q_ref, kv_ref, kvw_ref, gate_ref, wc_ref, o_ref, kc_ref, vc_ref, ks_ref, kw_ref, vs_ref, vw_ref,
                       *, t_len, tq, key_step):
    qi = pl.program_id(1)
    n_cmp = t_len // CMP_BLOCK
    n_sel = -(-t_len // SEL_BLOCK)
    w = kc_ref.shape[1]
    stream = lambda s: slice(s * KV_WIDTH, (s + 1) * KV_WIDTH)

    @pl.when(qi == 0)
    def _():
        kc_ref[...] = _dot_f32(kv_ref[stream(0), :], wc_ref[0]).astype(BF16)
        vc_ref[...] = _dot_f32(kv_ref[stream(1), :], wc_ref[1]).astype(BF16)
        for kvh in range(NSA_KV_HEADS):
            ks_ref[kvh] = _augmented_keys(kv_ref[stream(2), :], kvh, True)
            kw_ref[kvh] = _augmented_keys(kvw_ref[stream(0), :], kvh, False)
            vs_ref[kvh] = _augmented_values(kv_ref[stream(3), :], kvh)
            vw_ref[kvh] = _augmented_values(kvw_ref[stream(1), :], kvh)

    q0 = qi * tq
    qpos = q0 + lax.broadcasted_iota(jnp.int32, (tq, 1), 0)
    qpos_row = q0 + lax.broadcasted_iota(jnp.int32, (1, tq), 1)
    blk_of_lane = _split_order_block(w)
    n_win = min(WINDOW + tq, t_len)
    w_start = pl.multiple_of(jnp.maximum(q0 + tq - n_win, 0), LANES)

    def body(n_keys):
        q_tile = q_ref[...]
        gates = gate_ref[...]
        dist_s = qpos - lax.broadcasted_iota(jnp.int32, (1, n_keys), 1)
        dist_w = qpos - (w_start + lax.broadcasted_iota(jnp.int32, (1, n_win), 1))
        bias_s = jnp.where(dist_s >= 0, 0.0, NEG_INF)
        bias_w = jnp.where((dist_w >= 0) & (dist_w < WINDOW), 0.0, NEG_INF)
        per_group = []
        for kvh in range(NSA_KV_HEADS):
            qs = _group_queries(q_tile, kvh)
            p_c, imp = _compressed_probs(_dot(qs, kc_ref[...]), qpos, kvh, tq, n_cmp, blk_of_lane)
            o_c = _dot_nt(p_c, vc_ref[...])
            sel = _select_blocks_t(imp, qpos_row, n_sel)
            s_s = _dot(_augmented_queries(q_tile, kvh, sel), ks_ref[kvh, :, 0:n_keys])
            o_s, inv_s = _attend_group(s_s, bias_s, vs_ref[kvh, :, 0:n_keys], kvh, tq)
            s_w = _dot(_augmented_queries(q_tile, kvh, None), kw_ref[kvh, :, pl.ds(w_start, n_win)])
            o_w, inv_w = _attend_group(s_w, bias_w, vw_ref[kvh, :, pl.ds(w_start, n_win)], kvh, tq)
            per_group.append(_gated_sum_scaled(gates, kvh, tq, [(o_c, None), (o_s, inv_s), (o_w, inv_w)]))
        _store_slabs(o_ref, per_group)

    n_classes = -(-t_len // key_step)
    for c in range(n_classes):
        n_keys = min((c + 1) * key_step, t_len)

        @pl.when((q0 + tq - 1) // key_step == c)
        def _(n_keys=n_keys):
            body(n_keys)


def _nsa_prompt(qn, rows_t, win_t, gates, wc, layer, bsz, t_len):
    tq = math.gcd(t_len, 128)
    key_step = math.gcd(t_len, 256)
    w = wc.shape[2]
    assert tq == LANES and w == LANES
    assert -(-t_len // SEL_BLOCK) <= N_SEL_SLOTS
    kern = functools.partial(_nsa_prompt_kernel, t_len=t_len, tq=tq, key_step=key_step)
    return pl.pallas_call(
        kern,
        grid=(bsz, t_len // tq),
        in_specs=[
            pl.BlockSpec((None, tq, Q_WIDTH), lambda b, i: (b, i, 0)),
            pl.BlockSpec((None, None, 4 * KV_WIDTH, t_len), lambda b, i: (layer, b, 0, 0)),
            pl.BlockSpec((None, None, 2 * KV_WIDTH, t_len), lambda b, i: (layer, b, 0, 0)),
            pl.BlockSpec((None, tq, MISC_WIDTH), lambda b, i: (b, i, 0)),
            pl.BlockSpec(wc.shape, lambda b, i: (0, 0, 0)),
        ],
        out_specs=pl.BlockSpec((None, tq, Q_WIDTH), lambda b, i: (b, i, 0)),
        out_shape=jax.ShapeDtypeStruct((bsz, t_len, Q_WIDTH), F32),
        scratch_shapes=[pltpu.VMEM((KV_WIDTH, w), BF16), pltpu.VMEM((KV_WIDTH, w), BF16),
                        pltpu.VMEM((NSA_KV_HEADS, KV_WIDTH, t_len), BF16), pltpu.VMEM((NSA_KV_HEADS, KV_WIDTH, t_len), BF16),
                        pltpu.VMEM((NSA_KV_HEADS, KV_WIDTH, t_len), BF16), pltpu.VMEM((NSA_KV_HEADS, KV_WIDTH, t_len), BF16)],
        compiler_params=pltpu.CompilerParams(dimension_semantics=("arbitrary", "arbitrary"), vmem_limit_bytes=VMEM_LIMIT),
        name="nsa_prompt",
    )(qn.reshape(bsz, t_len, Q_WIDTH), rows_t, win_t, gates.reshape(bsz, t_len, MISC_WIDTH), wc)


def _pad_rows(a, n):
    return jnp.concatenate([a, jnp.zeros((n - a.shape[0], a.shape[1]), a.dtype)], axis=0)


def _nsa_decode_kernel(pt_ref, q_ref, kv_ref, gate_ref, win_ref, wpos_ref, *rest, ts, past_len, page_size, n_pages, w):
    page_refs = rest[:n_pages]
    o_ref, win_out_ref, s_ref, p_ref = rest[n_pages:]
    del pt_ref
    n_cmp = past_len // CMP_BLOCK
    n_sel = past_len // SEL_BLOCK + 1
    n_keys = past_len + page_size
    n_win = win_ref.shape[1]
    rows_q = NSA_GROUP * ts
    stream = lambda s: slice(s * KV_WIDTH, (s + 1) * KV_WIDTH)

    new = kv_ref[...]
    q_tile = q_ref[...]
    gates = gate_ref[...]
    qs = jnp.concatenate([_group_queries(q_tile, kvh) for kvh in range(NSA_KV_HEADS)], axis=0)
    qpos = past_len + lax.broadcasted_iota(jnp.int32, (ts, 1), 0)

    per_pair = 2 * page_size // CMP_BLOCK
    wk = jnp.concatenate([wpos_ref[0]] * per_pair, axis=0).reshape(per_pair, CMP_BLOCK, KV_WIDTH)
    wv = jnp.concatenate([wpos_ref[1]] * per_pair, axis=0).reshape(per_pair, CMP_BLOCK, KV_WIDTH)
    kcs, vcs = [], []
    for j in range(0, n_pages, 2):
        k2 = jnp.concatenate([page_refs[j][stream(0), :].T, page_refs[j + 1][stream(0), :].T], axis=0)
        v2 = jnp.concatenate([page_refs[j][stream(1), :].T, page_refs[j + 1][stream(1), :].T], axis=0)
        kcs.append(jnp.sum(k2.reshape(per_pair, CMP_BLOCK, KV_WIDTH) * wk, axis=1))
        vcs.append(jnp.sum(v2.reshape(per_pair, CMP_BLOCK, KV_WIDTH) * wv, axis=1))
    kc = _pad_rows(jnp.concatenate(kcs, axis=0), w).astype(BF16)
    vc = _pad_rows(jnp.concatenate(vcs, axis=0), w).astype(BF16)

    s_c = _dot_nt(qs, kc)
    p_cs, sels = [], []
    for kvh in range(NSA_KV_HEADS):
        p_c, imp = _compressed_probs(s_c[kvh * rows_q:(kvh + 1) * rows_q], qpos, kvh, ts, n_cmp)
        p_cs.append(p_c)
        sels.append(_select_blocks(imp, qpos, n_sel))
    o_c = _dot(jnp.concatenate(p_cs, axis=0), vc)

    new_k = _pad_rows(new[:, stream(2)], page_size).astype(BF16)
    new_v = _pad_rows(new[:, stream(3)], page_size).astype(BF16)
    for j in range(n_pages):
        s_ref[:, j * page_size:(j + 1) * page_size] = _dot(qs, page_refs[j][stream(2), :].astype(BF16))
    s_ref[:, past_len:n_keys] = _dot_nt(qs, new_k)
    dist_s = qpos - lax.broadcasted_iota(jnp.int32, (1, n_keys), 1)
    blocks_per_page = page_size // SEL_BLOCK
    page_lane_blk = lax.broadcasted_iota(jnp.int32, (1, page_size), 1) >> int(math.log2(SEL_BLOCK))
    invs = []
    for kvh in range(NSA_KV_HEADS):
        sel = sels[kvh]
        pieces = []
        for j in range(n_pages + 1):
            piece = jnp.zeros((ts, page_size), F32)
            for r in range(blocks_per_page):
                blk = j * blocks_per_page + r
                if blk < n_sel:
                    piece = jnp.where(page_lane_blk == r, sel[:, 2 * blk:2 * blk + 1], piece)
            pieces.append(piece)
        keymask = jnp.concatenate(pieces, axis=1)
        nd_s = _masked_neg_dist(dist_s, (keymask > 0.5) & (dist_s >= 0))
        p_s, inv_s = _softmax_rows(s_ref[kvh * rows_q:(kvh + 1) * rows_q, :], nd_s, kvh, ts)
        p_ref[kvh * rows_q:(kvh + 1) * rows_q, :] = p_s
        invs.append(inv_s)
    o_s = _dot(p_ref[:, past_len:n_keys], new_v)
    for j in range(n_pages):
        o_s = o_s + _dot_nt(p_ref[:, j * page_size:(j + 1) * page_size], page_refs[j][stream(3), :].astype(BF16))
    o_s = o_s * jnp.concatenate(invs, axis=0)

    new_kw = _pad_rows(new[:, stream(4)], LANES).astype(BF16)
    new_vw = _pad_rows(new[:, stream(5)], LANES).astype(BF16)
    s_w = jnp.concatenate([_dot(qs, win_ref[0:KV_WIDTH, :].astype(BF16)), _dot_nt(qs, new_kw)], axis=1)
    wpos = jnp.concatenate([past_len - n_win + lax.broadcasted_iota(jnp.int32, (1, n_win), 1),
                            past_len + lax.broadcasted_iota(jnp.int32, (1, LANES), 1)], axis=1)
    dist_w = qpos - wpos
    nd_w = _masked_neg_dist(dist_w, (dist_w >= 0) & (dist_w < WINDOW))
    p_ws, inv_ws = [], []
    for kvh in range(NSA_KV_HEADS):
        p_w, inv_w = _softmax_rows(s_w[kvh * rows_q:(kvh + 1) * rows_q], nd_w, kvh, ts)
        p_ws.append(p_w)
        inv_ws.append(inv_w)
    p_w = jnp.concatenate(p_ws, axis=0)
    o_w = (_dot_nt(p_w[:, 0:n_win], win_ref[KV_WIDTH:2 * KV_WIDTH, :].astype(BF16)) + _dot(p_w[:, n_win:], new_vw))
    o_w = o_w * jnp.concatenate(inv_ws, axis=0)

    per_group = [_gated_sum(gates, kvh, ts, [o[kvh * rows_q:(kvh + 1) * rows_q] for o in (o_c, o_s, o_w)])
                 for kvh in range(NSA_KV_HEADS)]
    _store_slabs(o_ref, per_group)

    new_t = _pad_rows(new[:, 4 * KV_WIDTH:6 * KV_WIDTH], LANES).T
    new_t = pltpu.roll(new_t, LANES - ts, 1)
    shifted = pltpu.roll(win_ref[...], n_win - ts, 1)
    tail_lane = lax.broadcasted_iota(jnp.int32, (1, LANES), 1)
    win_out_ref[:, 0:n_win - LANES] = shifted[:, 0:n_win - LANES]
    win_out_ref[:, n_win - LANES:n_win] = jnp.where(tail_lane >= LANES - ts, new_t, shifted[:, n_win - LANES:n_win])


def _nsa_decode(qn, kvn, gates, win_t, cache_t, page_table, wpos_rows, layer, dbsz, ts):
    n_pages = page_table.shape[1]
    page_size = cache_t.shape[3]
    past_len = n_pages * page_size
    n_win = win_t.shape[3]
    assert ts < CMP_BLOCK and ts % SUBLANES == 0 and page_size == LANES and n_pages % 2 == 0
    assert n_win == WINDOW and past_len >= WINDOW
    n_cmp = past_len // CMP_BLOCK
    n_sel = past_len // SEL_BLOCK + 1
    w = -(-max(n_cmp, 2 * n_sel) // LANES) * LANES
    n_keys = past_len + page_size
    kern = functools.partial(_nsa_decode_kernel, ts=ts, past_len=past_len, page_size=page_size, n_pages=n_pages, w=w)

    def page_spec(j):
        return pl.BlockSpec((None, None, 4 * KV_WIDTH, page_size), lambda b, pt: (layer, pt[b, j], 0, 0))

    grid_spec = pltpu.PrefetchScalarGridSpec(
        num_scalar_prefetch=1,
        grid=(dbsz,),
        in_specs=[
            pl.BlockSpec((None, ts, Q_WIDTH), lambda b, pt: (b, 0, 0)),
            pl.BlockSpec((None, ts, KV6_WIDTH), lambda b, pt: (b, 0, 0)),
            pl.BlockSpec((None, ts, MISC_WIDTH), lambda b, pt: (b, 0, 0)),
            pl.BlockSpec((None, None, 2 * KV_WIDTH, n_win), lambda b, pt: (layer, b, 0, 0)),
            pl.BlockSpec(wpos_rows.shape, lambda b, pt: (0, 0, 0)),
        ] + [page_spec(j) for j in range(n_pages)],
        out_specs=[
            pl.BlockSpec((None, ts, Q_WIDTH), lambda b, pt: (b, 0, 0)),
            pl.BlockSpec((None, 2 * KV_WIDTH, n_win), lambda b, pt: (b, 0, 0)),
        ],
        scratch_shapes=[pltpu.VMEM((NSA_HEADS * ts, n_keys), F32), pltpu.VMEM((NSA_HEADS * ts, n_keys), BF16)],
    )
    return pl.pallas_call(
        kern,
        grid_spec=grid_spec,
        out_shape=[jax.ShapeDtypeStruct((dbsz, ts, Q_WIDTH), F32), jax.ShapeDtypeStruct((dbsz, 2 * KV_WIDTH, n_win), F32)],
        compiler_params=pltpu.CompilerParams(dimension_semantics=("arbitrary",), vmem_limit_bytes=VMEM_LIMIT),
        name="nsa_decode",
    )(page_table, qn.reshape(dbsz, ts, Q_WIDTH), kvn.reshape(dbsz, ts, KV6_WIDTH), gates.reshape(dbsz, ts, MISC_WIDTH),
      win_t, wpos_rows, *([cache_t] * n_pages))


def _gla_kernel(*refs, t_len, chunk, has_s0, nb):
    for i in range(nb):
        _gla_one([r if j == 5 else r.at[i] for j, r in enumerate(refs)], t_len, chunk, has_s0)


def _gla_one(refs, t_len, chunk, has_s0):
    if has_s0:
        q_ref, k_ref, v_ref, gg_ref, la_ref, gn_ref, s0_ref, o_ref, s_out_ref, st_ref, u_ref, d_ref, sb_ref, qe_ref = refs
    else:
        q_ref, k_ref, v_ref, gg_ref, la_ref, gn_ref, o_ref, s_out_ref, st_ref, u_ref, d_ref, sb_ref, qe_ref = refs
    dk2, dv2 = 2 * GLA_DK, 2 * GLA_DV
    n_chunks = t_len // chunk
    per_group = math.gcd(n_chunks, 4)
    rows_g = per_group * chunk
    n_groups = n_chunks // per_group
    rp = max(rows_g, LANES)
    pad = rp - rows_g

    rr = lax.broadcasted_iota(jnp.int32, (dv2, dk2), 0) // GLA_DV
    cc = lax.broadcasted_iota(jnp.int32, (dv2, dk2), 1) // GLA_DK
    diag = rr == cc
    if has_s0:
        z = jnp.zeros((GLA_DK, GLA_DV), F32)
        s_full = jnp.concatenate([jnp.concatenate([s0_ref[0], z], axis=1), jnp.concatenate([z, s0_ref[1]], axis=1)], axis=0)
        st_ref[...] = s_full.T
    else:
        st_ref[...] = jnp.zeros((dv2, dk2), F32)

    lane_head = lax.broadcasted_iota(jnp.int32, (1, dk2), 1) // GLA_DK
    trow = lax.broadcasted_iota(jnp.int32, (rows_g, rp), 0)
    tcol = lax.broadcasted_iota(jnp.int32, (rows_g, rp), 1)
    causal = (tcol <= trow) & (tcol // chunk == trow // chunk)
    tril = jnp.where(causal, 1.0, 0.0).astype(BF16)
    prow_chunk = lax.broadcasted_iota(jnp.int32, (rp, 1), 0) // chunk
    mid = chunk // 2
    gn = gn_ref[...]

    def pad_rows(a):
        return a if pad == 0 else jnp.concatenate([a, jnp.zeros((pad, a.shape[1]), a.dtype)], axis=0)

    def aligned(start, multiple):
        return start if isinstance(start, int) else pl.multiple_of(start, multiple)

    def per_chunk_row(cum, r):
        return jnp.concatenate([jnp.broadcast_to(cum[c * chunk + r:c * chunk + r + 1, :], (chunk, dk2))
                                for c in range(per_group)], axis=0)

    def group_local(gi, carry):
        r0 = aligned(gi * rows_g, rows_g)
        rows = pl.ds(r0, rows_g)
        q = q_ref[rows, :] * (GLA_DK ** -0.5)
        k = k_ref[rows, :]
        la0, la1, la2 = _split3(pad_rows(la_ref[rows, :]))
        cum = _dot(tril, la0) + _dot(tril, la1) + _dot(tril, la2)
        m = per_chunk_row(cum, mid)
        last = per_chunk_row(cum, chunk - 1)
        qe_ref[rows, :] = q * jnp.exp(cum)
        qs = q * jnp.exp(cum - m)
        ks = pad_rows((k * jnp.exp(m - cum)).astype(BF16))
        vp = pad_rows(v_ref[rows, :].astype(BF16))
        intra = []
        for h in range(2):
            a = _dot_nt(jnp.where(lane_head == h, qs, 0.0).astype(BF16), ks)
            a = jnp.where(causal, a, 0.0).astype(BF16)
            intra.append(_dot(a, vp[:, h * GLA_DV:(h + 1) * GLA_DV]))
        o_ref[rows, :] = jnp.concatenate(intra, axis=1)
        kd = pad_rows(k * jnp.exp(last - cum))
        for c in range(per_group):
            ci = gi * per_group + c
            kd_c = jnp.where(prow_chunk == c, kd, 0.0).astype(BF16)
            u_ref[ci] = jnp.where(diag, _dot_tn(vp, kd_c), 0.0)
            d_ref[ci] = jnp.exp(last[c * chunk:c * chunk + SUBLANES, :])
        return carry

    def chunk_state(ci, carry):
        st = st_ref[...]
        sb_ref[ci] = st.astype(BF16)
        st_ref[...] = st * d_ref[ci][0:1, :] + u_ref[ci]
        return carry

    def group_output(gi, carry):
        r0 = aligned(gi * rows_g, rows_g)
        rows = pl.ds(r0, rows_g)
        inter = [_dot_nt(qe_ref[pl.ds(aligned(r0 + c * chunk, chunk), chunk), :].astype(BF16),
                         sb_ref[gi * per_group + c]) for c in range(per_group)]
        o = o_ref[rows, :] + jnp.concatenate(inter, axis=0)
        gg = gg_ref[rows, :]
        outs = []
        for h in range(2):
            oh = o[:, h * GLA_DV:(h + 1) * GLA_DV]
            y = oh * lax.rsqrt(jnp.mean(oh * oh, axis=-1, keepdims=True) + EPS) * gn
            gh = gg[:, h * GLA_DV:(h + 1) * GLA_DV]
            outs.append(y * (gh * _sigmoid(gh)))
        o_ref[rows, :] = jnp.concatenate(outs, axis=1)
        return carry

    def loop(n, body):
        if n == 1:
            body(0, 0)
        else:
            lax.fori_loop(0, n, body, 0)

    loop(n_groups, group_local)
    loop(n_chunks, chunk_state)
    loop(n_groups, group_output)
    s_fin = st_ref[...].T
    s_out_ref[0] = s_fin[0:GLA_DK, 0:GLA_DV]
    s_out_ref[1] = s_fin[GLA_DK:dk2, GLA_DV:dv2]


def _gla(g, la, gnorm, s0, bsz, t_len):
    chunk = math.gcd(t_len, GLA_CHUNK)
    n_chunks = t_len // chunk
    nb = math.gcd(bsz, 8) if n_chunks == 1 else 1
    has_s0 = s0 is not None
    kern = functools.partial(_gla_kernel, t_len=t_len, chunk=chunk, has_s0=has_s0, nb=nb)
    qk_blk = lambda off: pl.BlockSpec((nb, t_len, 2 * GLA_DK), lambda b, p: (b, 0, off + p))
    v_blk = lambda off: pl.BlockSpec((nb, t_len, 2 * GLA_DV), lambda b, p: (b, 0, off + p))
    g3 = g.reshape(bsz, t_len, G_WIDTH)
    n_qk = GLA_QK_WIDTH // (2 * GLA_DK)
    in_specs = [qk_blk(0), qk_blk(n_qk), v_blk(n_qk), v_blk(n_qk + GLA_V_WIDTH // (2 * GLA_DV)),
                pl.BlockSpec((nb, t_len, 2 * GLA_DK), lambda b, p: (b, 0, p)),
                pl.BlockSpec(gnorm.shape, lambda b, p: (0, 0))]
    args = [g3, g3, g3, g3, la.reshape(bsz, t_len, GLA_QK_WIDTH), gnorm]
    if has_s0:
        in_specs.append(pl.BlockSpec((nb, 2, GLA_DK, GLA_DV), lambda b, p: (b, p, 0, 0)))
        args.append(s0)
    return pl.pallas_call(
        kern,
        grid=(bsz // nb, GLA_HEADS // 2),
        in_specs=in_specs,
        out_specs=[pl.BlockSpec((nb, t_len, 2 * GLA_DV), lambda b, p: (b, 0, p)),
                   pl.BlockSpec((nb, 2, GLA_DK, GLA_DV), lambda b, p: (b, p, 0, 0))],
        out_shape=[jax.ShapeDtypeStruct((bsz, t_len, GLA_V_WIDTH), F32),
                   jax.ShapeDtypeStruct((bsz, GLA_HEADS, GLA_DK, GLA_DV), F32)],
        scratch_shapes=[pltpu.VMEM((nb, 2 * GLA_DV, 2 * GLA_DK), F32),
                        pltpu.VMEM((nb, n_chunks, 2 * GLA_DV, 2 * GLA_DK), F32),
                        pltpu.VMEM((nb, n_chunks, SUBLANES, 2 * GLA_DK), F32),
                        pltpu.VMEM((nb, n_chunks, 2 * GLA_DV, 2 * GLA_DK), BF16),
                        pltpu.VMEM((nb, t_len, 2 * GLA_DK), F32)],
        compiler_params=pltpu.CompilerParams(dimension_semantics=("arbitrary", "arbitrary"), vmem_limit_bytes=VMEM_LIMIT),
        name="gla",
    )(*args)


FF_STEPS = 2
FF_CHUNK = D_FF // FF_STEPS
assert FF_CHUNK * FF_STEPS == D_FF and FF_CHUNK % LANES == 0


def _out_ffn_kernel(x_ref, on_ref, og_ref, wo_n_ref, wo_g_ref, ln_ref, wg_ref, wu_ref, wd_ref, y_ref, h_ref):
    j = pl.program_id(1)

    @pl.when(j == 0)
    def _():
        x1 = x_ref[...] + _dot(on_ref[...].astype(BF16), wo_n_ref[...]) + _dot(og_ref[...].astype(BF16), wo_g_ref[...])
        h_ref[...] = (x1 * lax.rsqrt(jnp.mean(x1 * x1, axis=-1, keepdims=True) + EPS) * ln_ref[...]).astype(BF16)
        y_ref[...] = x1

    h = h_ref[...]
    gate = _dot(h, wg_ref[...])
    up = _dot(h, wu_ref[...])
    y_ref[...] += _dot((gate * _sigmoid(gate) * up).astype(BF16), wd_ref[...])


def _out_ffn(x, o_nsa, o_gla, wo_n, wo_g, ln, wg, wu, wd, tm):
    n = x.shape[0]
    row = lambda w: pl.BlockSpec((tm, w), lambda i, j: (i, 0))
    full = lambda a: pl.BlockSpec(a.shape, lambda i, j: (0,) * a.ndim)
    return pl.pallas_call(
        _out_ffn_kernel,
        grid=(n // tm, FF_STEPS),
        in_specs=[row(D_MODEL), row(Q_WIDTH), row(GLA_V_WIDTH), full(wo_n), full(wo_g), full(ln),
                  pl.BlockSpec((D_MODEL, FF_CHUNK), lambda i, j: (0, j)),
                  pl.BlockSpec((D_MODEL, FF_CHUNK), lambda i, j: (0, j)),
                  pl.BlockSpec((FF_CHUNK, D_MODEL), lambda i, j: (j, 0))],
        out_specs=row(D_MODEL),
        out_shape=jax.ShapeDtypeStruct((n, D_MODEL), F32),
        scratch_shapes=[pltpu.VMEM((tm, D_MODEL), BF16)],
        compiler_params=pltpu.CompilerParams(dimension_semantics=("arbitrary", "arbitrary"), vmem_limit_bytes=VMEM_LIMIT),
        name="out_ffn",
    )(x, o_nsa, o_gla, wo_n, wo_g, ln, wg, wu, wd)


def _layer_weights(ln_mix, w_in, q_norm, k_norm, cmp_pos_w, w_a2, b_a, gla_norm, w_out, ln_ffn, w_gate, w_up, w_down,
                   tm_prompt, t_len):
    w_in_t = w_in.T
    o = 0
    parts = []
    for width in (Q_WIDTH, KV6_WIDTH, N_GATES, GLA_QK_WIDTH, GLA_QK_WIDTH, GLA_V_WIDTH, GLA_V_WIDTH, GLA_RANK):
        parts.append(w_in_t[o:o + width])
        o += width
    wq, wkv, wgate, wgq, wgk, wgv, wgg, wga = parts
    wq = jnp.concatenate([wq[h * HEAD_DIM:(h + 1) * HEAD_DIM] for h in SLAB_HEADS], axis=0)
    pad = jnp.zeros((MISC_WIDTH - N_GATES - GLA_RANK, D_MODEL), w_in.dtype)
    wa = jnp.zeros((MISC_WIDTH, GLA_QK_WIDTH), F32).at[N_GATES:N_GATES + GLA_RANK].set(w_a2)
    n_cmp = t_len // CMP_BLOCK
    w_lanes = -(-max(n_cmp, 2 * (-(-t_len // SEL_BLOCK))) // LANES) * LANES
    tok = jnp.arange(t_len)
    lane = jnp.arange(w_lanes)
    blk_of_lane = jnp.where(lane < w_lanes // 2, 2 * lane, 2 * (lane - w_lanes // 2) + 1)
    in_blk = (tok[:, None] // CMP_BLOCK == blk_of_lane[None, :]) & (tok[:, None] < n_cmp * CMP_BLOCK)
    wc = jnp.where(in_blk[None], cmp_pos_w[:, tok % CMP_BLOCK][:, :, None], 0.0)
    kg_rows = jnp.tile(k_norm, (1, NSA_KV_HEADS))
    return dict(
        ln_mix=ln_mix.reshape(1, D_MODEL),
        wq=wq.astype(BF16), wkv=wkv.astype(BF16),
        wg=jnp.concatenate([wgq, wgk, wgv, wgg], axis=0).astype(BF16),
        wm=jnp.concatenate([wgate, wga, pad], axis=0).astype(BF16),
        qg=jnp.tile(q_norm, LANES // HEAD_DIM).reshape(1, LANES),
        kg=kg_rows, kg_t=jnp.broadcast_to(kg_rows[:, :, None], (3, KV_WIDTH, tm_prompt)),
        wa=wa, ba=b_a.reshape(1, GLA_QK_WIDTH),
        wpos=jnp.broadcast_to(cmp_pos_w[:, :, None], (2, CMP_BLOCK, LANES)), wc=wc,
        gnorm=gla_norm.reshape(1, GLA_DV),
        wo_n=jnp.concatenate([w_out[h * HEAD_DIM:(h + 1) * HEAD_DIM] for h in SLAB_HEADS], axis=0).astype(BF16),
        wo_g=w_out[Q_WIDTH:].astype(BF16),
        ln_ffn=ln_ffn.reshape(1, D_MODEL),
        w_gate=w_gate.astype(BF16), w_up=w_up.astype(BF16), w_down=w_down.astype(BF16),
    )


def _row_tile(n):
    return math.gcd(n, 512)


def _mix_and_ffn(x, w, attn, s0, bsz, t_len, kv_bufs=None, layer=None):
    tm = _row_tile(x.shape[0])
    qn, *kv, g, gates, la = _proj_in(x, w, tm, kv_bufs, layer)
    o_nsa, extra = attn(qn, kv, gates)
    o_gla, s_new = _gla(g, la, w["gnorm"], s0, bsz, t_len)
    y = _out_ffn(x, o_nsa.reshape(-1, Q_WIDTH), o_gla.reshape(-1, GLA_V_WIDTH), w["wo_n"], w["wo_g"], w["ln_ffn"],
                 w["w_gate"], w["w_up"], w["w_down"], tm)
    return y, kv, s_new, extra


def _token_major(a_t, lead):
    n_lead = len(lead)
    a = a_t.reshape(*lead, -1, NSA_KV_HEADS, HEAD_DIM, a_t.shape[-1])
    return a.transpose(*range(n_lead), n_lead + 3, n_lead, n_lead + 1, n_lead + 2)


def kernel(x_prompt, x_sample, cache_nsa_kv, state_nsa_win, state_gla, page_table, ln_mix, w_in, q_norm, k_norm, cmp_pos_w,
           w_a2, b_a, gla_norm, w_out, ln_ffn, w_gate, w_up, w_down):
    bsz, t_len = x_prompt.shape[:2]
    dbsz, ts = x_sample.shape[:2]
    depth = w_in.shape[0]
    n_phys, page_size = cache_nsa_kv.shape[1:3]
    n_win = state_nsa_win.shape[2]
    cache_t = cache_nsa_kv.transpose(0, 1, 3, 4, 5, 2).reshape(depth, n_phys, 4 * KV_WIDTH, page_size)
    win_t = state_nsa_win.transpose(0, 1, 3, 4, 5, 2).reshape(depth, dbsz, 2 * KV_WIDTH, n_win)
    keep_p = min(WINDOW, t_len)
    tm_prompt = _row_tile(bsz * t_len)
    assert t_len % tm_prompt == 0

    yp = x_prompt.reshape(bsz * t_len, D_MODEL)
    ys = x_sample.reshape(dbsz * ts, D_MODEL)
    kv_bufs = [jnp.zeros((depth, bsz, 4 * KV_WIDTH, t_len), F32), jnp.zeros((depth, bsz, 2 * KV_WIDTH, t_len), F32)]
    gla_p, rows_s, win_s, gla_s = [], [], [], []
    for l in range(depth):
        w = _layer_weights(ln_mix[l], w_in[l], q_norm[l], k_norm[l], cmp_pos_w[l], w_a2[l], b_a[l], gla_norm[l], w_out[l],
                           ln_ffn[l], w_gate[l], w_up[l], w_down[l], tm_prompt, t_len)

        def attn_prompt(qn, kv, gates):
            return _nsa_prompt(qn, kv[0], kv[1], gates, w["wc"], l, bsz, t_len), None

        def attn_sample(qn, kv, gates):
            return _nsa_decode(qn, kv[0], gates, win_t, cache_t, page_table, w["wpos"], l, dbsz, ts)

        yp, kv_bufs, st_p, _ = _mix_and_ffn(yp, w, attn_prompt, None, bsz, t_len, kv_bufs, l)
        ys, (kvn_s,), st_s, nw_s = _mix_and_ffn(ys, w, attn_sample, state_gla[l], dbsz, ts)

        gla_p.append(st_p.astype(state_gla.dtype))
        rows_s.append(kvn_s.reshape(dbsz, ts, 6, NSA_KV_HEADS, HEAD_DIM)[:, :, :4])
        win_s.append(nw_s)
        gla_s.append(st_s.astype(state_gla.dtype))
    lead_p, lead_s = (depth, bsz), (depth, dbsz)
    return (yp.reshape(bsz, t_len, D_MODEL), ys.reshape(dbsz, ts, D_MODEL),
            _token_major(kv_bufs[0], lead_p), _token_major(kv_bufs[1][:, :, :, t_len - keep_p:], lead_p), jnp.stack(gla_p),
            jnp.stack(rows_s), _token_major(jnp.stack(win_s), lead_s), jnp.stack(gla_s))
```

```python
import functools
import math

import jax
import jax.numpy as jnp
import numpy as np
from jax import lax
from jax.experimental import pallas as pl
from jax.experimental.pallas import tpu as pltpu

F32 = jnp.float32
BF16 = jnp.bfloat16

D_MODEL = 1024
NSA_HEADS = 8
NSA_KV_HEADS = 2
NSA_GROUP = NSA_HEADS // NSA_KV_HEADS
HEAD_DIM = 64
CMP_BLOCK = 32
SEL_BLOCK = 64
TOP_K = 16
WINDOW = 512
GLA_HEADS = 4
GLA_DK = 64
GLA_DV = 128
GLA_RANK = 16
GLA_TAU = 16.0
GLA_CHUNK = 64
D_FF = -(-(8 * D_MODEL) // (3 * 256)) * 256
KV_WIDTH = NSA_KV_HEADS * HEAD_DIM
Q_WIDTH = NSA_HEADS * HEAD_DIM
N_GATES = 3 * NSA_HEADS
GLA_QK_WIDTH = GLA_HEADS * GLA_DK
GLA_V_WIDTH = GLA_HEADS * GLA_DV
EPS = 1e-6
NEG_INF = -1e30
FORCE_SCORE = 1e4
BELOW_ALL = -3e38
SCALE = HEAD_DIM ** -0.5
LOG2E = 1.4426950408889634
SLOPES = tuple(tuple(2.0 ** (-8.0 * (k * NSA_GROUP + g + 1) / NSA_HEADS) for g in range(NSA_GROUP))
               for k in range(NSA_KV_HEADS))
SLAB_HEADS = tuple(h for g in range(NSA_GROUP) for h in (g, NSA_GROUP + g))

LANES = 128
SUBLANES = 8
VMEM_LIMIT = 56 * 1024 * 1024
assert KV_WIDTH == LANES and NSA_KV_HEADS == 2

G_WIDTH = 2 * GLA_QK_WIDTH + 2 * GLA_V_WIDTH
MISC_WIDTH = LANES
KV6_WIDTH = 6 * KV_WIDTH


def _dot(a, b):
    return jnp.dot(a, b, preferred_element_type=F32)


def _dot_nt(a, b):
    return lax.dot_general(a, b, (((1,), (1,)), ((), ())), preferred_element_type=F32)


def _dot_tn(a, b):
    return lax.dot_general(a, b, (((0,), (0,)), ((), ())), preferred_element_type=F32)


def _split3(a):
    a0 = a.astype(BF16)
    r = a - a0.astype(F32)
    a1 = r.astype(BF16)
    a2 = (r - a1.astype(F32)).astype(BF16)
    return a0, a1, a2


def _dot_f32(a, b):
    a0, a1, a2 = _split3(a)
    b0, b1, b2 = _split3(b)
    return (_dot(a0, b0) + (_dot(a0, b1) + _dot(a1, b0)) + (_dot(a0, b2) + _dot(a1, b1) + _dot(a2, b0)))


def _sigmoid(x):
    return 1.0 / (1.0 + jnp.exp(-x))


def _low_half(width=LANES):
    return lax.broadcasted_iota(jnp.int32, (1, width), 1) < HEAD_DIM


def _half_lane_rms(x, gain):
    x2 = x * x
    lo = _low_half()
    s_lo = jnp.sum(jnp.where(lo, x2, 0.0), axis=-1, keepdims=True)
    s_hi = jnp.sum(jnp.where(lo, 0.0, x2), axis=-1, keepdims=True)
    ms = jnp.where(lo, s_lo, s_hi) * (1.0 / HEAD_DIM)
    return x * lax.rsqrt(ms + EPS) * gain


def _proj_in_kernel(x_ref, ln_ref, wq_ref, wkv_ref, wg_ref, wm_ref, qg_ref, kg_ref, wa_ref, ba_ref, *rest, kv_transposed):
    if kv_transposed:
        _, _, q_out, rows_out, win_out, g_out, gate_out, la_out = rest
    else:
        q_out, kv_out, g_out, gate_out, la_out = rest
    x = x_ref[...]
    h = x * lax.rsqrt(jnp.mean(x * x, axis=-1, keepdims=True) + EPS) * ln_ref[...]
    hb = h.astype(BF16)

    q = _dot_nt(hb, wq_ref[...])
    for j in range(Q_WIDTH // LANES):
        sl = slice(j * LANES, (j + 1) * LANES)
        q_out[:, sl] = _half_lane_rms(q[:, sl], qg_ref[...])

    if kv_transposed:
        kv = _dot_nt(wkv_ref[...], hb)
        tm = kv.shape[1]
        for s in range(6):
            rows = slice(s * KV_WIDTH, (s + 1) * KV_WIDTH)
            out, s_out = (rows_out, s) if s < 4 else (win_out, s - 4)
            dst = slice(s_out * KV_WIDTH, (s_out + 1) * KV_WIDTH)
            if s % 2 == 0:
                k3 = kv[rows].reshape(NSA_KV_HEADS, HEAD_DIM, tm)
                ms = jnp.mean(k3 * k3, axis=1, keepdims=True)
                out[dst, :] = (k3 * lax.rsqrt(ms + EPS)).reshape(KV_WIDTH, tm) * kg_ref[s // 2]
            else:
                out[dst, :] = kv[rows]
    else:
        kv = _dot_nt(hb, wkv_ref[...])
        for s in range(6):
            sl = slice(s * KV_WIDTH, (s + 1) * KV_WIDTH)
            if s % 2 == 0:
                kv_out[:, sl] = _half_lane_rms(kv[:, sl], kg_ref[s // 2:s // 2 + 1, :])
            else:
                kv_out[:, sl] = kv[:, sl]

    g_out[...] = _dot_nt(hb, wg_ref[...])

    m = _dot_nt(hb, wm_ref[...])
    gate_out[...] = _sigmoid(m)
    z = _dot_f32(m, wa_ref[...]) + ba_ref[...]
    la_out[...] = (jnp.minimum(z, 0.0) - jnp.log1p(jnp.exp(-jnp.abs(z)))) * (1.0 / GLA_TAU)


def _proj_in(x, w, tm, kv_bufs=None, layer=None):
    n = x.shape[0]
    row = lambda width: pl.BlockSpec((tm, width), lambda i: (i, 0))
    full = lambda a: pl.BlockSpec(a.shape, lambda i: (0,) * a.ndim)
    args = [x, w["ln_mix"], w["wq"], w["wkv"], w["wg"], w["wm"], w["qg"], w["kg"] if kv_bufs is None else w["kg_t"],
            w["wa"], w["ba"]]
    in_specs = [row(D_MODEL)] + [full(a) for a in args[1:]]
    tail_specs = [row(G_WIDTH), row(MISC_WIDTH), row(GLA_QK_WIDTH)]
    tail_shapes = [jax.ShapeDtypeStruct((n, width), F32) for width in (G_WIDTH, MISC_WIDTH, GLA_QK_WIDTH)]
    q_shape = jax.ShapeDtypeStruct((n, Q_WIDTH), F32)
    if kv_bufs is None:
        kv_specs = [row(KV6_WIDTH)]
        kv_shapes = [jax.ShapeDtypeStruct((n, KV6_WIDTH), F32)]
        aliases = {}
    else:
        tiles = kv_bufs[0].shape[3] // tm
        kv_specs = [pl.BlockSpec((None, None, b.shape[2], tm), lambda i: (layer, i // tiles, 0, i % tiles)) for b in kv_bufs]
        kv_shapes = [jax.ShapeDtypeStruct(b.shape, b.dtype) for b in kv_bufs]
        aliases = {len(args): 1, len(args) + 1: 2}
        in_specs += [pl.BlockSpec(memory_space=pl.ANY)] * 2
        args += list(kv_bufs)
    return pl.pallas_call(
        functools.partial(_proj_in_kernel, kv_transposed=kv_bufs is not None),
        grid=(n // tm,),
        in_specs=in_specs,
        out_specs=[row(Q_WIDTH)] + kv_specs + tail_specs,
        out_shape=[q_shape] + kv_shapes + tail_shapes,
        input_output_aliases=aliases,
        compiler_params=pltpu.CompilerParams(dimension_semantics=("arbitrary",), vmem_limit_bytes=VMEM_LIMIT),
        name="proj_in",
    )(*args)


def _group_queries(q_tile, kvh):
    keep = _low_half() if kvh == 0 else ~_low_half()
    return jnp.concatenate([jnp.where(keep, q_tile[:, g * LANES:(g + 1) * LANES] * (SCALE * LOG2E), 0.0)
                            for g in range(NSA_GROUP)], axis=0).astype(BF16)


def _masked_neg_dist(dist, ok):
    return jnp.where(ok, -dist.astype(F32), NEG_INF)


def _softmax_rows(s_rows, neg_dist, kvh, tq):
    ps, inv = [], []
    for g in range(NSA_GROUP):
        s = s_rows[g * tq:(g + 1) * tq] + (SLOPES[kvh][g] * LOG2E) * neg_dist
        p = jnp.exp2(s - jnp.max(s, axis=-1, keepdims=True))
        inv.append(1.0 / jnp.sum(p, axis=-1, keepdims=True))
        ps.append(p)
    return jnp.concatenate(ps, axis=0).astype(BF16), jnp.concatenate(inv, axis=0)


N_SEL_SLOTS = 32
ALIBI_SLOT0 = N_SEL_SLOTS
MASK_BIG = 2.0 ** 100


def _bf16_terms(c):
    out = []
    for _ in range(3):
        t = float(np.float32(c).astype(BF16).astype(np.float32))
        out.append(t)
        c = c - t
    return out


ALIBI_TERMS = tuple(tuple(_bf16_terms(s * LOG2E) for s in row) for row in SLOPES)


def _slot_base(kvh):
    return HEAD_DIM if kvh == 0 else 0


def _augmented_keys(k_t, kvh, with_selection):
    n = k_t.shape[1]
    row = lax.broadcasted_iota(jnp.int32, (KV_WIDTH, 1), 0)
    kpos = lax.broadcasted_iota(jnp.int32, (1, n), 1)
    e = row - _slot_base(kvh)
    mine = (e < 0) | (e >= HEAD_DIM)
    k_hi = ((kpos >> 7) << 7).astype(F32)
    k_lo = (kpos & (LANES - 1)).astype(F32)
    extra = jnp.where((e >= ALIBI_SLOT0) & (e < ALIBI_SLOT0 + 3), k_hi,
                      jnp.where((e >= ALIBI_SLOT0 + 3) & (e < ALIBI_SLOT0 + 6), k_lo, 0.0))
    if with_selection:
        extra = jnp.where((e >= 0) & (e < N_SEL_SLOTS) & ((kpos >> int(math.log2(SEL_BLOCK))) == e), 1.0, extra)
    return jnp.where(mine, k_t, extra)


def _queries_t(slabs_t, kvh, alibi, sel_t):
    tq = slabs_t[0].shape[1]
    base = _slot_base(kvh)
    e = lax.broadcasted_iota(jnp.int32, (KV_WIDTH, 1), 0) - base
    own = (e < 0) | (e >= HEAD_DIM)
    if sel_t is None:
        sel_rows = None
    else:
        mask = (sel_t - 1.0) * MASK_BIG
        pieces = [jnp.zeros((base, tq), F32)] if base else []
        pieces += [mask, jnp.zeros((KV_WIDTH - base - mask.shape[0], tq), F32)]
        sel_rows = jnp.concatenate(pieces, axis=0)
    cols = []
    for g in range(NSA_GROUP):
        extra = jnp.zeros((KV_WIDTH, 1), F32)
        if alibi:
            for i, term in enumerate(ALIBI_TERMS[kvh][g] * 2):
                extra = jnp.where(e == ALIBI_SLOT0 + i, term, extra)
        if sel_rows is not None:
            extra = extra + sel_rows
        cols.append(jnp.where(own, slabs_t[g], extra))
    return jnp.concatenate(cols, axis=1).astype(BF16)


def _augmented_values(v_t, kvh):
    mine = (lax.broadcasted_iota(jnp.int32, (KV_WIDTH, 1), 0) < HEAD_DIM) == (kvh == 0)
    return jnp.where(mine, v_t, 1.0).astype(BF16)


def _attend_group_t(s_t, bias_t, v_aug, kvh, tq):
    ps = []
    for g in range(NSA_GROUP):
        s = s_t[:, g * tq:(g + 1) * tq] + bias_t
        ps.append(jnp.exp2((s - jnp.max(s, axis=0, keepdims=True)).astype(BF16)))
    o_t = _dot(v_aug, jnp.concatenate(ps, axis=1))
    denom_row = HEAD_DIM if kvh == 0 else 0
    return o_t, 1.0 / o_t[denom_row:denom_row + 1, :]


def _compressed_probs_t(s_t, qpos_row, kvh, tq, n_cmp, blk_of_row):
    dist = qpos_row - ((blk_of_row + 1) * CMP_BLOCK - 1)
    valid = (dist >= 0) & (blk_of_row < n_cmp)
    nd = _masked_neg_dist(dist, valid)
    ps = []
    imp = jnp.zeros(dist.shape, F32)
    for g in range(NSA_GROUP):
        s = s_t[:, g * tq:(g + 1) * tq] + (SLOPES[kvh][g] * LOG2E) * nd
        e = jnp.exp2(s - jnp.max(s, axis=0, keepdims=True))
        p = jnp.where(valid, e / jnp.sum(e, axis=0, keepdims=True), 0.0)
        imp = imp + p
        ps.append(p)
    return jnp.concatenate(ps, axis=1).astype(BF16), imp


def _split_order_block(w):
    slot = lax.broadcasted_iota(jnp.int32, (w, 1), 0)
    return jnp.where(slot < w // 2, 2 * slot, 2 * (slot - w // 2) + 1)


def _compressed_probs(s_rows, qpos, kvh, tq, n_cmp, blk_of_lane=None):
    w = s_rows.shape[1]
    lane = lax.broadcasted_iota(jnp.int32, (1, w), 1) if blk_of_lane is None else blk_of_lane
    dist = qpos - ((lane + 1) * CMP_BLOCK - 1)
    valid = (dist >= 0) & (lane < n_cmp)
    nd = _masked_neg_dist(dist, valid)
    ps = []
    imp = jnp.zeros((tq, w), F32)
    for g in range(NSA_GROUP):
        s = s_rows[g * tq:(g + 1) * tq] + (SLOPES[kvh][g] * LOG2E) * nd
        e = jnp.exp2(s - jnp.max(s, axis=-1, keepdims=True))
        p = jnp.where(valid, e / jnp.sum(e, axis=-1, keepdims=True), 0.0)
        imp = imp + p
        ps.append(p)
    return jnp.concatenate(ps, axis=0).astype(BF16), imp


def _select_blocks(imp, qpos, n_sel):
    tq, w = imp.shape
    lane = lax.broadcasted_iota(jnp.int32, (1, w), 1)
    pair = imp + pltpu.roll(imp, w - 1, 1)
    blk = lane >> 1
    cur = qpos >> int(math.log2(SEL_BLOCK))
    is_blk = ((lane & 1) == 0) & (blk < n_sel)
    forced = (blk == 0) | (blk == cur) | (blk == cur - 1)
    score = jnp.where(forced, FORCE_SCORE, jnp.where(blk <= cur, pair, NEG_INF))
    score = jnp.where(is_blk, score, BELOW_ALL)
    rank = jnp.zeros((tq, w), jnp.int32)
    for i in range(n_sel):
        col = score[:, 2 * i:2 * i + 1]
        beats = (col > score) | ((col == score) & (lane > 2 * i))
        rank = rank + jnp.where(beats, 1, 0)
    return jnp.where((rank < min(TOP_K, n_sel)) & is_blk, 1.0, 0.0)


def _select_blocks_t(imp_t, qpos_row, n_sel):
    w, tq = imp_t.shape
    nb = -(-n_sel // SUBLANES) * SUBLANES
    pair_t = imp_t[0:nb] + imp_t[w // 2:w // 2 + nb]
    blk = lax.broadcasted_iota(jnp.int32, (nb, 1), 0)
    cur = qpos_row >> int(math.log2(SEL_BLOCK))
    forced = (blk == 0) | (blk == cur) | (blk == cur - 1)
    score = jnp.where(forced, FORCE_SCORE, jnp.where(blk <= cur, pair_t, NEG_INF))
    score = jnp.where(blk < n_sel, score, BELOW_ALL)
    rank = jnp.zeros((nb, tq), jnp.int32)
    for i in range(n_sel):
        row = score[i:i + 1, :]
        beats = (row > score) | ((row == score) & (blk > i))
        rank = rank + jnp.where(beats, 1, 0)
    return jnp.where((rank < min(TOP_K, n_sel)) & (blk < n_sel), 1.0, 0.0)


def _gated_sum(gates, kvh, tq, branches):
    out = []
    for g in range(NSA_GROUP):
        h = kvh * NSA_GROUP + g
        rows = slice(g * tq, (g + 1) * tq)
        out.append(sum(gates[:, c * NSA_HEADS + h:c * NSA_HEADS + h + 1] * o[rows] for c, o in enumerate(branches)))
    return out


def _gated_sum_t(gates_t, kvh, branches):
    total = None
    for c, (o_t, scale) in enumerate(branches):
        first = c * NSA_HEADS + kvh * NSA_GROUP
        row = jnp.concatenate([gates_t[first + g:first + g + 1, :] for g in range(NSA_GROUP)], axis=1)
        term = o_t * (row if scale is None else row * scale)
        total = term if total is None else total + term
    return total


def _store_slabs(o_ref, per_group):
    lo = _low_half()
    for g in range(NSA_GROUP):
        o_ref[:, g * LANES:(g + 1) * LANES] = jnp.where(lo, per_group[0][g], per_group[1][g])


def _nsa_prompt_kernel(q_ref, kv_ref, kvw_ref, gate_ref, wc_ref, o_ref, kc_ref, vc_ref, ks_ref, kw_ref, vs_ref, vw_ref,
                       *, t_len, tq, key_step):
    qi = pl.program_id(1)
    n_cmp = t_len // CMP_BLOCK
    n_sel = -(-t_len // SEL_BLOCK)
    w = kc_ref.shape[0]
    stream = lambda s: slice(s * KV_WIDTH, (s + 1) * KV_WIDTH)

    @pl.when(qi == 0)
    def _():
        kc_ref[...] = _dot_f32(kv_ref[stream(0), :], wc_ref[0]).T.astype(BF16)
        vc_ref[...] = _dot_f32(kv_ref[stream(1), :], wc_ref[1]).astype(BF16)
        for kvh in range(NSA_KV_HEADS):
            ks_ref[kvh] = _augmented_keys(kv_ref[stream(2), :], kvh, True).T.astype(BF16)
            kw_ref[kvh] = _augmented_keys(kvw_ref[stream(0), :], kvh, False).T.astype(BF16)
            vs_ref[kvh] = _augmented_values(kv_ref[stream(3), :], kvh)
            vw_ref[kvh] = _augmented_values(kvw_ref[stream(1), :], kvh)

    q0 = qi * tq
    qpos_row = q0 + lax.broadcasted_iota(jnp.int32, (1, tq), 1)
    blk_of_row = _split_order_block(w)
    n_win = min(WINDOW + tq, t_len)
    w_start = pl.multiple_of(jnp.maximum(q0 + tq - n_win, 0), LANES)

    def body(n_keys):
        q_tile = q_ref[...]
        gates_t = gate_ref[...].T
        slabs_t = [(q_tile[:, g * LANES:(g + 1) * LANES] * (SCALE * LOG2E)).T for g in range(NSA_GROUP)]
        dist_s = qpos_row - lax.broadcasted_iota(jnp.int32, (n_keys, 1), 0)
        dist_w = qpos_row - (w_start + lax.broadcasted_iota(jnp.int32, (n_win, 1), 0))
        bias_s = jnp.where(dist_s >= 0, 0.0, NEG_INF)
        bias_w = jnp.where((dist_w >= 0) & (dist_w < WINDOW), 0.0, NEG_INF)
        groups = range(NSA_KV_HEADS)
        s_w = [_dot(kw_ref[kvh, pl.ds(w_start, n_win), :], _queries_t(slabs_t, kvh, True, None)) for kvh in groups]
        o_c, s_s = [], []
        for kvh in groups:
            s_c = _dot(kc_ref[...], _queries_t(slabs_t, kvh, False, None))
            p_c, imp_t = _compressed_probs_t(s_c, qpos_row, kvh, tq, n_cmp, blk_of_row)
            o_c.append(_dot(vc_ref[...], p_c))
            sel_t = _select_blocks_t(imp_t, qpos_row, n_sel)
            s_s.append(_dot(ks_ref[kvh, 0:n_keys, :], _queries_t(slabs_t, kvh, True, sel_t)))
        totals = []
        for kvh in groups:
            o_w, inv_w = _attend_group_t(s_w[kvh], bias_w, vw_ref[kvh, :, pl.ds(w_start, n_win)], kvh, tq)
            o_s, inv_s = _attend_group_t(s_s[kvh], bias_s, vs_ref[kvh, :, 0:n_keys], kvh, tq)
            totals.append(_gated_sum_t(gates_t, kvh, [(o_c[kvh], None), (o_s, inv_s), (o_w, inv_w)]))
        low_rows = lax.broadcasted_iota(jnp.int32, (KV_WIDTH, 1), 0) < HEAD_DIM
        for g in range(NSA_GROUP):
            cols = slice(g * tq, (g + 1) * tq)
            o_ref[:, g * LANES:(g + 1) * LANES] = jnp.where(low_rows, totals[0][:, cols], totals[1][:, cols]).T

    n_classes = -(-t_len // key_step)
    for c in range(n_classes):
        n_keys = min((c + 1) * key_step, t_len)

        @pl.when((q0 + tq - 1) // key_step == c)
        def _(n_keys=n_keys):
            body(n_keys)


def _nsa_prompt(qn, rows_t, win_t, gates, wc, layer, bsz, t_len):
    tq = math.gcd(t_len, 128)
    key_step = math.gcd(t_len, 256)
    w = wc.shape[2]
    assert tq == LANES and w == LANES
    assert -(-t_len // SEL_BLOCK) <= N_SEL_SLOTS
    kern = functools.partial(_nsa_prompt_kernel, t_len=t_len, tq=tq, key_step=key_step)
    return pl.pallas_call(
        kern,
        grid=(bsz, t_len // tq),
        in_specs=[
            pl.BlockSpec((None, tq, Q_WIDTH), lambda b, i: (b, i, 0)),
            pl.BlockSpec((None, None, 4 * KV_WIDTH, t_len), lambda b, i: (layer, b, 0, 0)),
            pl.BlockSpec((None, None, 2 * KV_WIDTH, t_len), lambda b, i: (layer, b, 0, 0)),
            pl.BlockSpec((None, tq, MISC_WIDTH), lambda b, i: (b, i, 0)),
            pl.BlockSpec(wc.shape, lambda b, i: (0, 0, 0)),
        ],
        out_specs=pl.BlockSpec((None, tq, Q_WIDTH), lambda b, i: (b, i, 0)),
        out_shape=jax.ShapeDtypeStruct((bsz, t_len, Q_WIDTH), F32),
        scratch_shapes=[pltpu.VMEM((w, KV_WIDTH), BF16), pltpu.VMEM((KV_WIDTH, w), BF16),
                        pltpu.VMEM((NSA_KV_HEADS, t_len, KV_WIDTH), BF16), pltpu.VMEM((NSA_KV_HEADS, t_len, KV_WIDTH), BF16),
                        pltpu.VMEM((NSA_KV_HEADS, KV_WIDTH, t_len), BF16), pltpu.VMEM((NSA_KV_HEADS, KV_WIDTH, t_len), BF16)],
        compiler_params=pltpu.CompilerParams(dimension_semantics=("arbitrary", "arbitrary"), vmem_limit_bytes=VMEM_LIMIT),
        name="nsa_prompt",
    )(qn.reshape(bsz, t_len, Q_WIDTH), rows_t, win_t, gates.reshape(bsz, t_len, MISC_WIDTH), wc)


def _pad_rows(a, n):
    return jnp.concatenate([a, jnp.zeros((n - a.shape[0], a.shape[1]), a.dtype)], axis=0)


def _nsa_decode_kernel(pt_ref, q_ref, kv_ref, gate_ref, win_ref, wpos_ref, *rest, ts, past_len, page_size, n_pages, w):
    page_refs = rest[:n_pages]
    o_ref, win_out_ref, s_ref, p_ref = rest[n_pages:]
    del pt_ref
    n_cmp = past_len // CMP_BLOCK
    n_sel = past_len // SEL_BLOCK + 1
    n_keys = past_len + page_size
    n_win = win_ref.shape[1]
    rows_q = NSA_GROUP * ts
    stream = lambda s: slice(s * KV_WIDTH, (s + 1) * KV_WIDTH)

    new = kv_ref[...]
    q_tile = q_ref[...]
    gates = gate_ref[...]
    qs = jnp.concatenate([_group_queries(q_tile, kvh) for kvh in range(NSA_KV_HEADS)], axis=0)
    qpos = past_len + lax.broadcasted_iota(jnp.int32, (ts, 1), 0)

    per_pair = 2 * page_size // CMP_BLOCK
    wk = jnp.concatenate([wpos_ref[0]] * per_pair, axis=0).reshape(per_pair, CMP_BLOCK, KV_WIDTH)
    wv = jnp.concatenate([wpos_ref[1]] * per_pair, axis=0).reshape(per_pair, CMP_BLOCK, KV_WIDTH)
    kcs, vcs = [], []
    for j in range(0, n_pages, 2):
        k2 = jnp.concatenate([page_refs[j][stream(0), :].T, page_refs[j + 1][stream(0), :].T], axis=0)
        v2 = jnp.concatenate([page_refs[j][stream(1), :].T, page_refs[j + 1][stream(1), :].T], axis=0)
        kcs.append(jnp.sum(k2.reshape(per_pair, CMP_BLOCK, KV_WIDTH) * wk, axis=1))
        vcs.append(jnp.sum(v2.reshape(per_pair, CMP_BLOCK, KV_WIDTH) * wv, axis=1))
    kc = _pad_rows(jnp.concatenate(kcs, axis=0), w).astype(BF16)
    vc = _pad_rows(jnp.concatenate(vcs, axis=0), w).astype(BF16)

    s_c = _dot_nt(qs, kc)
    p_cs, sels = [], []
    for kvh in range(NSA_KV_HEADS):
        p_c, imp = _compressed_probs(s_c[kvh * rows_q:(kvh + 1) * rows_q], qpos, kvh, ts, n_cmp)
        p_cs.append(p_c)
        sels.append(_select_blocks(imp, qpos, n_sel))
    o_c = _dot(jnp.concatenate(p_cs, axis=0), vc)

    new_k = _pad_rows(new[:, stream(2)], page_size).astype(BF16)
    new_v = _pad_rows(new[:, stream(3)], page_size).astype(BF16)
    for j in range(n_pages):
        s_ref[:, j * page_size:(j + 1) * page_size] = _dot(qs, page_refs[j][stream(2), :].astype(BF16))
    s_ref[:, past_len:n_keys] = _dot_nt(qs, new_k)
    dist_s = qpos - lax.broadcasted_iota(jnp.int32, (1, n_keys), 1)
    blocks_per_page = page_size // SEL_BLOCK
    page_lane_blk = lax.broadcasted_iota(jnp.int32, (1, page_size), 1) >> int(math.log2(SEL_BLOCK))
    invs = []
    for kvh in range(NSA_KV_HEADS):
        sel = sels[kvh]
        pieces = []
        for j in range(n_pages + 1):
            piece = jnp.zeros((ts, page_size), F32)
            for r in range(blocks_per_page):
                blk = j * blocks_per_page + r
                if blk < n_sel:
                    piece = jnp.where(page_lane_blk == r, sel[:, 2 * blk:2 * blk + 1], piece)
            pieces.append(piece)
        keymask = jnp.concatenate(pieces, axis=1)
        nd_s = _masked_neg_dist(dist_s, (keymask > 0.5) & (dist_s >= 0))
        p_s, inv_s = _softmax_rows(s_ref[kvh * rows_q:(kvh + 1) * rows_q, :], nd_s, kvh, ts)
        p_ref[kvh * rows_q:(kvh + 1) * rows_q, :] = p_s
        invs.append(inv_s)
    o_s = _dot(p_ref[:, past_len:n_keys], new_v)
    for j in range(n_pages):
        o_s = o_s + _dot_nt(p_ref[:, j * page_size:(j + 1) * page_size], page_refs[j][stream(3), :].astype(BF16))
    o_s = o_s * jnp.concatenate(invs, axis=0)

    new_kw = _pad_rows(new[:, stream(4)], LANES).astype(BF16)
    new_vw = _pad_rows(new[:, stream(5)], LANES).astype(BF16)
    s_w = jnp.concatenate([_dot(qs, win_ref[0:KV_WIDTH, :].astype(BF16)), _dot_nt(qs, new_kw)], axis=1)
    wpos = jnp.concatenate([past_len - n_win + lax.broadcasted_iota(jnp.int32, (1, n_win), 1),
                            past_len + lax.broadcasted_iota(jnp.int32, (1, LANES), 1)], axis=1)
    dist_w = qpos - wpos
    nd_w = _masked_neg_dist(dist_w, (dist_w >= 0) & (dist_w < WINDOW))
    p_ws, inv_ws = [], []
    for kvh in range(NSA_KV_HEADS):
        p_w, inv_w = _softmax_rows(s_w[kvh * rows_q:(kvh + 1) * rows_q], nd_w, kvh, ts)
        p_ws.append(p_w)
        inv_ws.append(inv_w)
    p_w = jnp.concatenate(p_ws, axis=0)
    o_w = (_dot_nt(p_w[:, 0:n_win], win_ref[KV_WIDTH:2 * KV_WIDTH, :].astype(BF16)) + _dot(p_w[:, n_win:], new_vw))
    o_w = o_w * jnp.concatenate(inv_ws, axis=0)

    per_group = [_gated_sum(gates, kvh, ts, [o[kvh * rows_q:(kvh + 1) * rows_q] for o in (o_c, o_s, o_w)])
                 for kvh in range(NSA_KV_HEADS)]
    _store_slabs(o_ref, per_group)

    new_t = _pad_rows(new[:, 4 * KV_WIDTH:6 * KV_WIDTH], LANES).T
    new_t = pltpu.roll(new_t, LANES - ts, 1)
    shifted = pltpu.roll(win_ref[...], n_win - ts, 1)
    tail_lane = lax.broadcasted_iota(jnp.int32, (1, LANES), 1)
    win_out_ref[:, 0:n_win - LANES] = shifted[:, 0:n_win - LANES]
    win_out_ref[:, n_win - LANES:n_win] = jnp.where(tail_lane >= LANES - ts, new_t, shifted[:, n_win - LANES:n_win])


def _nsa_decode(qn, kvn, gates, win_t, cache_t, page_table, wpos_rows, layer, dbsz, ts):
    n_pages = page_table.shape[1]
    page_size = cache_t.shape[3]
    past_len = n_pages * page_size
    n_win = win_t.shape[3]
    assert ts < CMP_BLOCK and ts % SUBLANES == 0 and page_size == LANES and n_pages % 2 == 0
    assert n_win == WINDOW and past_len >= WINDOW
    n_cmp = past_len // CMP_BLOCK
    n_sel = past_len // SEL_BLOCK + 1
    w = -(-max(n_cmp, 2 * n_sel) // LANES) * LANES
    n_keys = past_len + page_size
    kern = functools.partial(_nsa_decode_kernel, ts=ts, past_len=past_len, page_size=page_size, n_pages=n_pages, w=w)

    def page_spec(j):
        return pl.BlockSpec((None, None, 4 * KV_WIDTH, page_size), lambda b, pt: (layer, pt[b, j], 0, 0))

    grid_spec = pltpu.PrefetchScalarGridSpec(
        num_scalar_prefetch=1,
        grid=(dbsz,),
        in_specs=[
            pl.BlockSpec((None, ts, Q_WIDTH), lambda b, pt: (b, 0, 0)),
            pl.BlockSpec((None, ts, KV6_WIDTH), lambda b, pt: (b, 0, 0)),
            pl.BlockSpec((None, ts, MISC_WIDTH), lambda b, pt: (b, 0, 0)),
            pl.BlockSpec((None, None, 2 * KV_WIDTH, n_win), lambda b, pt: (layer, b, 0, 0)),
            pl.BlockSpec(wpos_rows.shape, lambda b, pt: (0, 0, 0)),
        ] + [page_spec(j) for j in range(n_pages)],
        out_specs=[
            pl.BlockSpec((None, ts, Q_WIDTH), lambda b, pt: (b, 0, 0)),
            pl.BlockSpec((None, 2 * KV_WIDTH, n_win), lambda b, pt: (b, 0, 0)),
        ],
        scratch_shapes=[pltpu.VMEM((NSA_HEADS * ts, n_keys), F32), pltpu.VMEM((NSA_HEADS * ts, n_keys), BF16)],
    )
    return pl.pallas_call(
        kern,
        grid_spec=grid_spec,
        out_shape=[jax.ShapeDtypeStruct((dbsz, ts, Q_WIDTH), F32), jax.ShapeDtypeStruct((dbsz, 2 * KV_WIDTH, n_win), F32)],
        compiler_params=pltpu.CompilerParams(dimension_semantics=("arbitrary",), vmem_limit_bytes=VMEM_LIMIT),
        name="nsa_decode",
    )(page_table, qn.reshape(dbsz, ts, Q_WIDTH), kvn.reshape(dbsz, ts, KV6_WIDTH), gates.reshape(dbsz, ts, MISC_WIDTH),
      win_t, wpos_rows, *([cache_t] * n_pages))


def _gla_kernel(*refs, t_len, chunk, has_s0, nb):
    for i in range(nb):
        _gla_one([r if j == 5 else r.at[i] for j, r in enumerate(refs)], t_len, chunk, has_s0)


def _gla_one(refs, t_len, chunk, has_s0):
    if has_s0:
        q_ref, k_ref, v_ref, gg_ref, la_ref, gn_ref, s0_ref, o_ref, s_out_ref, st_ref, u_ref, d_ref, sb_ref, qe_ref = refs
    else:
        q_ref, k_ref, v_ref, gg_ref, la_ref, gn_ref, o_ref, s_out_ref, st_ref, u_ref, d_ref, sb_ref, qe_ref = refs
    dk2, dv2 = 2 * GLA_DK, 2 * GLA_DV
    n_chunks = t_len // chunk
    per_group = math.gcd(n_chunks, 4)
    rows_g = per_group * chunk
    n_groups = n_chunks // per_group
    rp = max(rows_g, LANES)
    pad = rp - rows_g

    rr = lax.broadcasted_iota(jnp.int32, (dv2, dk2), 0) // GLA_DV
    cc = lax.broadcasted_iota(jnp.int32, (dv2, dk2), 1) // GLA_DK
    diag = rr == cc
    if has_s0:
        z = jnp.zeros((GLA_DK, GLA_DV), F32)
        s_full = jnp.concatenate([jnp.concatenate([s0_ref[0], z], axis=1), jnp.concatenate([z, s0_ref[1]], axis=1)], axis=0)
        st_ref[...] = s_full.T
    else:
        st_ref[...] = jnp.zeros((dv2, dk2), F32)

    lane_head = lax.broadcasted_iota(jnp.int32, (1, dk2), 1) // GLA_DK
    trow = lax.broadcasted_iota(jnp.int32, (rows_g, rp), 0)
    tcol = lax.broadcasted_iota(jnp.int32, (rows_g, rp), 1)
    causal = (tcol <= trow) & (tcol // chunk == trow // chunk)
    tril = jnp.where(causal, 1.0, 0.0).astype(BF16)
    prow_chunk = lax.broadcasted_iota(jnp.int32, (rp, 1), 0) // chunk
    mid = chunk // 2
    gn = gn_ref[...]

    def pad_rows(a):
        return a if pad == 0 else jnp.concatenate([a, jnp.zeros((pad, a.shape[1]), a.dtype)], axis=0)

    def aligned(start, multiple):
        return start if isinstance(start, int) else pl.multiple_of(start, multiple)

    def per_chunk_row(cum, r):
        return jnp.concatenate([jnp.broadcast_to(cum[c * chunk + r:c * chunk + r + 1, :], (chunk, dk2))
                                for c in range(per_group)], axis=0)

    def group_local(gi, carry):
        r0 = aligned(gi * rows_g, rows_g)
        rows = pl.ds(r0, rows_g)
        q = q_ref[rows, :] * (GLA_DK ** -0.5)
        k = k_ref[rows, :]
        la0, la1, la2 = _split3(pad_rows(la_ref[rows, :]))
        cum = _dot(tril, la0) + _dot(tril, la1) + _dot(tril, la2)
        m = per_chunk_row(cum, mid)
        last = per_chunk_row(cum, chunk - 1)
        qe_ref[rows, :] = q * jnp.exp(cum)
        qs = q * jnp.exp(cum - m)
        ks = pad_rows((k * jnp.exp(m - cum)).astype(BF16))
        vp = pad_rows(v_ref[rows, :].astype(BF16))
        intra = []
        for h in range(2):
            a = _dot_nt(jnp.where(lane_head == h, qs, 0.0).astype(BF16), ks)
            a = jnp.where(causal, a, 0.0).astype(BF16)
            intra.append(_dot(a, vp[:, h * GLA_DV:(h + 1) * GLA_DV]))
        o_ref[rows, :] = jnp.concatenate(intra, axis=1)
        kd = pad_rows(k * jnp.exp(last - cum))
        for c in range(per_group):
            ci = gi * per_group + c
            kd_c = jnp.where(prow_chunk == c, kd, 0.0).astype(BF16)
            u_ref[ci] = jnp.where(diag, _dot_tn(vp, kd_c), 0.0)
            d_ref[ci] = jnp.exp(last[c * chunk:c * chunk + SUBLANES, :])
        return carry

    def chunk_state(ci, carry):
        st = st_ref[...]
        sb_ref[ci] = st.astype(BF16)
        st_ref[...] = st * d_ref[ci][0:1, :] + u_ref[ci]
        return carry

    def group_output(gi, carry):
        r0 = aligned(gi * rows_g, rows_g)
        rows = pl.ds(r0, rows_g)
        inter = [_dot_nt(qe_ref[pl.ds(aligned(r0 + c * chunk, chunk), chunk), :].astype(BF16),
                         sb_ref[gi * per_group + c]) for c in range(per_group)]
        o = o_ref[rows, :] + jnp.concatenate(inter, axis=0)
        gg = gg_ref[rows, :]
        outs = []
        for h in range(2):
            oh = o[:, h * GLA_DV:(h + 1) * GLA_DV]
            y = oh * lax.rsqrt(jnp.mean(oh * oh, axis=-1, keepdims=True) + EPS) * gn
            gh = gg[:, h * GLA_DV:(h + 1) * GLA_DV]
            outs.append(y * (gh * _sigmoid(gh)))
        o_ref[rows, :] = jnp.concatenate(outs, axis=1)
        return carry

    def loop(n, body):
        if n == 1:
            body(0, 0)
        else:
            lax.fori_loop(0, n, body, 0)

    loop(n_groups, group_local)
    loop(n_chunks, chunk_state)
    loop(n_groups, group_output)
    s_fin = st_ref[...].T
    s_out_ref[0] = s_fin[0:GLA_DK, 0:GLA_DV]
    s_out_ref[1] = s_fin[GLA_DK:dk2, GLA_DV:dv2]


def _gla(g, la, gnorm, s0, bsz, t_len):
    chunk = math.gcd(t_len, GLA_CHUNK)
    n_chunks = t_len // chunk
    nb = math.gcd(bsz, 8) if n_chunks == 1 else 1
    has_s0 = s0 is not None
    kern = functools.partial(_gla_kernel, t_len=t_len, chunk=chunk, has_s0=has_s0, nb=nb)
    qk_blk = lambda off: pl.BlockSpec((nb, t_len, 2 * GLA_DK), lambda b, p: (b, 0, off + p))
    v_blk = lambda off: pl.BlockSpec((nb, t_len, 2 * GLA_DV), lambda b, p: (b, 0, off + p))
    g3 = g.reshape(bsz, t_len, G_WIDTH)
    n_qk = GLA_QK_WIDTH // (2 * GLA_DK)
    in_specs = [qk_blk(0), qk_blk(n_qk), v_blk(n_qk), v_blk(n_qk + GLA_V_WIDTH // (2 * GLA_DV)),
                pl.BlockSpec((nb, t_len, 2 * GLA_DK), lambda b, p: (b, 0, p)),
                pl.BlockSpec(gnorm.shape, lambda b, p: (0, 0))]
    args = [g3, g3, g3, g3, la.reshape(bsz, t_len, GLA_QK_WIDTH), gnorm]
    if has_s0:
        in_specs.append(pl.BlockSpec((nb, 2, GLA_DK, GLA_DV), lambda b, p: (b, p, 0, 0)))
        args.append(s0)
    return pl.pallas_call(
        kern,
        grid=(bsz // nb, GLA_HEADS // 2),
        in_specs=in_specs,
        out_specs=[pl.BlockSpec((nb, t_len, 2 * GLA_DV), lambda b, p: (b, 0, p)),
                   pl.BlockSpec((nb, 2, GLA_DK, GLA_DV), lambda b, p: (b, p, 0, 0))],
        out_shape=[jax.ShapeDtypeStruct((bsz, t_len, GLA_V_WIDTH), F32),
                   jax.ShapeDtypeStruct((bsz, GLA_HEADS, GLA_DK, GLA_DV), F32)],
        scratch_shapes=[pltpu.VMEM((nb, 2 * GLA_DV, 2 * GLA_DK), F32),
                        pltpu.VMEM((nb, n_chunks, 2 * GLA_DV, 2 * GLA_DK), F32),
                        pltpu.VMEM((nb, n_chunks, SUBLANES, 2 * GLA_DK), F32),
                        pltpu.VMEM((nb, n_chunks, 2 * GLA_DV, 2 * GLA_DK), BF16),
                        pltpu.VMEM((nb, t_len, 2 * GLA_DK), F32)],
        compiler_params=pltpu.CompilerParams(dimension_semantics=("arbitrary", "arbitrary"), vmem_limit_bytes=VMEM_LIMIT),
        name="gla",
    )(*args)


FF_STEPS = 2
FF_CHUNK = D_FF // FF_STEPS
assert FF_CHUNK * FF_STEPS == D_FF and FF_CHUNK % LANES == 0


def _out_ffn_kernel(x_ref, on_ref, og_ref, wo_n_ref, wo_g_ref, ln_ref, wg_ref, wu_ref, wd_ref, y_ref, h_ref):
    j = pl.program_id(1)

    @pl.when(j == 0)
    def _():
        x1 = x_ref[...] + _dot(on_ref[...].astype(BF16), wo_n_ref[...]) + _dot(og_ref[...].astype(BF16), wo_g_ref[...])
        h_ref[...] = (x1 * lax.rsqrt(jnp.mean(x1 * x1, axis=-1, keepdims=True) + EPS) * ln_ref[...]).astype(BF16)
        y_ref[...] = x1

    h = h_ref[...]
    gate = _dot(h, wg_ref[...])
    up = _dot(h, wu_ref[...])
    y_ref[...] += _dot((gate * _sigmoid(gate) * up).astype(BF16), wd_ref[...])


def _out_ffn(x, o_nsa, o_gla, wo_n, wo_g, ln, wg, wu, wd, tm):
    n = x.shape[0]
    row = lambda w: pl.BlockSpec((tm, w), lambda i, j: (i, 0))
    full = lambda a: pl.BlockSpec(a.shape, lambda i, j: (0,) * a.ndim)
    return pl.pallas_call(
        _out_ffn_kernel,
        grid=(n // tm, FF_STEPS),
        in_specs=[row(D_MODEL), row(Q_WIDTH), row(GLA_V_WIDTH), full(wo_n), full(wo_g), full(ln),
                  pl.BlockSpec((D_MODEL, FF_CHUNK), lambda i, j: (0, j)),
                  pl.BlockSpec((D_MODEL, FF_CHUNK), lambda i, j: (0, j)),
                  pl.BlockSpec((FF_CHUNK, D_MODEL), lambda i, j: (j, 0))],
        out_specs=row(D_MODEL),
        out_shape=jax.ShapeDtypeStruct((n, D_MODEL), F32),
        scratch_shapes=[pltpu.VMEM((tm, D_MODEL), BF16)],
        compiler_params=pltpu.CompilerParams(dimension_semantics=("arbitrary", "arbitrary"), vmem_limit_bytes=VMEM_LIMIT),
        name="out_ffn",
    )(x, o_nsa, o_gla, wo_n, wo_g, ln, wg, wu, wd)


def _layer_weights(ln_mix, w_in, q_norm, k_norm, cmp_pos_w, w_a2, b_a, gla_norm, w_out, ln_ffn, w_gate, w_up, w_down,
                   tm_prompt, t_len):
    w_in_t = w_in.T
    o = 0
    parts = []
    for width in (Q_WIDTH, KV6_WIDTH, N_GATES, GLA_QK_WIDTH, GLA_QK_WIDTH, GLA_V_WIDTH, GLA_V_WIDTH, GLA_RANK):
        parts.append(w_in_t[o:o + width])
        o += width
    wq, wkv, wgate, wgq, wgk, wgv, wgg, wga = parts
    wq = jnp.concatenate([wq[h * HEAD_DIM:(h + 1) * HEAD_DIM] for h in SLAB_HEADS], axis=0)
    pad = jnp.zeros((MISC_WIDTH - N_GATES - GLA_RANK, D_MODEL), w_in.dtype)
    wa = jnp.zeros((MISC_WIDTH, GLA_QK_WIDTH), F32).at[N_GATES:N_GATES + GLA_RANK].set(w_a2)
    n_cmp = t_len // CMP_BLOCK
    w_lanes = -(-max(n_cmp, 2 * (-(-t_len // SEL_BLOCK))) // LANES) * LANES
    tok = jnp.arange(t_len)
    lane = jnp.arange(w_lanes)
    blk_of_lane = jnp.where(lane < w_lanes // 2, 2 * lane, 2 * (lane - w_lanes // 2) + 1)
    in_blk = (tok[:, None] // CMP_BLOCK == blk_of_lane[None, :]) & (tok[:, None] < n_cmp * CMP_BLOCK)
    wc = jnp.where(in_blk[None], cmp_pos_w[:, tok % CMP_BLOCK][:, :, None], 0.0)
    kg_rows = jnp.tile(k_norm, (1, NSA_KV_HEADS))
    return dict(
        ln_mix=ln_mix.reshape(1, D_MODEL),
        wq=wq.astype(BF16), wkv=wkv.astype(BF16),
        wg=jnp.concatenate([wgq, wgk, wgv, wgg], axis=0).astype(BF16),
        wm=jnp.concatenate([wgate, wga, pad], axis=0).astype(BF16),
        qg=jnp.tile(q_norm, LANES // HEAD_DIM).reshape(1, LANES),
        kg=kg_rows, kg_t=jnp.broadcast_to(kg_rows[:, :, None], (3, KV_WIDTH, tm_prompt)),
        wa=wa, ba=b_a.reshape(1, GLA_QK_WIDTH),
        wpos=jnp.broadcast_to(cmp_pos_w[:, :, None], (2, CMP_BLOCK, LANES)), wc=wc,
        gnorm=gla_norm.reshape(1, GLA_DV),
        wo_n=jnp.concatenate([w_out[h * HEAD_DIM:(h + 1) * HEAD_DIM] for h in SLAB_HEADS], axis=0).astype(BF16),
        wo_g=w_out[Q_WIDTH:].astype(BF16),
        ln_ffn=ln_ffn.reshape(1, D_MODEL),
        w_gate=w_gate.astype(BF16), w_up=w_up.astype(BF16), w_down=w_down.astype(BF16),
    )


def _row_tile(n):
    return math.gcd(n, 512)


def _mix_and_ffn(x, w, attn, s0, bsz, t_len, kv_bufs=None, layer=None):
    tm = _row_tile(x.shape[0])
    qn, *kv, g, gates, la = _proj_in(x, w, tm, kv_bufs, layer)
    o_nsa, extra = attn(qn, kv, gates)
    o_gla, s_new = _gla(g, la, w["gnorm"], s0, bsz, t_len)
    y = _out_ffn(x, o_nsa.reshape(-1, Q_WIDTH), o_gla.reshape(-1, GLA_V_WIDTH), w["wo_n"], w["wo_g"], w["ln_ffn"],
                 w["w_gate"], w["w_up"], w["w_down"], tm)
    return y, kv, s_new, extra


def _token_major(a_t, lead):
    n_lead = len(lead)
    a = a_t.reshape(*lead, -1, NSA_KV_HEADS, HEAD_DIM, a_t.shape[-1])
    return a.transpose(*range(n_lead), n_lead + 3, n_lead, n_lead + 1, n_lead + 2)


def kernel(x_prompt, x_sample, cache_nsa_kv, state_nsa_win, state_gla, page_table, ln_mix, w_in, q_norm, k_norm, cmp_pos_w,
           w_a2, b_a, gla_norm, w_out, ln_ffn, w_gate, w_up, w_down):
    bsz, t_len = x_prompt.shape[:2]
    dbsz, ts = x_sample.shape[:2]
    depth = w_in.shape[0]
    n_phys, page_size = cache_nsa_kv.shape[1:3]
    n_win = state_nsa_win.shape[2]
    cache_t = cache_nsa_kv.transpose(0, 1, 3, 4, 5, 2).reshape(depth, n_phys, 4 * KV_WIDTH, page_size)
    win_t = state_nsa_win.transpose(0, 1, 3, 4, 5, 2).reshape(depth, dbsz, 2 * KV_WIDTH, n_win)
    keep_p = min(WINDOW, t_len)
    tm_prompt = _row_tile(bsz * t_len)
    assert t_len % tm_prompt == 0

    yp = x_prompt.reshape(bsz * t_len, D_MODEL)
    ys = x_sample.reshape(dbsz * ts, D_MODEL)
    kv_bufs = [jnp.zeros((depth, bsz, 4 * KV_WIDTH, t_len), F32), jnp.zeros((depth, bsz, 2 * KV_WIDTH, t_len), F32)]
    gla_p, rows_s, win_s, gla_s = [], [], [], []
    for l in range(depth):
        w = _layer_weights(ln_mix[l], w_in[l], q_norm[l], k_norm[l], cmp_pos_w[l], w_a2[l], b_a[l], gla_norm[l], w_out[l],
                           ln_ffn[l], w_gate[l], w_up[l], w_down[l], tm_prompt, t_len)

        def attn_prompt(qn, kv, gates):
            return _nsa_prompt(qn, kv[0], kv[1], gates, w["wc"], l, bsz, t_len), None

        def attn_sample(qn, kv, gates):
            return _nsa_decode(qn, kv[0], gates, win_t, cache_t, page_table, w["wpos"], l, dbsz, ts)

        yp, kv_bufs, st_p, _ = _mix_and_ffn(yp, w, attn_prompt, None, bsz, t_len, kv_bufs, l)
        ys, (kvn_s,), st_s, nw_s = _mix_and_ffn(ys, w, attn_sample, state_gla[l], dbsz, ts)

        gla_p.append(st_p.astype(state_gla.dtype))
        rows_s.append(kvn_s.reshape(dbsz, ts, 6, NSA_KV_HEADS, HEAD_DIM)[:, :, :4])
        win_s.append(nw_s)
        gla_s.append(st_s.astype(state_gla.dtype))
    lead_p, lead_s = (depth, bsz), (depth, dbsz)
    return (yp.reshape(bsz, t_len, D_MODEL), ys.reshape(dbsz, ts, D_MODEL),
            _token_major(kv_bufs[0], lead_p), _token_major(kv_bufs[1][:, :, :, t_len - keep_p:], lead_p), jnp.stack(gla_p),
            jnp.stack(rows_s), _token_major(jnp.stack(win_s), lead_s), jnp.stack(gla_s))
```

```python
import functools
import math

import jax
import jax.numpy as jnp
import numpy as np
from jax import lax
from jax.experimental import pallas as pl
from jax.experimental.pallas import tpu as pltpu

F32 = jnp.float32
BF16 = jnp.bfloat16

D_MODEL = 1024
NSA_HEADS = 8
NSA_KV_HEADS = 2
NSA_GROUP = NSA_HEADS // NSA_KV_HEADS
HEAD_DIM = 64
CMP_BLOCK = 32
SEL_BLOCK = 64
TOP_K = 16
WINDOW = 512
GLA_HEADS = 4
GLA_DK = 64
GLA_DV = 128
GLA_RANK = 16
GLA_TAU = 16.0
GLA_CHUNK = 64
D_FF = -(-(8 * D_MODEL) // (3 * 256)) * 256
KV_WIDTH = NSA_KV_HEADS * HEAD_DIM
Q_WIDTH = NSA_HEADS * HEAD_DIM
N_GATES = 3 * NSA_HEADS
GLA_QK_WIDTH = GLA_HEADS * GLA_DK
GLA_V_WIDTH = GLA_HEADS * GLA_DV
EPS = 1e-6
NEG_INF = -1e30
FORCE_SCORE = 1e4
BELOW_ALL = -3e38
SCALE = HEAD_DIM ** -0.5
LOG2E = 1.4426950408889634
SLOPES = tuple(tuple(2.0 ** (-8.0 * (k * NSA_GROUP + g + 1) / NSA_HEADS) for g in range(NSA_GROUP))
               for k in range(NSA_KV_HEADS))
SLAB_HEADS = tuple(h for g in range(NSA_GROUP) for h in (g, NSA_GROUP + g))

LANES = 128
SUBLANES = 8
VMEM_LIMIT = 56 * 1024 * 1024
assert KV_WIDTH == LANES and NSA_KV_HEADS == 2

G_WIDTH = 2 * GLA_QK_WIDTH + 2 * GLA_V_WIDTH
MISC_WIDTH = LANES
KV6_WIDTH = 6 * KV_WIDTH


def _dot(a, b):
    return jnp.dot(a, b, preferred_element_type=F32)


def _dot_nt(a, b):
    return lax.dot_general(a, b, (((1,), (1,)), ((), ())), preferred_element_type=F32)


def _dot_tn(a, b):
    return lax.dot_general(a, b, (((0,), (0,)), ((), ())), preferred_element_type=F32)


def _split3(a):
    a0 = a.astype(BF16)
    r = a - a0.astype(F32)
    a1 = r.astype(BF16)
    a2 = (r - a1.astype(F32)).astype(BF16)
    return a0, a1, a2


def _dot_f32(a, b):
    a0, a1, a2 = _split3(a)
    b0, b1, b2 = _split3(b)
    return (_dot(a0, b0) + (_dot(a0, b1) + _dot(a1, b0)) + (_dot(a0, b2) + _dot(a1, b1) + _dot(a2, b0)))


def _dot_2term(a, b):
    a0 = a.astype(BF16)
    a1 = (a - a0.astype(F32)).astype(BF16)
    b0 = b.astype(BF16)
    b1 = (b - b0.astype(F32)).astype(BF16)
    return _dot(a0, b0) + (_dot(a0, b1) + _dot(a1, b0))


def _sigmoid(x):
    return 1.0 / (1.0 + jnp.exp(-x))


def _low_half(width=LANES):
    return lax.broadcasted_iota(jnp.int32, (1, width), 1) < HEAD_DIM


def _half_lane_rms(x, gain):
    x2 = x * x
    lo = _low_half()
    s_lo = jnp.sum(jnp.where(lo, x2, 0.0), axis=-1, keepdims=True)
    s_hi = jnp.sum(jnp.where(lo, 0.0, x2), axis=-1, keepdims=True)
    ms = jnp.where(lo, s_lo, s_hi) * (1.0 / HEAD_DIM)
    return x * lax.rsqrt(ms + EPS) * gain


def _proj_in_kernel(x_ref, ln_ref, wq_ref, wkv_ref, wg_ref, qg_ref, kg_ref, wa_ref, ba_ref, *rest, kv_transposed):
    if kv_transposed:
        _, _, q_out, rows_out, win_out, g_out, gate_out, la_out = rest
    else:
        q_out, kv_out, g_out, gate_out, la_out = rest
    x = x_ref[...]
    h = x * lax.rsqrt(jnp.mean(x * x, axis=-1, keepdims=True) + EPS) * ln_ref[...]
    hb = h.astype(BF16)

    q = _dot_nt(hb, wq_ref[...])
    for j in range(Q_WIDTH // LANES):
        sl = slice(j * LANES, (j + 1) * LANES)
        q_out[:, sl] = _half_lane_rms(q[:, sl], qg_ref[...])

    if kv_transposed:
        kv = _dot_nt(wkv_ref[...], hb)
        tm = kv.shape[1]
        for s in range(6):
            rows = slice(s * KV_WIDTH, (s + 1) * KV_WIDTH)
            out, s_out = (rows_out, s) if s < 4 else (win_out, s - 4)
            dst = slice(s_out * KV_WIDTH, (s_out + 1) * KV_WIDTH)
            if s % 2 == 0:
                k3 = kv[rows].reshape(NSA_KV_HEADS, HEAD_DIM, tm)
                ms = jnp.mean(k3 * k3, axis=1, keepdims=True)
                out[dst, :] = (k3 * lax.rsqrt(ms + EPS)).reshape(KV_WIDTH, tm) * kg_ref[s // 2]
            else:
                out[dst, :] = kv[rows]
    else:
        kv = _dot_nt(hb, wkv_ref[...])
        for s in range(6):
            sl = slice(s * KV_WIDTH, (s + 1) * KV_WIDTH)
            if s % 2 == 0:
                kv_out[:, sl] = _half_lane_rms(kv[:, sl], kg_ref[s // 2:s // 2 + 1, :])
            else:
                kv_out[:, sl] = kv[:, sl]

    gm = _dot_nt(hb, wg_ref[...])
    g_out[...] = gm[:, 0:G_WIDTH]
    m = gm[:, G_WIDTH:G_WIDTH + MISC_WIDTH]
    gate_out[...] = _sigmoid(m)
    z = _dot_2term(m, wa_ref[...]) + ba_ref[...]
    la_out[...] = (jnp.minimum(z, 0.0) - jnp.log1p(jnp.exp(-jnp.abs(z)))) * (1.0 / GLA_TAU)


def _proj_in(x, w, tm, kv_bufs=None, layer=None):
    n = x.shape[0]
    row = lambda width: pl.BlockSpec((tm, width), lambda i: (i, 0))
    full = lambda a: pl.BlockSpec(a.shape, lambda i: (0,) * a.ndim)
    args = [x, w["ln_mix"], w["wq"], w["wkv"], w["wg"], w["qg"], w["kg"] if kv_bufs is None else w["kg_t"],
            w["wa"], w["ba"]]
    in_specs = [row(D_MODEL)] + [full(a) for a in args[1:]]
    tail_specs = [row(G_WIDTH), row(MISC_WIDTH), row(GLA_QK_WIDTH)]
    tail_shapes = [jax.ShapeDtypeStruct((n, width), F32) for width in (G_WIDTH, MISC_WIDTH, GLA_QK_WIDTH)]
    q_shape = jax.ShapeDtypeStruct((n, Q_WIDTH), F32)
    if kv_bufs is None:
        kv_specs = [row(KV6_WIDTH)]
        kv_shapes = [jax.ShapeDtypeStruct((n, KV6_WIDTH), F32)]
        aliases = {}
    else:
        tiles = kv_bufs[0].shape[3] // tm
        kv_specs = [pl.BlockSpec((None, None, b.shape[2], tm), lambda i: (layer, i // tiles, 0, i % tiles)) for b in kv_bufs]
        kv_shapes = [jax.ShapeDtypeStruct(b.shape, b.dtype) for b in kv_bufs]
        aliases = {len(args): 1, len(args) + 1: 2}
        in_specs += [pl.BlockSpec(memory_space=pl.ANY)] * 2
        args += list(kv_bufs)
    return pl.pallas_call(
        functools.partial(_proj_in_kernel, kv_transposed=kv_bufs is not None),
        grid=(n // tm,),
        in_specs=in_specs,
        out_specs=[row(Q_WIDTH)] + kv_specs + tail_specs,
        out_shape=[q_shape] + kv_shapes + tail_shapes,
        input_output_aliases=aliases,
        compiler_params=pltpu.CompilerParams(dimension_semantics=("arbitrary",), vmem_limit_bytes=VMEM_LIMIT),
        name="proj_in",
    )(*args)


def _group_queries(q_tile, kvh):
    keep = _low_half() if kvh == 0 else ~_low_half()
    return jnp.concatenate([jnp.where(keep, q_tile[:, g * LANES:(g + 1) * LANES] * (SCALE * LOG2E), 0.0)
                            for g in range(NSA_GROUP)], axis=0).astype(BF16)


def _masked_neg_dist(dist, ok):
    return jnp.where(ok, -dist.astype(F32), NEG_INF)


def _softmax_rows(s_rows, neg_dist, kvh, tq):
    ps, inv = [], []
    for g in range(NSA_GROUP):
        s = s_rows[g * tq:(g + 1) * tq] + (SLOPES[kvh][g] * LOG2E) * neg_dist
        p = jnp.exp2(s - jnp.max(s, axis=-1, keepdims=True))
        inv.append(1.0 / jnp.sum(p, axis=-1, keepdims=True))
        ps.append(p)
    return jnp.concatenate(ps, axis=0).astype(BF16), jnp.concatenate(inv, axis=0)


N_SEL_SLOTS = 32
ALIBI_SLOT0 = N_SEL_SLOTS
MASK_BIG = 2.0 ** 100


def _bf16_terms(c):
    out = []
    for _ in range(3):
        t = float(np.float32(c).astype(BF16).astype(np.float32))
        out.append(t)
        c = c - t
    return out


ALIBI_TERMS = tuple(tuple(_bf16_terms(s * LOG2E) for s in row) for row in SLOPES)


def _slot_base(kvh):
    return HEAD_DIM if kvh == 0 else 0


def _augmented_keys(k_t, kvh, with_selection):
    n = k_t.shape[1]
    row = lax.broadcasted_iota(jnp.int32, (KV_WIDTH, 1), 0)
    kpos = lax.broadcasted_iota(jnp.int32, (1, n), 1)
    e = row - _slot_base(kvh)
    mine = (e < 0) | (e >= HEAD_DIM)
    k_hi = ((kpos >> 7) << 7).astype(F32)
    k_lo = (kpos & (LANES - 1)).astype(F32)
    extra = jnp.where((e >= ALIBI_SLOT0) & (e < ALIBI_SLOT0 + 3), k_hi,
                      jnp.where((e >= ALIBI_SLOT0 + 3) & (e < ALIBI_SLOT0 + 6), k_lo, 0.0))
    if with_selection:
        extra = jnp.where((e >= 0) & (e < N_SEL_SLOTS) & ((kpos >> int(math.log2(SEL_BLOCK))) == e), 1.0, extra)
    return jnp.where(mine, k_t, extra)


def _queries_t(slabs_t, kvh, alibi, sel_t):
    tq = slabs_t[0].shape[1]
    base = _slot_base(kvh)
    e = lax.broadcasted_iota(jnp.int32, (KV_WIDTH, 1), 0) - base
    own = (e < 0) | (e >= HEAD_DIM)
    if sel_t is None:
        sel_rows = None
    else:
        mask = (sel_t - 1.0) * MASK_BIG
        pieces = [jnp.zeros((base, tq), F32)] if base else []
        pieces += [mask, jnp.zeros((KV_WIDTH - base - mask.shape[0], tq), F32)]
        sel_rows = jnp.concatenate(pieces, axis=0)
    cols = []
    for g in range(NSA_GROUP):
        extra = jnp.zeros((KV_WIDTH, 1), F32)
        if alibi:
            for i, term in enumerate(ALIBI_TERMS[kvh][g] * 2):
                extra = jnp.where(e == ALIBI_SLOT0 + i, term, extra)
        if sel_rows is not None:
            extra = extra + sel_rows
        cols.append(jnp.where(own, slabs_t[g], extra))
    return jnp.concatenate(cols, axis=1).astype(BF16)


def _augmented_values(v_t, kvh):
    mine = (lax.broadcasted_iota(jnp.int32, (KV_WIDTH, 1), 0) < HEAD_DIM) == (kvh == 0)
    return jnp.where(mine, v_t, 1.0).astype(BF16)


def _attend_group_t(s_t, bias_t, v_aug, kvh, tq):
    ps = []
    for g in range(NSA_GROUP):
        s = s_t[:, g * tq:(g + 1) * tq] + bias_t
        ps.append(jnp.exp2((s - jnp.max(s, axis=0, keepdims=True)).astype(BF16)))
    o_t = _dot(v_aug, jnp.concatenate(ps, axis=1))
    denom_row = HEAD_DIM if kvh == 0 else 0
    return o_t, 1.0 / o_t[denom_row:denom_row + 1, :]


def _compressed_probs_t(s_t, qpos_row, kvh, tq, n_cmp, blk_of_row):
    dist = qpos_row - ((blk_of_row + 1) * CMP_BLOCK - 1)
    valid = (dist >= 0) & (blk_of_row < n_cmp)
    nd = _masked_neg_dist(dist, valid)
    ps = []
    imp = jnp.zeros(dist.shape, F32)
    for g in range(NSA_GROUP):
        s = s_t[:, g * tq:(g + 1) * tq] + (SLOPES[kvh][g] * LOG2E) * nd
        e = jnp.exp2(s - jnp.max(s, axis=0, keepdims=True))
        p = jnp.where(valid, e / jnp.sum(e, axis=0, keepdims=True), 0.0)
        imp = imp + p
        ps.append(p)
    return jnp.concatenate(ps, axis=1).astype(BF16), imp


def _split_order_block(w):
    slot = lax.broadcasted_iota(jnp.int32, (w, 1), 0)
    return jnp.where(slot < w // 2, 2 * slot, 2 * (slot - w // 2) + 1)


def _compressed_probs(s_rows, qpos, kvh, tq, n_cmp, blk_of_lane=None):
    w = s_rows.shape[1]
    lane = lax.broadcasted_iota(jnp.int32, (1, w), 1) if blk_of_lane is None else blk_of_lane
    dist = qpos - ((lane + 1) * CMP_BLOCK - 1)
    valid = (dist >= 0) & (lane < n_cmp)
    nd = _masked_neg_dist(dist, valid)
    ps = []
    imp = jnp.zeros((tq, w), F32)
    for g in range(NSA_GROUP):
        s = s_rows[g * tq:(g + 1) * tq] + (SLOPES[kvh][g] * LOG2E) * nd
        e = jnp.exp2(s - jnp.max(s, axis=-1, keepdims=True))
        p = jnp.where(valid, e / jnp.sum(e, axis=-1, keepdims=True), 0.0)
        imp = imp + p
        ps.append(p)
    return jnp.concatenate(ps, axis=0).astype(BF16), imp


def _select_blocks(imp, qpos, n_sel):
    tq, w = imp.shape
    lane = lax.broadcasted_iota(jnp.int32, (1, w), 1)
    pair = imp + pltpu.roll(imp, w - 1, 1)
    blk = lane >> 1
    cur = qpos >> int(math.log2(SEL_BLOCK))
    is_blk = ((lane & 1) == 0) & (blk < n_sel)
    forced = (blk == 0) | (blk == cur) | (blk == cur - 1)
    score = jnp.where(forced, FORCE_SCORE, jnp.where(blk <= cur, pair, NEG_INF))
    score = jnp.where(is_blk, score, BELOW_ALL)
    rank = jnp.zeros((tq, w), jnp.int32)
    for i in range(n_sel):
        col = score[:, 2 * i:2 * i + 1]
        beats = (col > score) | ((col == score) & (lane > 2 * i))
        rank = rank + jnp.where(beats, 1, 0)
    return jnp.where((rank < min(TOP_K, n_sel)) & is_blk, 1.0, 0.0)


def _select_blocks_t(imp_t, qpos_row, n_sel):
    w, tq = imp_t.shape
    nb = -(-n_sel // SUBLANES) * SUBLANES
    pair_t = imp_t[0:nb] + imp_t[w // 2:w // 2 + nb]
    blk = lax.broadcasted_iota(jnp.int32, (nb, 1), 0)
    cur = qpos_row >> int(math.log2(SEL_BLOCK))
    forced = (blk == 0) | (blk == cur) | (blk == cur - 1)
    score = jnp.where(forced, FORCE_SCORE, jnp.where(blk <= cur, pair_t, NEG_INF))
    score = jnp.where(blk < n_sel, score, BELOW_ALL)
    rank = jnp.zeros((nb, tq), jnp.int32)
    for i in range(n_sel):
        row = score[i:i + 1, :]
        beats = (row > score) | ((row == score) & (blk > i))
        rank = rank + jnp.where(beats, 1, 0)
    return jnp.where((rank < min(TOP_K, n_sel)) & (blk < n_sel), 1.0, 0.0)


def _gated_sum(gates, kvh, tq, branches):
    out = []
    for g in range(NSA_GROUP):
        h = kvh * NSA_GROUP + g
        rows = slice(g * tq, (g + 1) * tq)
        out.append(sum(gates[:, c * NSA_HEADS + h:c * NSA_HEADS + h + 1] * o[rows] for c, o in enumerate(branches)))
    return out


def _gated_sum_t(gates_t, kvh, branches):
    total = None
    for c, (o_t, scale) in enumerate(branches):
        first = c * NSA_HEADS + kvh * NSA_GROUP
        row = jnp.concatenate([gates_t[first + g:first + g + 1, :] for g in range(NSA_GROUP)], axis=1)
        term = o_t * (row if scale is None else row * scale)
        total = term if total is None else total + term
    return total


def _store_slabs(o_ref, per_group):
    lo = _low_half()
    for g in range(NSA_GROUP):
        o_ref[:, g * LANES:(g + 1) * LANES] = jnp.where(lo, per_group[0][g], per_group[1][g])


def _nsa_prompt_kernel(q_ref, kv_ref, kvw_ref, gate_ref, wc_ref, o_ref, kc_ref, vc_ref, ks_ref, kw_ref, vs_ref, vw_ref,
                       *, t_len, tq, key_step):
    qi = pl.program_id(1)
    n_cmp = t_len // CMP_BLOCK
    n_sel = -(-t_len // SEL_BLOCK)
    w = kc_ref.shape[0]
    stream = lambda s: slice(s * KV_WIDTH, (s + 1) * KV_WIDTH)

    @pl.when(qi == 0)
    def _():
        kc_ref[...] = _dot_f32(kv_ref[stream(0), :], wc_ref[0]).T.astype(BF16)
        vc_ref[...] = _dot_f32(kv_ref[stream(1), :], wc_ref[1]).astype(BF16)
        for kvh in range(NSA_KV_HEADS):
            ks_ref[kvh] = _augmented_keys(kv_ref[stream(2), :], kvh, True).T.astype(BF16)
            kw_ref[kvh] = _augmented_keys(kvw_ref[stream(0), :], kvh, False).T.astype(BF16)
            vs_ref[kvh] = _augmented_values(kv_ref[stream(3), :], kvh)
            vw_ref[kvh] = _augmented_values(kvw_ref[stream(1), :], kvh)

    q0 = qi * tq
    qpos_row = q0 + lax.broadcasted_iota(jnp.int32, (1, tq), 1)
    blk_of_row = _split_order_block(w)
    n_win = min(WINDOW + tq, t_len)
    w_start = pl.multiple_of(jnp.maximum(q0 + tq - n_win, 0), LANES)

    def body(n_keys):
        q_tile = q_ref[...]
        gates_t = gate_ref[...].T
        slabs_t = [(q_tile[:, g * LANES:(g + 1) * LANES] * (SCALE * LOG2E)).T for g in range(NSA_GROUP)]
        dist_s = qpos_row - lax.broadcasted_iota(jnp.int32, (n_keys, 1), 0)
        dist_w = qpos_row - (w_start + lax.broadcasted_iota(jnp.int32, (n_win, 1), 0))
        bias_s = jnp.where(dist_s >= 0, 0.0, NEG_INF)
        bias_w = jnp.where((dist_w >= 0) & (dist_w < WINDOW), 0.0, NEG_INF)
        groups = range(NSA_KV_HEADS)
        s_w = [_dot(kw_ref[kvh, pl.ds(w_start, n_win), :], _queries_t(slabs_t, kvh, True, None)) for kvh in groups]
        o_c, s_s = [], []
        for kvh in groups:
            s_c = _dot(kc_ref[...], _queries_t(slabs_t, kvh, False, None))
            p_c, imp_t = _compressed_probs_t(s_c, qpos_row, kvh, tq, n_cmp, blk_of_row)
            o_c.append(_dot(vc_ref[...], p_c))
            sel_t = _select_blocks_t(imp_t, qpos_row, n_sel)
            s_s.append(_dot(ks_ref[kvh, 0:n_keys, :], _queries_t(slabs_t, kvh, True, sel_t)))
        totals = []
        for kvh in groups:
            o_w, inv_w = _attend_group_t(s_w[kvh], bias_w, vw_ref[kvh, :, pl.ds(w_start, n_win)], kvh, tq)
            o_s, inv_s = _attend_group_t(s_s[kvh], bias_s, vs_ref[kvh, :, 0:n_keys], kvh, tq)
            totals.append(_gated_sum_t(gates_t, kvh, [(o_c[kvh], None), (o_s, inv_s), (o_w, inv_w)]))
        low_rows = lax.broadcasted_iota(jnp.int32, (KV_WIDTH, 1), 0) < HEAD_DIM
        for g in range(NSA_GROUP):
            cols = slice(g * tq, (g + 1) * tq)
            o_ref[:, g * LANES:(g + 1) * LANES] = jnp.where(low_rows, totals[0][:, cols], totals[1][:, cols]).T

    n_classes = -(-t_len // key_step)
    for c in range(n_classes):
        n_keys = min((c + 1) * key_step, t_len)

        @pl.when((q0 + tq - 1) // key_step == c)
        def _(n_keys=n_keys):
            body(n_keys)


def _nsa_prompt(qn, rows_t, win_t, gates, wc, layer, bsz, t_len):
    tq = math.gcd(t_len, 128)
    key_step = math.gcd(t_len, 256)
    w = wc.shape[2]
    assert tq == LANES and w == LANES
    assert -(-t_len // SEL_BLOCK) <= N_SEL_SLOTS
    kern = functools.partial(_nsa_prompt_kernel, t_len=t_len, tq=tq, key_step=key_step)
    return pl.pallas_call(
        kern,
        grid=(bsz, t_len // tq),
        in_specs=[
            pl.BlockSpec((None, tq, Q_WIDTH), lambda b, i: (b, i, 0)),
            pl.BlockSpec((None, None, 4 * KV_WIDTH, t_len), lambda b, i: (layer, b, 0, 0)),
            pl.BlockSpec((None, None, 2 * KV_WIDTH, t_len), lambda b, i: (layer, b, 0, 0)),
            pl.BlockSpec((None, tq, MISC_WIDTH), lambda b, i: (b, i, 0)),
            pl.BlockSpec(wc.shape, lambda b, i: (0, 0, 0)),
        ],
        out_specs=pl.BlockSpec((None, tq, Q_WIDTH), lambda b, i: (b, i, 0)),
        out_shape=jax.ShapeDtypeStruct((bsz, t_len, Q_WIDTH), F32),
        scratch_shapes=[pltpu.VMEM((w, KV_WIDTH), BF16), pltpu.VMEM((KV_WIDTH, w), BF16),
                        pltpu.VMEM((NSA_KV_HEADS, t_len, KV_WIDTH), BF16), pltpu.VMEM((NSA_KV_HEADS, t_len, KV_WIDTH), BF16),
                        pltpu.VMEM((NSA_KV_HEADS, KV_WIDTH, t_len), BF16), pltpu.VMEM((NSA_KV_HEADS, KV_WIDTH, t_len), BF16)],
        compiler_params=pltpu.CompilerParams(dimension_semantics=("arbitrary", "arbitrary"), vmem_limit_bytes=VMEM_LIMIT),
        name="nsa_prompt",
    )(qn.reshape(bsz, t_len, Q_WIDTH), rows_t, win_t, gates.reshape(bsz, t_len, MISC_WIDTH), wc)


def _pad_rows(a, n):
    return jnp.concatenate([a, jnp.zeros((n - a.shape[0], a.shape[1]), a.dtype)], axis=0)


def _nsa_decode_kernel(pt_ref, q_ref, kv_ref, gate_ref, win_ref, wpos_ref, *rest, ts, past_len, page_size, n_pages, w):
    page_refs = rest[:n_pages]
    o_ref, win_out_ref, s_ref, p_ref = rest[n_pages:]
    del pt_ref
    n_cmp = past_len // CMP_BLOCK
    n_sel = past_len // SEL_BLOCK + 1
    n_keys = past_len + page_size
    n_win = win_ref.shape[1]
    rows_q = NSA_GROUP * ts
    stream = lambda s: slice(s * KV_WIDTH, (s + 1) * KV_WIDTH)

    new = kv_ref[...]
    q_tile = q_ref[...]
    gates = gate_ref[...]
    qs = jnp.concatenate([_group_queries(q_tile, kvh) for kvh in range(NSA_KV_HEADS)], axis=0)
    qpos = past_len + lax.broadcasted_iota(jnp.int32, (ts, 1), 0)

    per_pair = 2 * page_size // CMP_BLOCK
    wk = jnp.concatenate([wpos_ref[0]] * per_pair, axis=0).reshape(per_pair, CMP_BLOCK, KV_WIDTH)
    wv = jnp.concatenate([wpos_ref[1]] * per_pair, axis=0).reshape(per_pair, CMP_BLOCK, KV_WIDTH)
    kcs, vcs = [], []
    for j in range(0, n_pages, 2):
        k2 = jnp.concatenate([page_refs[j][stream(0), :].T, page_refs[j + 1][stream(0), :].T], axis=0)
        v2 = jnp.concatenate([page_refs[j][stream(1), :].T, page_refs[j + 1][stream(1), :].T], axis=0)
        kcs.append(jnp.sum(k2.reshape(per_pair, CMP_BLOCK, KV_WIDTH) * wk, axis=1))
        vcs.append(jnp.sum(v2.reshape(per_pair, CMP_BLOCK, KV_WIDTH) * wv, axis=1))
    kc = _pad_rows(jnp.concatenate(kcs, axis=0), w).astype(BF16)
    vc = _pad_rows(jnp.concatenate(vcs, axis=0), w).astype(BF16)

    s_c = _dot_nt(qs, kc)
    p_cs, sels = [], []
    for kvh in range(NSA_KV_HEADS):
        p_c, imp = _compressed_probs(s_c[kvh * rows_q:(kvh + 1) * rows_q], qpos, kvh, ts, n_cmp)
        p_cs.append(p_c)
        sels.append(_select_blocks(imp, qpos, n_sel))
    o_c = _dot(jnp.concatenate(p_cs, axis=0), vc)

    new_k = _pad_rows(new[:, stream(2)], page_size).astype(BF16)
    new_v = _pad_rows(new[:, stream(3)], page_size).astype(BF16)
    for j in range(n_pages):
        s_ref[:, j * page_size:(j + 1) * page_size] = _dot(qs, page_refs[j][stream(2), :].astype(BF16))
    s_ref[:, past_len:n_keys] = _dot_nt(qs, new_k)
    dist_s = qpos - lax.broadcasted_iota(jnp.int32, (1, n_keys), 1)
    blocks_per_page = page_size // SEL_BLOCK
    page_lane_blk = lax.broadcasted_iota(jnp.int32, (1, page_size), 1) >> int(math.log2(SEL_BLOCK))
    invs = []
    for kvh in range(NSA_KV_HEADS):
        sel = sels[kvh]
        pieces = []
        for j in range(n_pages + 1):
            piece = jnp.zeros((ts, page_size), F32)
            for r in range(blocks_per_page):
                blk = j * blocks_per_page + r
                if blk < n_sel:
                    piece = jnp.where(page_lane_blk == r, sel[:, 2 * blk:2 * blk + 1], piece)
            pieces.append(piece)
        keymask = jnp.concatenate(pieces, axis=1)
        nd_s = _masked_neg_dist(dist_s, (keymask > 0.5) & (dist_s >= 0))
        p_s, inv_s = _softmax_rows(s_ref[kvh * rows_q:(kvh + 1) * rows_q, :], nd_s, kvh, ts)
        p_ref[kvh * rows_q:(kvh + 1) * rows_q, :] = p_s
        invs.append(inv_s)
    o_s = _dot(p_ref[:, past_len:n_keys], new_v)
    for j in range(n_pages):
        o_s = o_s + _dot_nt(p_ref[:, j * page_size:(j + 1) * page_size], page_refs[j][stream(3), :].astype(BF16))
    o_s = o_s * jnp.concatenate(invs, axis=0)

    new_kw = _pad_rows(new[:, stream(4)], LANES).astype(BF16)
    new_vw = _pad_rows(new[:, stream(5)], LANES).astype(BF16)
    s_w = jnp.concatenate([_dot(qs, win_ref[0:KV_WIDTH, :].astype(BF16)), _dot_nt(qs, new_kw)], axis=1)
    wpos = jnp.concatenate([past_len - n_win + lax.broadcasted_iota(jnp.int32, (1, n_win), 1),
                            past_len + lax.broadcasted_iota(jnp.int32, (1, LANES), 1)], axis=1)
    dist_w = qpos - wpos
    nd_w = _masked_neg_dist(dist_w, (dist_w >= 0) & (dist_w < WINDOW))
    p_ws, inv_ws = [], []
    for kvh in range(NSA_KV_HEADS):
        p_w, inv_w = _softmax_rows(s_w[kvh * rows_q:(kvh + 1) * rows_q], nd_w, kvh, ts)
        p_ws.append(p_w)
        inv_ws.append(inv_w)
    p_w = jnp.concatenate(p_ws, axis=0)
    o_w = (_dot_nt(p_w[:, 0:n_win], win_ref[KV_WIDTH:2 * KV_WIDTH, :].astype(BF16)) + _dot(p_w[:, n_win:], new_vw))
    o_w = o_w * jnp.concatenate(inv_ws, axis=0)

    per_group = [_gated_sum(gates, kvh, ts, [o[kvh * rows_q:(kvh + 1) * rows_q] for o in (o_c, o_s, o_w)])
                 for kvh in range(NSA_KV_HEADS)]
    _store_slabs(o_ref, per_group)

    new_t = _pad_rows(new[:, 4 * KV_WIDTH:6 * KV_WIDTH], LANES).T
    new_t = pltpu.roll(new_t, LANES - ts, 1)
    shifted = pltpu.roll(win_ref[...], n_win - ts, 1)
    tail_lane = lax.broadcasted_iota(jnp.int32, (1, LANES), 1)
    win_out_ref[:, 0:n_win - LANES] = shifted[:, 0:n_win - LANES]
    win_out_ref[:, n_win - LANES:n_win] = jnp.where(tail_lane >= LANES - ts, new_t, shifted[:, n_win - LANES:n_win])


def _nsa_decode(qn, kvn, gates, win_t, cache_t, page_table, wpos_rows, layer, dbsz, ts):
    n_pages = page_table.shape[1]
    page_size = cache_t.shape[3]
    past_len = n_pages * page_size
    n_win = win_t.shape[3]
    assert ts < CMP_BLOCK and ts % SUBLANES == 0 and page_size == LANES and n_pages % 2 == 0
    assert n_win == WINDOW and past_len >= WINDOW
    n_cmp = past_len // CMP_BLOCK
    n_sel = past_len // SEL_BLOCK + 1
    w = -(-max(n_cmp, 2 * n_sel) // LANES) * LANES
    n_keys = past_len + page_size
    kern = functools.partial(_nsa_decode_kernel, ts=ts, past_len=past_len, page_size=page_size, n_pages=n_pages, w=w)

    def page_spec(j):
        return pl.BlockSpec((None, None, 4 * KV_WIDTH, page_size), lambda b, pt: (layer, pt[b, j], 0, 0))

    grid_spec = pltpu.PrefetchScalarGridSpec(
        num_scalar_prefetch=1,
        grid=(dbsz,),
        in_specs=[
            pl.BlockSpec((None, ts, Q_WIDTH), lambda b, pt: (b, 0, 0)),
            pl.BlockSpec((None, ts, KV6_WIDTH), lambda b, pt: (b, 0, 0)),
            pl.BlockSpec((None, ts, MISC_WIDTH), lambda b, pt: (b, 0, 0)),
            pl.BlockSpec((None, None, 2 * KV_WIDTH, n_win), lambda b, pt: (layer, b, 0, 0)),
            pl.BlockSpec(wpos_rows.shape, lambda b, pt: (0, 0, 0)),
        ] + [page_spec(j) for j in range(n_pages)],
        out_specs=[
            pl.BlockSpec((None, ts, Q_WIDTH), lambda b, pt: (b, 0, 0)),
            pl.BlockSpec((None, 2 * KV_WIDTH, n_win), lambda b, pt: (b, 0, 0)),
        ],
        scratch_shapes=[pltpu.VMEM((NSA_HEADS * ts, n_keys), F32), pltpu.VMEM((NSA_HEADS * ts, n_keys), BF16)],
    )
    return pl.pallas_call(
        kern,
        grid_spec=grid_spec,
        out_shape=[jax.ShapeDtypeStruct((dbsz, ts, Q_WIDTH), F32), jax.ShapeDtypeStruct((dbsz, 2 * KV_WIDTH, n_win), F32)],
        compiler_params=pltpu.CompilerParams(dimension_semantics=("arbitrary",), vmem_limit_bytes=VMEM_LIMIT),
        name="nsa_decode",
    )(page_table, qn.reshape(dbsz, ts, Q_WIDTH), kvn.reshape(dbsz, ts, KV6_WIDTH), gates.reshape(dbsz, ts, MISC_WIDTH),
      win_t, wpos_rows, *([cache_t] * n_pages))


def _gla_kernel(*refs, t_len, chunk, has_s0, nb):
    for i in range(nb):
        _gla_one([r if j == 5 else r.at[i] for j, r in enumerate(refs)], t_len, chunk, has_s0)


def _gla_one(refs, t_len, chunk, has_s0):
    if has_s0:
        q_ref, k_ref, v_ref, gg_ref, la_ref, gn_ref, s0_ref, o_ref, s_out_ref, st_ref, u_ref, d_ref, sb_ref, qe_ref = refs
    else:
        q_ref, k_ref, v_ref, gg_ref, la_ref, gn_ref, o_ref, s_out_ref, st_ref, u_ref, d_ref, sb_ref, qe_ref = refs
    dk2, dv2 = 2 * GLA_DK, 2 * GLA_DV
    n_chunks = t_len // chunk
    per_group = math.gcd(n_chunks, 4)
    rows_g = per_group * chunk
    n_groups = n_chunks // per_group
    rp = max(rows_g, LANES)
    pad = rp - rows_g

    rr = lax.broadcasted_iota(jnp.int32, (dv2, dk2), 0) // GLA_DV
    cc = lax.broadcasted_iota(jnp.int32, (dv2, dk2), 1) // GLA_DK
    diag = rr == cc
    if has_s0:
        z = jnp.zeros((GLA_DK, GLA_DV), F32)
        s_full = jnp.concatenate([jnp.concatenate([s0_ref[0], z], axis=1), jnp.concatenate([z, s0_ref[1]], axis=1)], axis=0)
        st_ref[...] = s_full.T
    else:
        st_ref[...] = jnp.zeros((dv2, dk2), F32)

    lane_head = lax.broadcasted_iota(jnp.int32, (1, dk2), 1) // GLA_DK
    trow = lax.broadcasted_iota(jnp.int32, (rows_g, rp), 0)
    tcol = lax.broadcasted_iota(jnp.int32, (rows_g, rp), 1)
    causal = (tcol <= trow) & (tcol // chunk == trow // chunk)
    tril = jnp.where(causal, 1.0, 0.0).astype(BF16)
    prow_chunk = lax.broadcasted_iota(jnp.int32, (rp, 1), 0) // chunk
    mid = chunk // 2
    gn = gn_ref[...]

    def pad_rows(a):
        return a if pad == 0 else jnp.concatenate([a, jnp.zeros((pad, a.shape[1]), a.dtype)], axis=0)

    def aligned(start, multiple):
        return start if isinstance(start, int) else pl.multiple_of(start, multiple)

    def per_chunk_row(cum, r):
        return jnp.concatenate([jnp.broadcast_to(cum[c * chunk + r:c * chunk + r + 1, :], (chunk, dk2))
                                for c in range(per_group)], axis=0)

    def group_local(gi, carry):
        r0 = aligned(gi * rows_g, rows_g)
        rows = pl.ds(r0, rows_g)
        q = q_ref[rows, :] * (GLA_DK ** -0.5)
        k = k_ref[rows, :]
        la0, la1, _ = _split3(pad_rows(la_ref[rows, :]))
        cum = _dot(tril, la0) + _dot(tril, la1)
        m = per_chunk_row(cum, mid)
        last = per_chunk_row(cum, chunk - 1)
        qe_ref[rows, :] = q * jnp.exp(cum)
        qs = q * jnp.exp(cum - m)
        ks = pad_rows((k * jnp.exp(m - cum)).astype(BF16))
        vp = pad_rows(v_ref[rows, :].astype(BF16))
        intra = []
        for h in range(2):
            a = _dot_nt(jnp.where(lane_head == h, qs, 0.0).astype(BF16), ks)
            a = jnp.where(causal, a, 0.0).astype(BF16)
            intra.append(_dot(a, vp[:, h * GLA_DV:(h + 1) * GLA_DV]))
        o_ref[rows, :] = jnp.concatenate(intra, axis=1)
        kd = pad_rows(k * jnp.exp(last - cum))
        for c in range(per_group):
            ci = gi * per_group + c
            kd_c = jnp.where(prow_chunk == c, kd, 0.0).astype(BF16)
            u_ref[ci] = jnp.where(diag, _dot_tn(vp, kd_c), 0.0)
            d_ref[ci] = jnp.exp(last[c * chunk:c * chunk + SUBLANES, :])
        return carry

    def chunk_state(ci, carry):
        st = st_ref[...]
        sb_ref[ci] = st.astype(BF16)
        st_ref[...] = st * d_ref[ci][0:1, :] + u_ref[ci]
        return carry

    def group_output(gi, carry):
        r0 = aligned(gi * rows_g, rows_g)
        rows = pl.ds(r0, rows_g)
        inter = [_dot_nt(qe_ref[pl.ds(aligned(r0 + c * chunk, chunk), chunk), :].astype(BF16),
                         sb_ref[gi * per_group + c]) for c in range(per_group)]
        o = o_ref[rows, :] + jnp.concatenate(inter, axis=0)
        gg = gg_ref[rows, :]
        outs = []
        for h in range(2):
            oh = o[:, h * GLA_DV:(h + 1) * GLA_DV]
            y = oh * lax.rsqrt(jnp.mean(oh * oh, axis=-1, keepdims=True) + EPS) * gn
            gh = gg[:, h * GLA_DV:(h + 1) * GLA_DV]
            outs.append(y * (gh * _sigmoid(gh)))
        o_ref[rows, :] = jnp.concatenate(outs, axis=1)
        return carry

    def loop(n, body):
        if n == 1:
            body(0, 0)
        else:
            lax.fori_loop(0, n, body, 0)

    loop(n_groups, group_local)
    loop(n_chunks, chunk_state)
    loop(n_groups, group_output)
    s_fin = st_ref[...].T
    s_out_ref[0] = s_fin[0:GLA_DK, 0:GLA_DV]
    s_out_ref[1] = s_fin[GLA_DK:dk2, GLA_DV:dv2]


def _gla(g, la, gnorm, s0, bsz, t_len):
    chunk = math.gcd(t_len, GLA_CHUNK)
    n_chunks = t_len // chunk
    nb = math.gcd(bsz, 8) if n_chunks == 1 else 1
    has_s0 = s0 is not None
    kern = functools.partial(_gla_kernel, t_len=t_len, chunk=chunk, has_s0=has_s0, nb=nb)
    qk_blk = lambda off: pl.BlockSpec((nb, t_len, 2 * GLA_DK), lambda b, p: (b, 0, off + p))
    v_blk = lambda off: pl.BlockSpec((nb, t_len, 2 * GLA_DV), lambda b, p: (b, 0, off + p))
    g3 = g.reshape(bsz, t_len, G_WIDTH)
    n_qk = GLA_QK_WIDTH // (2 * GLA_DK)
    in_specs = [qk_blk(0), qk_blk(n_qk), v_blk(n_qk), v_blk(n_qk + GLA_V_WIDTH // (2 * GLA_DV)),
                pl.BlockSpec((nb, t_len, 2 * GLA_DK), lambda b, p: (b, 0, p)),
                pl.BlockSpec(gnorm.shape, lambda b, p: (0, 0))]
    args = [g3, g3, g3, g3, la.reshape(bsz, t_len, GLA_QK_WIDTH), gnorm]
    if has_s0:
        in_specs.append(pl.BlockSpec((nb, 2, GLA_DK, GLA_DV), lambda b, p: (b, p, 0, 0)))
        args.append(s0)
    return pl.pallas_call(
        kern,
        grid=(bsz // nb, GLA_HEADS // 2),
        in_specs=in_specs,
        out_specs=[pl.BlockSpec((nb, t_len, 2 * GLA_DV), lambda b, p: (b, 0, p)),
                   pl.BlockSpec((nb, 2, GLA_DK, GLA_DV), lambda b, p: (b, p, 0, 0))],
        out_shape=[jax.ShapeDtypeStruct((bsz, t_len, GLA_V_WIDTH), F32),
                   jax.ShapeDtypeStruct((bsz, GLA_HEADS, GLA_DK, GLA_DV), F32)],
        scratch_shapes=[pltpu.VMEM((nb, 2 * GLA_DV, 2 * GLA_DK), F32),
                        pltpu.VMEM((nb, n_chunks, 2 * GLA_DV, 2 * GLA_DK), F32),
                        pltpu.VMEM((nb, n_chunks, SUBLANES, 2 * GLA_DK), F32),
                        pltpu.VMEM((nb, n_chunks, 2 * GLA_DV, 2 * GLA_DK), BF16),
                        pltpu.VMEM((nb, t_len, 2 * GLA_DK), F32)],
        compiler_params=pltpu.CompilerParams(dimension_semantics=("arbitrary", "arbitrary"), vmem_limit_bytes=VMEM_LIMIT),
        name="gla",
    )(*args)


FFN_ROWS = 512
FF_STEPS = 2
FF_CHUNK = D_FF // FF_STEPS
assert FF_CHUNK * FF_STEPS == D_FF and FF_CHUNK % LANES == 0


def _out_ffn_kernel(x_ref, on_ref, og_ref, wo_n_ref, wo_g_ref, ln_ref, wg_ref, wu_ref, wd_ref, y_ref, h_ref):
    j = pl.program_id(1)

    @pl.when(j == 0)
    def _():
        x1 = x_ref[...] + _dot(on_ref[...].astype(BF16), wo_n_ref[...]) + _dot(og_ref[...].astype(BF16), wo_g_ref[...])
        h_ref[...] = (x1 * lax.rsqrt(jnp.mean(x1 * x1, axis=-1, keepdims=True) + EPS) * ln_ref[...]).astype(BF16)
        y_ref[...] = x1

    h = h_ref[...]
    gate = _dot(h, wg_ref[...])
    up = _dot(h, wu_ref[...])
    y_ref[...] += _dot((gate * _sigmoid(gate) * up).astype(BF16), wd_ref[...])


def _out_ffn(x, o_nsa, o_gla, wo_n, wo_g, ln, wg, wu, wd, tm):
    n = x.shape[0]
    row = lambda w: pl.BlockSpec((tm, w), lambda i, j: (i, 0))
    full = lambda a: pl.BlockSpec(a.shape, lambda i, j: (0,) * a.ndim)
    return pl.pallas_call(
        _out_ffn_kernel,
        grid=(n // tm, FF_STEPS),
        in_specs=[row(D_MODEL), row(Q_WIDTH), row(GLA_V_WIDTH), full(wo_n), full(wo_g), full(ln),
                  pl.BlockSpec((D_MODEL, FF_CHUNK), lambda i, j: (0, j)),
                  pl.BlockSpec((D_MODEL, FF_CHUNK), lambda i, j: (0, j)),
                  pl.BlockSpec((FF_CHUNK, D_MODEL), lambda i, j: (j, 0))],
        out_specs=row(D_MODEL),
        out_shape=jax.ShapeDtypeStruct((n, D_MODEL), F32),
        scratch_shapes=[pltpu.VMEM((tm, D_MODEL), BF16)],
        compiler_params=pltpu.CompilerParams(dimension_semantics=("arbitrary", "arbitrary"), vmem_limit_bytes=VMEM_LIMIT),
        name="out_ffn",
    )(x, o_nsa, o_gla, wo_n, wo_g, ln, wg, wu, wd)


def _layer_weights(ln_mix, w_in, q_norm, k_norm, cmp_pos_w, w_a2, b_a, gla_norm, w_out, ln_ffn, w_gate, w_up, w_down,
                   tm_prompt, t_len):
    w_in_t = w_in.T
    o = 0
    parts = []
    for width in (Q_WIDTH, KV6_WIDTH, N_GATES, GLA_QK_WIDTH, GLA_QK_WIDTH, GLA_V_WIDTH, GLA_V_WIDTH, GLA_RANK):
        parts.append(w_in_t[o:o + width])
        o += width
    wq, wkv, wgate, wgq, wgk, wgv, wgg, wga = parts
    wq = jnp.concatenate([wq[h * HEAD_DIM:(h + 1) * HEAD_DIM] for h in SLAB_HEADS], axis=0)
    pad = jnp.zeros((MISC_WIDTH - N_GATES - GLA_RANK, D_MODEL), w_in.dtype)
    wa = jnp.zeros((MISC_WIDTH, GLA_QK_WIDTH), F32).at[N_GATES:N_GATES + GLA_RANK].set(w_a2)
    n_cmp = t_len // CMP_BLOCK
    w_lanes = -(-max(n_cmp, 2 * (-(-t_len // SEL_BLOCK))) // LANES) * LANES
    tok = jnp.arange(t_len)
    lane = jnp.arange(w_lanes)
    blk_of_lane = jnp.where(lane < w_lanes // 2, 2 * lane, 2 * (lane - w_lanes // 2) + 1)
    in_blk = (tok[:, None] // CMP_BLOCK == blk_of_lane[None, :]) & (tok[:, None] < n_cmp * CMP_BLOCK)
    wc = jnp.where(in_blk[None], cmp_pos_w[:, tok % CMP_BLOCK][:, :, None], 0.0)
    kg_rows = jnp.tile(k_norm, (1, NSA_KV_HEADS))
    return dict(
        ln_mix=ln_mix.reshape(1, D_MODEL),
        wq=wq.astype(BF16), wkv=wkv.astype(BF16),
        wg=jnp.concatenate([wgq, wgk, wgv, wgg, wgate, wga, pad], axis=0).astype(BF16),
        qg=jnp.tile(q_norm, LANES // HEAD_DIM).reshape(1, LANES),
        kg=kg_rows, kg_t=jnp.broadcast_to(kg_rows[:, :, None], (3, KV_WIDTH, tm_prompt)),
        wa=wa, ba=b_a.reshape(1, GLA_QK_WIDTH),
        wpos=jnp.broadcast_to(cmp_pos_w[:, :, None], (2, CMP_BLOCK, LANES)), wc=wc,
        gnorm=gla_norm.reshape(1, GLA_DV),
        wo_n=jnp.concatenate([w_out[h * HEAD_DIM:(h + 1) * HEAD_DIM] for h in SLAB_HEADS], axis=0).astype(BF16),
        wo_g=w_out[Q_WIDTH:].astype(BF16),
        ln_ffn=ln_ffn.reshape(1, D_MODEL),
        w_gate=w_gate.astype(BF16), w_up=w_up.astype(BF16), w_down=w_down.astype(BF16),
    )


def _row_tile(n):
    return math.gcd(n, 512)


def _mix_and_ffn(x, w, attn, s0, bsz, t_len, kv_bufs=None, layer=None):
    tm = _row_tile(x.shape[0])
    qn, *kv, g, gates, la = _proj_in(x, w, tm, kv_bufs, layer)
    o_nsa, extra = attn(qn, kv, gates)
    o_gla, s_new = _gla(g, la, w["gnorm"], s0, bsz, t_len)
    y = _out_ffn(x, o_nsa.reshape(-1, Q_WIDTH), o_gla.reshape(-1, GLA_V_WIDTH), w["wo_n"], w["wo_g"], w["ln_ffn"],
                 w["w_gate"], w["w_up"], w["w_down"], math.gcd(x.shape[0], FFN_ROWS))
    return y, kv, s_new, extra


def _token_major(a_t, lead):
    n_lead = len(lead)
    a = a_t.reshape(*lead, -1, NSA_KV_HEADS, HEAD_DIM, a_t.shape[-1])
    return a.transpose(*range(n_lead), n_lead + 3, n_lead, n_lead + 1, n_lead + 2)


def kernel(x_prompt, x_sample, cache_nsa_kv, state_nsa_win, state_gla, page_table, ln_mix, w_in, q_norm, k_norm, cmp_pos_w,
           w_a2, b_a, gla_norm, w_out, ln_ffn, w_gate, w_up, w_down):
    bsz, t_len = x_prompt.shape[:2]
    dbsz, ts = x_sample.shape[:2]
    depth = w_in.shape[0]
    n_phys, page_size = cache_nsa_kv.shape[1:3]
    n_win = state_nsa_win.shape[2]
    cache_t = cache_nsa_kv.transpose(0, 1, 3, 4, 5, 2).reshape(depth, n_phys, 4 * KV_WIDTH, page_size)
    win_t = state_nsa_win.transpose(0, 1, 3, 4, 5, 2).reshape(depth, dbsz, 2 * KV_WIDTH, n_win)
    keep_p = min(WINDOW, t_len)
    tm_prompt = _row_tile(bsz * t_len)
    assert t_len % tm_prompt == 0

    yp = x_prompt.reshape(bsz * t_len, D_MODEL)
    ys = x_sample.reshape(dbsz * ts, D_MODEL)
    kv_bufs = [jnp.zeros((depth, bsz, 4 * KV_WIDTH, t_len), F32), jnp.zeros((depth, bsz, 2 * KV_WIDTH, t_len), F32)]
    gla_p, rows_s, win_s, gla_s = [], [], [], []
    for l in range(depth):
        w = _layer_weights(ln_mix[l], w_in[l], q_norm[l], k_norm[l], cmp_pos_w[l], w_a2[l], b_a[l], gla_norm[l], w_out[l],
                           ln_ffn[l], w_gate[l], w_up[l], w_down[l], tm_prompt, t_len)

        def attn_prompt(qn, kv, gates):
            return _nsa_prompt(qn, kv[0], kv[1], gates, w["wc"], l, bsz, t_len), None

        def attn_sample(qn, kv, gates):
            return _nsa_decode(qn, kv[0], gates, win_t, cache_t, page_table, w["wpos"], l, dbsz, ts)

        yp, kv_bufs, st_p, _ = _mix_and_ffn(yp, w, attn_prompt, None, bsz, t_len, kv_bufs, l)
        ys, (kvn_s,), st_s, nw_s = _mix_and_ffn(ys, w, attn_sample, state_gla[l], dbsz, ts)

        gla_p.append(st_p.astype(state_gla.dtype))
        rows_s.append(kvn_s.reshape(dbsz, ts, 6, NSA_KV_HEADS, HEAD_DIM)[:, :, :4])
        win_s.append(nw_s)
        gla_s.append(st_s.astype(state_gla.dtype))
    lead_p, lead_s = (depth, bsz), (depth, dbsz)
    return (yp.reshape(bsz, t_len, D_MODEL), ys.reshape(dbsz, ts, D_MODEL),
            _token_major(kv_bufs[0], lead_p), _token_major(kv_bufs[1][:, :, :, t_len - keep_p:], lead_p), jnp.stack(gla_p),
            jnp.stack(rows_s), _token_major(jnp.stack(win_s), lead_s), jnp.stack(gla_s))
```

```python
import functools
import math

import jax
import jax.numpy as jnp
import numpy as np
from jax import lax
from jax.experimental import pallas as pl
from jax.experimental.pallas import tpu as pltpu

F32 = jnp.float32
BF16 = jnp.bfloat16

D_MODEL = 1024
NSA_HEADS = 8
NSA_KV_HEADS = 2
NSA_GROUP = NSA_HEADS // NSA_KV_HEADS
HEAD_DIM = 64
CMP_BLOCK = 32
SEL_BLOCK = 64
TOP_K = 16
WINDOW = 512
GLA_HEADS = 4
GLA_DK = 64
GLA_DV = 128
GLA_RANK = 16
GLA_TAU = 16.0
GLA_CHUNK = 64
D_FF = -(-(8 * D_MODEL) // (3 * 256)) * 256
KV_WIDTH = NSA_KV_HEADS * HEAD_DIM
Q_WIDTH = NSA_HEADS * HEAD_DIM
N_GATES = 3 * NSA_HEADS
GLA_QK_WIDTH = GLA_HEADS * GLA_DK
GLA_V_WIDTH = GLA_HEADS * GLA_DV
EPS = 1e-6
NEG_INF = -1e30
FORCE_SCORE = 1e4
BELOW_ALL = -3e38
SCALE = HEAD_DIM ** -0.5
LOG2E = 1.4426950408889634
SLOPES = tuple(tuple(2.0 ** (-8.0 * (k * NSA_GROUP + g + 1) / NSA_HEADS) for g in range(NSA_GROUP))
               for k in range(NSA_KV_HEADS))
SLAB_HEADS = tuple(h for g in range(NSA_GROUP) for h in (g, NSA_GROUP + g))

LANES = 128
SUBLANES = 8
VMEM_LIMIT = 56 * 1024 * 1024
assert KV_WIDTH == LANES and NSA_KV_HEADS == 2

G_WIDTH = 2 * GLA_QK_WIDTH + 2 * GLA_V_WIDTH
MISC_WIDTH = LANES
KV6_WIDTH = 6 * KV_WIDTH


def _dot(a, b):
    return jnp.dot(a, b, preferred_element_type=F32)


def _dot_nt(a, b):
    return lax.dot_general(a, b, (((1,), (1,)), ((), ())), preferred_element_type=F32)


def _dot_tn(a, b):
    return lax.dot_general(a, b, (((0,), (0,)), ((), ())), preferred_element_type=F32)


def _split3(a):
    a0 = a.astype(BF16)
    r = a - a0.astype(F32)
    a1 = r.astype(BF16)
    a2 = (r - a1.astype(F32)).astype(BF16)
    return a0, a1, a2


def _dot_f32(a, b):
    a0, a1, a2 = _split3(a)
    b0, b1, b2 = _split3(b)
    return (_dot(a0, b0) + (_dot(a0, b1) + _dot(a1, b0)) + (_dot(a0, b2) + _dot(a1, b1) + _dot(a2, b0)))


def _dot_2term(a, b):
    a0 = a.astype(BF16)
    a1 = (a - a0.astype(F32)).astype(BF16)
    b0 = b.astype(BF16)
    b1 = (b - b0.astype(F32)).astype(BF16)
    return _dot(a0, b0) + (_dot(a0, b1) + _dot(a1, b0))


def _sigmoid(x):
    return 1.0 / (1.0 + jnp.exp(-x))


def _low_half(width=LANES):
    return lax.broadcasted_iota(jnp.int32, (1, width), 1) < HEAD_DIM


def _half_lane_rms(x, gain):
    x2 = x * x
    lo = _low_half()
    s_lo = jnp.sum(jnp.where(lo, x2, 0.0), axis=-1, keepdims=True)
    s_hi = jnp.sum(jnp.where(lo, 0.0, x2), axis=-1, keepdims=True)
    ms = jnp.where(lo, s_lo, s_hi) * (1.0 / HEAD_DIM)
    return x * lax.rsqrt(ms + EPS) * gain


def _proj_in_kernel(x_ref, ln_ref, wq_ref, wkv_ref, wg_ref, qg_ref, kg_ref, wa_ref, ba_ref, *rest, kv_transposed):
    if kv_transposed:
        _, _, q_out, rows_out, win_out, g_out, gate_out, la_out = rest
    else:
        q_out, kv_out, g_out, gate_out, la_out = rest
    x = x_ref[...]
    h = x * lax.rsqrt(jnp.mean(x * x, axis=-1, keepdims=True) + EPS) * ln_ref[...]
    hb = h.astype(BF16)

    q = _dot_nt(hb, wq_ref[...])
    for j in range(Q_WIDTH // LANES):
        sl = slice(j * LANES, (j + 1) * LANES)
        q_out[:, sl] = _half_lane_rms(q[:, sl], qg_ref[...])

    if kv_transposed:
        kv = _dot_nt(wkv_ref[...], hb)
        tm = kv.shape[1]
        for s in range(6):
            rows = slice(s * KV_WIDTH, (s + 1) * KV_WIDTH)
            out, s_out = (rows_out, s) if s < 4 else (win_out, s - 4)
            dst = slice(s_out * KV_WIDTH, (s_out + 1) * KV_WIDTH)
            if s % 2 == 0:
                k3 = kv[rows].reshape(NSA_KV_HEADS, HEAD_DIM, tm)
                ms = jnp.mean(k3 * k3, axis=1, keepdims=True)
                out[dst, :] = (k3 * lax.rsqrt(ms + EPS)).reshape(KV_WIDTH, tm) * kg_ref[s // 2]
            else:
                out[dst, :] = kv[rows]
    else:
        kv = _dot_nt(hb, wkv_ref[...])
        for s in range(6):
            sl = slice(s * KV_WIDTH, (s + 1) * KV_WIDTH)
            if s % 2 == 0:
                kv_out[:, sl] = _half_lane_rms(kv[:, sl], kg_ref[s // 2:s // 2 + 1, :])
            else:
                kv_out[:, sl] = kv[:, sl]

    gm = _dot_nt(hb, wg_ref[...])
    g_out[...] = gm[:, 0:G_WIDTH]
    m = gm[:, G_WIDTH:G_WIDTH + MISC_WIDTH]
    gate_out[...] = _sigmoid(m)
    z = _dot_2term(m, wa_ref[...]) + ba_ref[...]
    la_out[...] = (jnp.minimum(z, 0.0) - jnp.log1p(jnp.exp(-jnp.abs(z)))) * (1.0 / GLA_TAU)


def _proj_in(x, w, tm, kv_bufs=None, layer=None):
    n = x.shape[0]
    row = lambda width: pl.BlockSpec((tm, width), lambda i: (i, 0))
    full = lambda a: pl.BlockSpec(a.shape, lambda i: (0,) * a.ndim)
    args = [x, w["ln_mix"], w["wq"], w["wkv"], w["wg"], w["qg"], w["kg"] if kv_bufs is None else w["kg_t"],
            w["wa"], w["ba"]]
    in_specs = [row(D_MODEL)] + [full(a) for a in args[1:]]
    tail_specs = [row(G_WIDTH), row(MISC_WIDTH), row(GLA_QK_WIDTH)]
    tail_shapes = [jax.ShapeDtypeStruct((n, width), F32) for width in (G_WIDTH, MISC_WIDTH, GLA_QK_WIDTH)]
    q_shape = jax.ShapeDtypeStruct((n, Q_WIDTH), F32)
    if kv_bufs is None:
        kv_specs = [row(KV6_WIDTH)]
        kv_shapes = [jax.ShapeDtypeStruct((n, KV6_WIDTH), F32)]
        aliases = {}
    else:
        tiles = kv_bufs[0].shape[3] // tm
        kv_specs = [pl.BlockSpec((None, None, b.shape[2], tm), lambda i: (layer, i // tiles, 0, i % tiles)) for b in kv_bufs]
        kv_shapes = [jax.ShapeDtypeStruct(b.shape, b.dtype) for b in kv_bufs]
        aliases = {len(args): 1, len(args) + 1: 2}
        in_specs += [pl.BlockSpec(memory_space=pl.ANY)] * 2
        args += list(kv_bufs)
    return pl.pallas_call(
        functools.partial(_proj_in_kernel, kv_transposed=kv_bufs is not None),
        grid=(n // tm,),
        in_specs=in_specs,
        out_specs=[row(Q_WIDTH)] + kv_specs + tail_specs,
        out_shape=[q_shape] + kv_shapes + tail_shapes,
        input_output_aliases=aliases,
        compiler_params=pltpu.CompilerParams(dimension_semantics=("arbitrary",), vmem_limit_bytes=VMEM_LIMIT),
        name="proj_in",
    )(*args)


def _group_queries(q_tile, kvh):
    keep = _low_half() if kvh == 0 else ~_low_half()
    return jnp.concatenate([jnp.where(keep, q_tile[:, g * LANES:(g + 1) * LANES] * (SCALE * LOG2E), 0.0)
                            for g in range(NSA_GROUP)], axis=0).astype(BF16)


def _masked_neg_dist(dist, ok):
    return jnp.where(ok, -dist.astype(F32), NEG_INF)


def _softmax_rows(s_rows, neg_dist, kvh, tq):
    ps, inv = [], []
    for g in range(NSA_GROUP):
        s = s_rows[g * tq:(g + 1) * tq] + (SLOPES[kvh][g] * LOG2E) * neg_dist
        p = jnp.exp2(s - jnp.max(s, axis=-1, keepdims=True))
        inv.append(1.0 / jnp.sum(p, axis=-1, keepdims=True))
        ps.append(p)
    return jnp.concatenate(ps, axis=0).astype(BF16), jnp.concatenate(inv, axis=0)


N_SEL_SLOTS = 32
ALIBI_SLOT0 = N_SEL_SLOTS
MASK_BIG = 2.0 ** 100


def _bf16_terms(c):
    out = []
    for _ in range(3):
        t = float(np.float32(c).astype(BF16).astype(np.float32))
        out.append(t)
        c = c - t
    return out


ALIBI_TERMS = tuple(tuple(_bf16_terms(s * LOG2E) for s in row) for row in SLOPES)


def _slot_base(kvh):
    return HEAD_DIM if kvh == 0 else 0


def _augmented_keys(k_t, kvh, with_selection):
    n = k_t.shape[1]
    row = lax.broadcasted_iota(jnp.int32, (KV_WIDTH, 1), 0)
    kpos = lax.broadcasted_iota(jnp.int32, (1, n), 1)
    e = row - _slot_base(kvh)
    mine = (e < 0) | (e >= HEAD_DIM)
    k_hi = ((kpos >> 7) << 7).astype(F32)
    k_lo = (kpos & (LANES - 1)).astype(F32)
    extra = jnp.where((e >= ALIBI_SLOT0) & (e < ALIBI_SLOT0 + 3), k_hi,
                      jnp.where((e >= ALIBI_SLOT0 + 3) & (e < ALIBI_SLOT0 + 6), k_lo, 0.0))
    if with_selection:
        extra = jnp.where((e >= 0) & (e < N_SEL_SLOTS) & ((kpos >> int(math.log2(SEL_BLOCK))) == e), 1.0, extra)
    return jnp.where(mine, k_t, extra)


def _queries_t(slabs_t, kvh, alibi, sel_t):
    tq = slabs_t[0].shape[1]
    base = _slot_base(kvh)
    e = lax.broadcasted_iota(jnp.int32, (KV_WIDTH, 1), 0) - base
    own = (e < 0) | (e >= HEAD_DIM)
    if sel_t is None:
        sel_rows = None
    else:
        mask = (sel_t - 1.0) * MASK_BIG
        pieces = [jnp.zeros((base, tq), F32)] if base else []
        pieces += [mask, jnp.zeros((KV_WIDTH - base - mask.shape[0], tq), F32)]
        sel_rows = jnp.concatenate(pieces, axis=0)
    cols = []
    for g in range(NSA_GROUP):
        extra = jnp.zeros((KV_WIDTH, 1), F32)
        if alibi:
            for i, term in enumerate(ALIBI_TERMS[kvh][g] * 2):
                extra = jnp.where(e == ALIBI_SLOT0 + i, term, extra)
        if sel_rows is not None:
            extra = extra + sel_rows
        cols.append(jnp.where(own, slabs_t[g], extra))
    return jnp.concatenate(cols, axis=1).astype(BF16)


def _augmented_values(v_t, kvh):
    mine = (lax.broadcasted_iota(jnp.int32, (KV_WIDTH, 1), 0) < HEAD_DIM) == (kvh == 0)
    return jnp.where(mine, v_t, 1.0).astype(BF16)


def _attend_group_t(s_t, bias_t, v_aug, kvh, tq):
    ps = []
    for g in range(NSA_GROUP):
        s = s_t[:, g * tq:(g + 1) * tq] + bias_t
        ps.append(jnp.exp2((s - jnp.max(s, axis=0, keepdims=True)).astype(BF16)))
    o_t = _dot(v_aug, jnp.concatenate(ps, axis=1))
    denom_row = HEAD_DIM if kvh == 0 else 0
    return o_t, 1.0 / o_t[denom_row:denom_row + 1, :]


def _compressed_probs_t(s_t, qpos_row, kvh, tq, n_cmp, blk_of_row):
    dist = qpos_row - ((blk_of_row + 1) * CMP_BLOCK - 1)
    valid = (dist >= 0) & (blk_of_row < n_cmp)
    nd = _masked_neg_dist(dist, valid)
    ps = []
    imp = jnp.zeros(dist.shape, F32)
    for g in range(NSA_GROUP):
        s = s_t[:, g * tq:(g + 1) * tq] + (SLOPES[kvh][g] * LOG2E) * nd
        e = jnp.exp2(s - jnp.max(s, axis=0, keepdims=True))
        p = jnp.where(valid, e / jnp.sum(e, axis=0, keepdims=True), 0.0)
        imp = imp + p
        ps.append(p)
    return jnp.concatenate(ps, axis=1).astype(BF16), imp


def _split_order_block(w):
    slot = lax.broadcasted_iota(jnp.int32, (w, 1), 0)
    return jnp.where(slot < w // 2, 2 * slot, 2 * (slot - w // 2) + 1)


def _compressed_probs(s_rows, qpos, kvh, tq, n_cmp, blk_of_lane=None):
    w = s_rows.shape[1]
    lane = lax.broadcasted_iota(jnp.int32, (1, w), 1) if blk_of_lane is None else blk_of_lane
    dist = qpos - ((lane + 1) * CMP_BLOCK - 1)
    valid = (dist >= 0) & (lane < n_cmp)
    nd = _masked_neg_dist(dist, valid)
    ps = []
    imp = jnp.zeros((tq, w), F32)
    for g in range(NSA_GROUP):
        s = s_rows[g * tq:(g + 1) * tq] + (SLOPES[kvh][g] * LOG2E) * nd
        e = jnp.exp2(s - jnp.max(s, axis=-1, keepdims=True))
        p = jnp.where(valid, e / jnp.sum(e, axis=-1, keepdims=True), 0.0)
        imp = imp + p
        ps.append(p)
    return jnp.concatenate(ps, axis=0).astype(BF16), imp


def _select_blocks(imp, qpos, n_sel):
    tq, w = imp.shape
    lane = lax.broadcasted_iota(jnp.int32, (1, w), 1)
    pair = imp + pltpu.roll(imp, w - 1, 1)
    blk = lane >> 1
    cur = qpos >> int(math.log2(SEL_BLOCK))
    is_blk = ((lane & 1) == 0) & (blk < n_sel)
    forced = (blk == 0) | (blk == cur) | (blk == cur - 1)
    score = jnp.where(forced, FORCE_SCORE, jnp.where(blk <= cur, pair, NEG_INF))
    score = jnp.where(is_blk, score, BELOW_ALL)
    rank = jnp.zeros((tq, w), jnp.int32)
    for i in range(n_sel):
        col = score[:, 2 * i:2 * i + 1]
        beats = (col > score) | ((col == score) & (lane > 2 * i))
        rank = rank + jnp.where(beats, 1, 0)
    return jnp.where((rank < min(TOP_K, n_sel)) & is_blk, 1.0, 0.0)


def _select_blocks_t(imp_t, qpos_row, n_sel):
    w, tq = imp_t.shape
    nb = -(-n_sel // SUBLANES) * SUBLANES
    pair_t = imp_t[0:nb] + imp_t[w // 2:w // 2 + nb]
    blk = lax.broadcasted_iota(jnp.int32, (nb, 1), 0)
    cur = qpos_row >> int(math.log2(SEL_BLOCK))
    forced = (blk == 0) | (blk == cur) | (blk == cur - 1)
    score = jnp.where(forced, FORCE_SCORE, jnp.where(blk <= cur, pair_t, NEG_INF))
    score = jnp.where(blk < n_sel, score, BELOW_ALL)
    rank = jnp.zeros((nb, tq), jnp.int32)
    for i in range(n_sel):
        row = score[i:i + 1, :]
        beats = (row > score) | ((row == score) & (blk > i))
        rank = rank + jnp.where(beats, 1, 0)
    return jnp.where((rank < min(TOP_K, n_sel)) & (blk < n_sel), 1.0, 0.0)


def _gated_sum(gates, kvh, tq, branches):
    out = []
    for g in range(NSA_GROUP):
        h = kvh * NSA_GROUP + g
        rows = slice(g * tq, (g + 1) * tq)
        out.append(sum(gates[:, c * NSA_HEADS + h:c * NSA_HEADS + h + 1] * o[rows] for c, o in enumerate(branches)))
    return out


def _gated_sum_t(gates_t, kvh, branches):
    total = None
    for c, (o_t, scale) in enumerate(branches):
        first = c * NSA_HEADS + kvh * NSA_GROUP
        row = jnp.concatenate([gates_t[first + g:first + g + 1, :] for g in range(NSA_GROUP)], axis=1)
        term = o_t * (row if scale is None else row * scale)
        total = term if total is None else total + term
    return total


def _store_slabs(o_ref, per_group):
    lo = _low_half()
    for g in range(NSA_GROUP):
        o_ref[:, g * LANES:(g + 1) * LANES] = jnp.where(lo, per_group[0][g], per_group[1][g])


def _nsa_prompt_kernel(q_ref, kv_ref, kvw_ref, gate_ref, wc_ref, o_ref, kc_ref, vc_ref, ks_ref, kw_ref, vs_ref, vw_ref,
                       *, t_len, tq, key_step):
    qi = pl.program_id(1)
    n_cmp = t_len // CMP_BLOCK
    n_sel = -(-t_len // SEL_BLOCK)
    w = kc_ref.shape[0]
    stream = lambda s: slice(s * KV_WIDTH, (s + 1) * KV_WIDTH)

    @pl.when(qi == 0)
    def _():
        kc_ref[...] = _dot_f32(kv_ref[stream(0), :], wc_ref[0]).T.astype(BF16)
        vc_ref[...] = _dot_f32(kv_ref[stream(1), :], wc_ref[1]).astype(BF16)
        for kvh in range(NSA_KV_HEADS):
            ks_ref[kvh] = _augmented_keys(kv_ref[stream(2), :], kvh, True).T.astype(BF16)
            kw_ref[kvh] = _augmented_keys(kvw_ref[stream(0), :], kvh, False).T.astype(BF16)
            vs_ref[kvh] = _augmented_values(kv_ref[stream(3), :], kvh)
            vw_ref[kvh] = _augmented_values(kvw_ref[stream(1), :], kvh)

    q0 = qi * tq
    qpos_row = q0 + lax.broadcasted_iota(jnp.int32, (1, tq), 1)
    blk_of_row = _split_order_block(w)
    n_win = min(WINDOW + tq, t_len)
    w_start = pl.multiple_of(jnp.maximum(q0 + tq - n_win, 0), LANES)

    def body(n_keys):
        q_tile = q_ref[...]
        gates_t = gate_ref[...].T
        slabs_t = [(q_tile[:, g * LANES:(g + 1) * LANES] * (SCALE * LOG2E)).T for g in range(NSA_GROUP)]
        dist_s = qpos_row - lax.broadcasted_iota(jnp.int32, (n_keys, 1), 0)
        dist_w = qpos_row - (w_start + lax.broadcasted_iota(jnp.int32, (n_win, 1), 0))
        bias_s = jnp.where(dist_s >= 0, 0.0, NEG_INF)
        bias_w = jnp.where((dist_w >= 0) & (dist_w < WINDOW), 0.0, NEG_INF)
        groups = range(NSA_KV_HEADS)
        s_w = [_dot(kw_ref[kvh, pl.ds(w_start, n_win), :], _queries_t(slabs_t, kvh, True, None)) for kvh in groups]
        o_c, s_s = [], []
        for kvh in groups:
            s_c = _dot(kc_ref[...], _queries_t(slabs_t, kvh, False, None))
            p_c, imp_t = _compressed_probs_t(s_c, qpos_row, kvh, tq, n_cmp, blk_of_row)
            o_c.append(_dot(vc_ref[...], p_c))
            sel_t = _select_blocks_t(imp_t, qpos_row, n_sel)
            s_s.append(_dot(ks_ref[kvh, 0:n_keys, :], _queries_t(slabs_t, kvh, True, sel_t)))
        totals = []
        for kvh in groups:
            o_w, inv_w = _attend_group_t(s_w[kvh], bias_w, vw_ref[kvh, :, pl.ds(w_start, n_win)], kvh, tq)
            o_s, inv_s = _attend_group_t(s_s[kvh], bias_s, vs_ref[kvh, :, 0:n_keys], kvh, tq)
            totals.append(_gated_sum_t(gates_t, kvh, [(o_c[kvh], None), (o_s, inv_s), (o_w, inv_w)]))
        low_rows = lax.broadcasted_iota(jnp.int32, (KV_WIDTH, 1), 0) < HEAD_DIM
        for g in range(NSA_GROUP):
            cols = slice(g * tq, (g + 1) * tq)
            o_ref[:, g * LANES:(g + 1) * LANES] = jnp.where(low_rows, totals[0][:, cols], totals[1][:, cols]).T

    n_classes = -(-t_len // key_step)
    for c in range(n_classes):
        n_keys = min((c + 1) * key_step, t_len)

        @pl.when((q0 + tq - 1) // key_step == c)
        def _(n_keys=n_keys):
            body(n_keys)


NSA_Q_TILE = 128


def _nsa_prompt(qn, rows_t, win_t, gates, wc, layer, bsz, t_len):
    tq = math.gcd(t_len, NSA_Q_TILE)
    key_step = math.gcd(t_len, 256)
    w = wc.shape[2]
    assert tq % LANES == 0 and key_step % tq == 0 and -(-t_len // SEL_BLOCK) <= N_SEL_SLOTS
    kern = functools.partial(_nsa_prompt_kernel, t_len=t_len, tq=tq, key_step=key_step)
    return pl.pallas_call(
        kern,
        grid=(bsz, t_len // tq),
        in_specs=[
            pl.BlockSpec((None, tq, Q_WIDTH), lambda b, i: (b, i, 0)),
            pl.BlockSpec((None, None, 4 * KV_WIDTH, t_len), lambda b, i: (layer, b, 0, 0)),
            pl.BlockSpec((None, None, 2 * KV_WIDTH, t_len), lambda b, i: (layer, b, 0, 0)),
            pl.BlockSpec((None, tq, MISC_WIDTH), lambda b, i: (b, i, 0)),
            pl.BlockSpec(wc.shape, lambda b, i: (0, 0, 0)),
        ],
        out_specs=pl.BlockSpec((None, tq, Q_WIDTH), lambda b, i: (b, i, 0)),
        out_shape=jax.ShapeDtypeStruct((bsz, t_len, Q_WIDTH), F32),
        scratch_shapes=[pltpu.VMEM((w, KV_WIDTH), BF16), pltpu.VMEM((KV_WIDTH, w), BF16),
                        pltpu.VMEM((NSA_KV_HEADS, t_len, KV_WIDTH), BF16), pltpu.VMEM((NSA_KV_HEADS, t_len, KV_WIDTH), BF16),
                        pltpu.VMEM((NSA_KV_HEADS, KV_WIDTH, t_len), BF16), pltpu.VMEM((NSA_KV_HEADS, KV_WIDTH, t_len), BF16)],
        compiler_params=pltpu.CompilerParams(dimension_semantics=("arbitrary", "arbitrary"), vmem_limit_bytes=VMEM_LIMIT),
        name="nsa_prompt",
    )(qn.reshape(bsz, t_len, Q_WIDTH), rows_t, win_t, gates.reshape(bsz, t_len, MISC_WIDTH), wc)


def _pad_rows(a, n):
    return jnp.concatenate([a, jnp.zeros((n - a.shape[0], a.shape[1]), a.dtype)], axis=0)


def _nsa_decode_kernel(pt_ref, q_ref, kv_ref, gate_ref, win_ref, wpos_ref, *rest, ts, past_len, page_size, n_pages, w):
    page_refs = rest[:n_pages]
    o_ref, win_out_ref, s_ref, p_ref = rest[n_pages:]
    del pt_ref
    n_cmp = past_len // CMP_BLOCK
    n_sel = past_len // SEL_BLOCK + 1
    n_keys = past_len + page_size
    n_win = win_ref.shape[1]
    rows_q = NSA_GROUP * ts
    stream = lambda s: slice(s * KV_WIDTH, (s + 1) * KV_WIDTH)

    new = kv_ref[...]
    q_tile = q_ref[...]
    gates = gate_ref[...]
    qs = jnp.concatenate([_group_queries(q_tile, kvh) for kvh in range(NSA_KV_HEADS)], axis=0)
    qpos = past_len + lax.broadcasted_iota(jnp.int32, (ts, 1), 0)

    per_pair = 2 * page_size // CMP_BLOCK
    wk = jnp.concatenate([wpos_ref[0]] * per_pair, axis=0).reshape(per_pair, CMP_BLOCK, KV_WIDTH)
    wv = jnp.concatenate([wpos_ref[1]] * per_pair, axis=0).reshape(per_pair, CMP_BLOCK, KV_WIDTH)
    kcs, vcs = [], []
    for j in range(0, n_pages, 2):
        k2 = jnp.concatenate([page_refs[j][stream(0), :].T, page_refs[j + 1][stream(0), :].T], axis=0)
        v2 = jnp.concatenate([page_refs[j][stream(1), :].T, page_refs[j + 1][stream(1), :].T], axis=0)
        kcs.append(jnp.sum(k2.reshape(per_pair, CMP_BLOCK, KV_WIDTH) * wk, axis=1))
        vcs.append(jnp.sum(v2.reshape(per_pair, CMP_BLOCK, KV_WIDTH) * wv, axis=1))
        for jj in (j, j + 1):
            s_ref[:, jj * page_size:(jj + 1) * page_size] = _dot(qs, page_refs[jj][stream(2), :].astype(BF16))
    kc = _pad_rows(jnp.concatenate(kcs, axis=0), w).astype(BF16)
    vc = _pad_rows(jnp.concatenate(vcs, axis=0), w).astype(BF16)

    s_c = _dot_nt(qs, kc)
    p_cs, sels = [], []
    for kvh in range(NSA_KV_HEADS):
        p_c, imp = _compressed_probs(s_c[kvh * rows_q:(kvh + 1) * rows_q], qpos, kvh, ts, n_cmp)
        p_cs.append(p_c)
        sels.append(_select_blocks(imp, qpos, n_sel))
    o_c = _dot(jnp.concatenate(p_cs, axis=0), vc)

    new_k = _pad_rows(new[:, stream(2)], page_size).astype(BF16)
    new_v = _pad_rows(new[:, stream(3)], page_size).astype(BF16)
    s_ref[:, past_len:n_keys] = _dot_nt(qs, new_k)
    dist_s = qpos - lax.broadcasted_iota(jnp.int32, (1, n_keys), 1)
    blocks_per_page = page_size // SEL_BLOCK
    page_lane_blk = lax.broadcasted_iota(jnp.int32, (1, page_size), 1) >> int(math.log2(SEL_BLOCK))
    invs = []
    for kvh in range(NSA_KV_HEADS):
        sel = sels[kvh]
        pieces = []
        for j in range(n_pages + 1):
            piece = jnp.zeros((ts, page_size), F32)
            for r in range(blocks_per_page):
                blk = j * blocks_per_page + r
                if blk < n_sel:
                    piece = jnp.where(page_lane_blk == r, sel[:, 2 * blk:2 * blk + 1], piece)
            pieces.append(piece)
        keymask = jnp.concatenate(pieces, axis=1)
        nd_s = _masked_neg_dist(dist_s, (keymask > 0.5) & (dist_s >= 0))
        p_s, inv_s = _softmax_rows(s_ref[kvh * rows_q:(kvh + 1) * rows_q, :], nd_s, kvh, ts)
        p_ref[kvh * rows_q:(kvh + 1) * rows_q, :] = p_s
        invs.append(inv_s)
    o_s = _dot(p_ref[:, past_len:n_keys], new_v)
    for j in range(n_pages):
        o_s = o_s + _dot_nt(p_ref[:, j * page_size:(j + 1) * page_size], page_refs[j][stream(3), :].astype(BF16))
    o_s = o_s * jnp.concatenate(invs, axis=0)

    new_kw = _pad_rows(new[:, stream(4)], LANES).astype(BF16)
    new_vw = _pad_rows(new[:, stream(5)], LANES).astype(BF16)
    s_w = jnp.concatenate([_dot(qs, win_ref[0:KV_WIDTH, :].astype(BF16)), _dot_nt(qs, new_kw)], axis=1)
    wpos = jnp.concatenate([past_len - n_win + lax.broadcasted_iota(jnp.int32, (1, n_win), 1),
                            past_len + lax.broadcasted_iota(jnp.int32, (1, LANES), 1)], axis=1)
    dist_w = qpos - wpos
    nd_w = _masked_neg_dist(dist_w, (dist_w >= 0) & (dist_w < WINDOW))
    p_ws, inv_ws = [], []
    for kvh in range(NSA_KV_HEADS):
        p_w, inv_w = _softmax_rows(s_w[kvh * rows_q:(kvh + 1) * rows_q], nd_w, kvh, ts)
        p_ws.append(p_w)
        inv_ws.append(inv_w)
    p_w = jnp.concatenate(p_ws, axis=0)
    o_w = (_dot_nt(p_w[:, 0:n_win], win_ref[KV_WIDTH:2 * KV_WIDTH, :].astype(BF16)) + _dot(p_w[:, n_win:], new_vw))
    o_w = o_w * jnp.concatenate(inv_ws, axis=0)

    per_group = [_gated_sum(gates, kvh, ts, [o[kvh * rows_q:(kvh + 1) * rows_q] for o in (o_c, o_s, o_w)])
                 for kvh in range(NSA_KV_HEADS)]
    _store_slabs(o_ref, per_group)

    new_t = _pad_rows(new[:, 4 * KV_WIDTH:6 * KV_WIDTH], LANES).T
    new_t = pltpu.roll(new_t, LANES - ts, 1)
    shifted = pltpu.roll(win_ref[...], n_win - ts, 1)
    tail_lane = lax.broadcasted_iota(jnp.int32, (1, LANES), 1)
    win_out_ref[:, 0:n_win - LANES] = shifted[:, 0:n_win - LANES]
    win_out_ref[:, n_win - LANES:n_win] = jnp.where(tail_lane >= LANES - ts, new_t, shifted[:, n_win - LANES:n_win])


def _nsa_decode(qn, kvn, gates, win_t, cache_t, page_table, wpos_rows, layer, dbsz, ts):
    n_pages = page_table.shape[1]
    page_size = cache_t.shape[3]
    past_len = n_pages * page_size
    n_win = win_t.shape[3]
    assert ts < CMP_BLOCK and ts % SUBLANES == 0 and page_size == LANES and n_pages % 2 == 0
    assert n_win == WINDOW and past_len >= WINDOW
    n_cmp = past_len // CMP_BLOCK
    n_sel = past_len // SEL_BLOCK + 1
    w = -(-max(n_cmp, 2 * n_sel) // LANES) * LANES
    n_keys = past_len + page_size
    kern = functools.partial(_nsa_decode_kernel, ts=ts, past_len=past_len, page_size=page_size, n_pages=n_pages, w=w)

    def page_spec(j):
        return pl.BlockSpec((None, None, 4 * KV_WIDTH, page_size), lambda b, pt: (layer, pt[b, j], 0, 0))

    grid_spec = pltpu.PrefetchScalarGridSpec(
        num_scalar_prefetch=1,
        grid=(dbsz,),
        in_specs=[
            pl.BlockSpec((None, ts, Q_WIDTH), lambda b, pt: (b, 0, 0)),
            pl.BlockSpec((None, ts, KV6_WIDTH), lambda b, pt: (b, 0, 0)),
            pl.BlockSpec((None, ts, MISC_WIDTH), lambda b, pt: (b, 0, 0)),
            pl.BlockSpec((None, None, 2 * KV_WIDTH, n_win), lambda b, pt: (layer, b, 0, 0)),
            pl.BlockSpec(wpos_rows.shape, lambda b, pt: (0, 0, 0)),
        ] + [page_spec(j) for j in range(n_pages)],
        out_specs=[
            pl.BlockSpec((None, ts, Q_WIDTH), lambda b, pt: (b, 0, 0)),
            pl.BlockSpec((None, 2 * KV_WIDTH, n_win), lambda b, pt: (b, 0, 0)),
        ],
        scratch_shapes=[pltpu.VMEM((NSA_HEADS * ts, n_keys), F32), pltpu.VMEM((NSA_HEADS * ts, n_keys), BF16)],
    )
    return pl.pallas_call(
        kern,
        grid_spec=grid_spec,
        out_shape=[jax.ShapeDtypeStruct((dbsz, ts, Q_WIDTH), F32), jax.ShapeDtypeStruct((dbsz, 2 * KV_WIDTH, n_win), F32)],
        compiler_params=pltpu.CompilerParams(dimension_semantics=("arbitrary",), vmem_limit_bytes=VMEM_LIMIT),
        name="nsa_decode",
    )(page_table, qn.reshape(dbsz, ts, Q_WIDTH), kvn.reshape(dbsz, ts, KV6_WIDTH), gates.reshape(dbsz, ts, MISC_WIDTH),
      win_t, wpos_rows, *([cache_t] * n_pages))


def _gla_kernel(*refs, t_len, chunk, has_s0, nb):
    for i in range(nb):
        _gla_one([r if j == 5 else r.at[i] for j, r in enumerate(refs)], t_len, chunk, has_s0)


def _gla_one(refs, t_len, chunk, has_s0):
    if has_s0:
        q_ref, k_ref, v_ref, gg_ref, la_ref, gn_ref, s0_ref, o_ref, s_out_ref, st_ref, u_ref, d_ref, sb_ref, qe_ref = refs
    else:
        q_ref, k_ref, v_ref, gg_ref, la_ref, gn_ref, o_ref, s_out_ref, st_ref, u_ref, d_ref, sb_ref, qe_ref = refs
    dk2, dv2 = 2 * GLA_DK, 2 * GLA_DV
    n_chunks = t_len // chunk
    per_group = math.gcd(n_chunks, 4)
    rows_g = per_group * chunk
    n_groups = n_chunks // per_group
    rp = max(rows_g, LANES)
    pad = rp - rows_g

    rr = lax.broadcasted_iota(jnp.int32, (dv2, dk2), 0) // GLA_DV
    cc = lax.broadcasted_iota(jnp.int32, (dv2, dk2), 1) // GLA_DK
    diag = rr == cc
    if has_s0:
        z = jnp.zeros((GLA_DK, GLA_DV), F32)
        s_full = jnp.concatenate([jnp.concatenate([s0_ref[0], z], axis=1), jnp.concatenate([z, s0_ref[1]], axis=1)], axis=0)
        st_ref[...] = s_full.T
    else:
        st_ref[...] = jnp.zeros((dv2, dk2), F32)

    lane_head = lax.broadcasted_iota(jnp.int32, (1, dk2), 1) // GLA_DK
    trow = lax.broadcasted_iota(jnp.int32, (rows_g, rp), 0)
    tcol = lax.broadcasted_iota(jnp.int32, (rows_g, rp), 1)
    causal = (tcol <= trow) & (tcol // chunk == trow // chunk)
    tril = jnp.where(causal, 1.0, 0.0).astype(BF16)
    prow_chunk = lax.broadcasted_iota(jnp.int32, (rp, 1), 0) // chunk
    mid = chunk // 2
    gn = gn_ref[...]

    def pad_rows(a):
        return a if pad == 0 else jnp.concatenate([a, jnp.zeros((pad, a.shape[1]), a.dtype)], axis=0)

    def aligned(start, multiple):
        return start if isinstance(start, int) else pl.multiple_of(start, multiple)

    def per_chunk_row(cum, r):
        return jnp.concatenate([jnp.broadcast_to(cum[c * chunk + r:c * chunk + r + 1, :], (chunk, dk2))
                                for c in range(per_group)], axis=0)

    def group_local(gi, carry):
        r0 = aligned(gi * rows_g, rows_g)
        rows = pl.ds(r0, rows_g)
        q = q_ref[rows, :] * (GLA_DK ** -0.5)
        k = k_ref[rows, :]
        la0, la1, _ = _split3(pad_rows(la_ref[rows, :]))
        cum = _dot(tril, la0) + _dot(tril, la1)
        m = per_chunk_row(cum, mid)
        last = per_chunk_row(cum, chunk - 1)
        qe_ref[rows, :] = q * jnp.exp(cum)
        qs = q * jnp.exp(cum - m)
        ks = pad_rows((k * jnp.exp(m - cum)).astype(BF16))
        vp = pad_rows(v_ref[rows, :].astype(BF16))
        intra = []
        for h in range(2):
            a = _dot_nt(jnp.where(lane_head == h, qs, 0.0).astype(BF16), ks)
            a = jnp.where(causal, a, 0.0).astype(BF16)
            intra.append(_dot(a, vp[:, h * GLA_DV:(h + 1) * GLA_DV]))
        o_ref[rows, :] = jnp.concatenate(intra, axis=1)
        kd = pad_rows(k * jnp.exp(last - cum))
        kd_by_chunk = jnp.concatenate([jnp.where(prow_chunk == c, kd, 0.0) for c in range(per_group)], axis=1).astype(BF16)
        u_all = _dot_tn(vp, kd_by_chunk)
        for c in range(per_group):
            ci = gi * per_group + c
            u_ref[ci] = jnp.where(diag, u_all[:, c * dk2:(c + 1) * dk2], 0.0)
            d_ref[ci] = jnp.exp(last[c * chunk:c * chunk + SUBLANES, :])
        return carry

    def chunk_state(ci, carry):
        st = st_ref[...]
        sb_ref[ci] = st.astype(BF16)
        st_ref[...] = st * d_ref[ci][0:1, :] + u_ref[ci]
        return carry

    def group_output(gi, carry):
        r0 = aligned(gi * rows_g, rows_g)
        rows = pl.ds(r0, rows_g)
        inter = [_dot_nt(qe_ref[pl.ds(aligned(r0 + c * chunk, chunk), chunk), :].astype(BF16),
                         sb_ref[gi * per_group + c]) for c in range(per_group)]
        o = o_ref[rows, :] + jnp.concatenate(inter, axis=0)
        gg = gg_ref[rows, :]
        outs = []
        for h in range(2):
            oh = o[:, h * GLA_DV:(h + 1) * GLA_DV]
            y = oh * lax.rsqrt(jnp.mean(oh * oh, axis=-1, keepdims=True) + EPS) * gn
            gh = gg[:, h * GLA_DV:(h + 1) * GLA_DV]
            outs.append(y * (gh * _sigmoid(gh)))
        o_ref[rows, :] = jnp.concatenate(outs, axis=1)
        return carry

    def loop(n, body):
        if n == 1:
            body(0, 0)
        else:
            lax.fori_loop(0, n, body, 0)

    loop(n_groups, group_local)
    loop(n_chunks, chunk_state)
    loop(n_groups, group_output)
    s_fin = st_ref[...].T
    s_out_ref[0] = s_fin[0:GLA_DK, 0:GLA_DV]
    s_out_ref[1] = s_fin[GLA_DK:dk2, GLA_DV:dv2]


def _gla(g, la, gnorm, s0, bsz, t_len):
    chunk = math.gcd(t_len, GLA_CHUNK)
    n_chunks = t_len // chunk
    nb = math.gcd(bsz, 8) if n_chunks == 1 else 1
    has_s0 = s0 is not None
    kern = functools.partial(_gla_kernel, t_len=t_len, chunk=chunk, has_s0=has_s0, nb=nb)
    qk_blk = lambda off: pl.BlockSpec((nb, t_len, 2 * GLA_DK), lambda b, p: (b, 0, off + p))
    v_blk = lambda off: pl.BlockSpec((nb, t_len, 2 * GLA_DV), lambda b, p: (b, 0, off + p))
    g3 = g.reshape(bsz, t_len, G_WIDTH)
    n_qk = GLA_QK_WIDTH // (2 * GLA_DK)
    in_specs = [qk_blk(0), qk_blk(n_qk), v_blk(n_qk), v_blk(n_qk + GLA_V_WIDTH // (2 * GLA_DV)),
                pl.BlockSpec((nb, t_len, 2 * GLA_DK), lambda b, p: (b, 0, p)),
                pl.BlockSpec(gnorm.shape, lambda b, p: (0, 0))]
    args = [g3, g3, g3, g3, la.reshape(bsz, t_len, GLA_QK_WIDTH), gnorm]
    if has_s0:
        in_specs.append(pl.BlockSpec((nb, 2, GLA_DK, GLA_DV), lambda b, p: (b, p, 0, 0)))
        args.append(s0)
    return pl.pallas_call(
        kern,
        grid=(bsz // nb, GLA_HEADS // 2),
        in_specs=in_specs,
        out_specs=[pl.BlockSpec((nb, t_len, 2 * GLA_DV), lambda b, p: (b, 0, p)),
                   pl.BlockSpec((nb, 2, GLA_DK, GLA_DV), lambda b, p: (b, p, 0, 0))],
        out_shape=[jax.ShapeDtypeStruct((bsz, t_len, GLA_V_WIDTH), F32),
                   jax.ShapeDtypeStruct((bsz, GLA_HEADS, GLA_DK, GLA_DV), F32)],
        scratch_shapes=[pltpu.VMEM((nb, 2 * GLA_DV, 2 * GLA_DK), F32),
                        pltpu.VMEM((nb, n_chunks, 2 * GLA_DV, 2 * GLA_DK), F32),
                        pltpu.VMEM((nb, n_chunks, SUBLANES, 2 * GLA_DK), F32),
                        pltpu.VMEM((nb, n_chunks, 2 * GLA_DV, 2 * GLA_DK), BF16),
                        pltpu.VMEM((nb, t_len, 2 * GLA_DK), F32)],
        compiler_params=pltpu.CompilerParams(dimension_semantics=("arbitrary", "arbitrary"), vmem_limit_bytes=VMEM_LIMIT),
        name="gla",
    )(*args)


FFN_ROWS = 512
FF_STEPS = 2
FF_CHUNK = D_FF // FF_STEPS
assert FF_CHUNK * FF_STEPS == D_FF and FF_CHUNK % LANES == 0


def _out_ffn_kernel(x_ref, on_ref, og_ref, wo_n_ref, wo_g_ref, ln_ref, wg_ref, wu_ref, wd_ref, y_ref, h_ref):
    j = pl.program_id(1)

    @pl.when(j == 0)
    def _():
        x1 = x_ref[...] + _dot(on_ref[...].astype(BF16), wo_n_ref[...]) + _dot(og_ref[...].astype(BF16), wo_g_ref[...])
        h_ref[...] = (x1 * lax.rsqrt(jnp.mean(x1 * x1, axis=-1, keepdims=True) + EPS) * ln_ref[...]).astype(BF16)
        y_ref[...] = x1

    h = h_ref[...]
    gate = _dot(h, wg_ref[...])
    up = _dot(h, wu_ref[...])
    y_ref[...] += _dot((gate * _sigmoid(gate) * up).astype(BF16), wd_ref[...])


def _out_ffn(x, o_nsa, o_gla, wo_n, wo_g, ln, wg, wu, wd, tm):
    n = x.shape[0]
    row = lambda w: pl.BlockSpec((tm, w), lambda i, j: (i, 0))
    full = lambda a: pl.BlockSpec(a.shape, lambda i, j: (0,) * a.ndim)
    return pl.pallas_call(
        _out_ffn_kernel,
        grid=(n // tm, FF_STEPS),
        in_specs=[row(D_MODEL), row(Q_WIDTH), row(GLA_V_WIDTH), full(wo_n), full(wo_g), full(ln),
                  pl.BlockSpec((D_MODEL, FF_CHUNK), lambda i, j: (0, j)),
                  pl.BlockSpec((D_MODEL, FF_CHUNK), lambda i, j: (0, j)),
                  pl.BlockSpec((FF_CHUNK, D_MODEL), lambda i, j: (j, 0))],
        out_specs=row(D_MODEL),
        out_shape=jax.ShapeDtypeStruct((n, D_MODEL), F32),
        scratch_shapes=[pltpu.VMEM((tm, D_MODEL), BF16)],
        compiler_params=pltpu.CompilerParams(dimension_semantics=("arbitrary", "arbitrary"), vmem_limit_bytes=VMEM_LIMIT),
        name="out_ffn",
    )(x, o_nsa, o_gla, wo_n, wo_g, ln, wg, wu, wd)


def _layer_weights(ln_mix, w_in, q_norm, k_norm, cmp_pos_w, w_a2, b_a, gla_norm, w_out, ln_ffn, w_gate, w_up, w_down,
                   tm_prompt, t_len):
    w_in_t = w_in.T
    o = 0
    parts = []
    for width in (Q_WIDTH, KV6_WIDTH, N_GATES, GLA_QK_WIDTH, GLA_QK_WIDTH, GLA_V_WIDTH, GLA_V_WIDTH, GLA_RANK):
        parts.append(w_in_t[o:o + width])
        o += width
    wq, wkv, wgate, wgq, wgk, wgv, wgg, wga = parts
    wq = jnp.concatenate([wq[h * HEAD_DIM:(h + 1) * HEAD_DIM] for h in SLAB_HEADS], axis=0)
    pad = jnp.zeros((MISC_WIDTH - N_GATES - GLA_RANK, D_MODEL), w_in.dtype)
    wa = jnp.zeros((MISC_WIDTH, GLA_QK_WIDTH), F32).at[N_GATES:N_GATES + GLA_RANK].set(w_a2)
    n_cmp = t_len // CMP_BLOCK
    w_lanes = -(-max(n_cmp, 2 * (-(-t_len // SEL_BLOCK))) // LANES) * LANES
    tok = jnp.arange(t_len)
    lane = jnp.arange(w_lanes)
    blk_of_lane = jnp.where(lane < w_lanes // 2, 2 * lane, 2 * (lane - w_lanes // 2) + 1)
    in_blk = (tok[:, None] // CMP_BLOCK == blk_of_lane[None, :]) & (tok[:, None] < n_cmp * CMP_BLOCK)
    wc = jnp.where(in_blk[None], cmp_pos_w[:, tok % CMP_BLOCK][:, :, None], 0.0)
    kg_rows = jnp.tile(k_norm, (1, NSA_KV_HEADS))
    return dict(
        ln_mix=ln_mix.reshape(1, D_MODEL),
        wq=wq.astype(BF16), wkv=wkv.astype(BF16),
        wg=jnp.concatenate([wgq, wgk, wgv, wgg, wgate, wga, pad], axis=0).astype(BF16),
        qg=jnp.tile(q_norm, LANES // HEAD_DIM).reshape(1, LANES),
        kg=kg_rows, kg_t=jnp.broadcast_to(kg_rows[:, :, None], (3, KV_WIDTH, tm_prompt)),
        wa=wa, ba=b_a.reshape(1, GLA_QK_WIDTH),
        wpos=jnp.broadcast_to(cmp_pos_w[:, :, None], (2, CMP_BLOCK, LANES)), wc=wc,
        gnorm=gla_norm.reshape(1, GLA_DV),
        wo_n=jnp.concatenate([w_out[h * HEAD_DIM:(h + 1) * HEAD_DIM] for h in SLAB_HEADS], axis=0).astype(BF16),
        wo_g=w_out[Q_WIDTH:].astype(BF16),
        ln_ffn=ln_ffn.reshape(1, D_MODEL),
        w_gate=w_gate.astype(BF16), w_up=w_up.astype(BF16), w_down=w_down.astype(BF16),
    )


def _row_tile(n):
    return math.gcd(n, 512)


def _mix_and_ffn(x, w, attn, s0, bsz, t_len, kv_bufs=None, layer=None):
    tm = _row_tile(x.shape[0])
    qn, *kv, g, gates, la = _proj_in(x, w, tm, kv_bufs, layer)
    o_nsa, extra = attn(qn, kv, gates)
    o_gla, s_new = _gla(g, la, w["gnorm"], s0, bsz, t_len)
    y = _out_ffn(x, o_nsa.reshape(-1, Q_WIDTH), o_gla.reshape(-1, GLA_V_WIDTH), w["wo_n"], w["wo_g"], w["ln_ffn"],
                 w["w_gate"], w["w_up"], w["w_down"], math.gcd(x.shape[0], FFN_ROWS))
    return y, kv, s_new, extra


def _token_major(a_t, lead):
    n_lead = len(lead)
    a = a_t.reshape(*lead, -1, NSA_KV_HEADS, HEAD_DIM, a_t.shape[-1])
    return a.transpose(*range(n_lead), n_lead + 3, n_lead, n_lead + 1, n_lead + 2)


def kernel(x_prompt, x_sample, cache_nsa_kv, state_nsa_win, state_gla, page_table, ln_mix, w_in, q_norm, k_norm, cmp_pos_w,
           w_a2, b_a, gla_norm, w_out, ln_ffn, w_gate, w_up, w_down):
    bsz, t_len = x_prompt.shape[:2]
    dbsz, ts = x_sample.shape[:2]
    depth = w_in.shape[0]
    n_phys, page_size = cache_nsa_kv.shape[1:3]
    n_win = state_nsa_win.shape[2]
    cache_t = cache_nsa_kv.transpose(0, 1, 3, 4, 5, 2).reshape(depth, n_phys, 4 * KV_WIDTH, page_size)
    win_t = state_nsa_win.transpose(0, 1, 3, 4, 5, 2).reshape(depth, dbsz, 2 * KV_WIDTH, n_win)
    keep_p = min(WINDOW, t_len)
    tm_prompt = _row_tile(bsz * t_len)
    assert t_len % tm_prompt == 0

    yp = x_prompt.reshape(bsz * t_len, D_MODEL)
    ys = x_sample.reshape(dbsz * ts, D_MODEL)
    kv_bufs = [jnp.zeros((depth, bsz, 4 * KV_WIDTH, t_len), F32), jnp.zeros((depth, bsz, 2 * KV_WIDTH, t_len), F32)]
    gla_p, rows_s, win_s, gla_s = [], [], [], []
    for l in range(depth):
        w = _layer_weights(ln_mix[l], w_in[l], q_norm[l], k_norm[l], cmp_pos_w[l], w_a2[l], b_a[l], gla_norm[l], w_out[l],
                           ln_ffn[l], w_gate[l], w_up[l], w_down[l], tm_prompt, t_len)

        def attn_prompt(qn, kv, gates):
            return _nsa_prompt(qn, kv[0], kv[1], gates, w["wc"], l, bsz, t_len), None

        def attn_sample(qn, kv, gates):
            return _nsa_decode(qn, kv[0], gates, win_t, cache_t, page_table, w["wpos"], l, dbsz, ts)

        yp, kv_bufs, st_p, _ = _mix_and_ffn(yp, w, attn_prompt, None, bsz, t_len, kv_bufs, l)
        ys, (kvn_s,), st_s, nw_s = _mix_and_ffn(ys, w, attn_sample, state_gla[l], dbsz, ts)

        gla_p.append(st_p.astype(state_gla.dtype))
        rows_s.append(kvn_s.reshape(dbsz, ts, 6, NSA_KV_HEADS, HEAD_DIM)[:, :, :4])
        win_s.append(nw_s)
        gla_s.append(st_s.astype(state_gla.dtype))
    lead_p, lead_s = (depth, bsz), (depth, dbsz)
    return (yp.reshape(bsz, t_len, D_MODEL), ys.reshape(dbsz, ts, D_MODEL),
            _token_major(kv_bufs[0], lead_p), _token_major(kv_bufs[1][:, :, :, t_len - keep_p:], lead_p), jnp.stack(gla_p),
            jnp.stack(rows_s), _token_major(jnp.stack(win_s), lead_s), jnp.stack(gla_s))
```

```python
import functools
import math

import jax
import jax.numpy as jnp
import numpy as np
from jax import lax
from jax.experimental import pallas as pl
from jax.experimental.pallas import tpu as pltpu

F32 = jnp.float32
BF16 = jnp.bfloat16

D_MODEL = 1024
NSA_HEADS = 8
NSA_KV_HEADS = 2
NSA_GROUP = NSA_HEADS // NSA_KV_HEADS
HEAD_DIM = 64
CMP_BLOCK = 32
SEL_BLOCK = 64
TOP_K = 16
WINDOW = 512
GLA_HEADS = 4
GLA_DK = 64
GLA_DV = 128
GLA_RANK = 16
GLA_TAU = 16.0
GLA_CHUNK = 64
D_FF = -(-(8 * D_MODEL) // (3 * 256)) * 256
KV_WIDTH = NSA_KV_HEADS * HEAD_DIM
Q_WIDTH = NSA_HEADS * HEAD_DIM
N_GATES = 3 * NSA_HEADS
GLA_QK_WIDTH = GLA_HEADS * GLA_DK
GLA_V_WIDTH = GLA_HEADS * GLA_DV
EPS = 1e-6
NEG_INF = -1e30
FORCE_SCORE = 1e4
BELOW_ALL = -3e38
SCALE = HEAD_DIM ** -0.5
LOG2E = 1.4426950408889634
SLOPES = tuple(tuple(2.0 ** (-8.0 * (k * NSA_GROUP + g + 1) / NSA_HEADS) for g in range(NSA_GROUP))
               for k in range(NSA_KV_HEADS))
SLAB_HEADS = tuple(h for g in range(NSA_GROUP) for h in (g, NSA_GROUP + g))

LANES = 128
SUBLANES = 8
VMEM_LIMIT = 56 * 1024 * 1024
assert KV_WIDTH == LANES and NSA_KV_HEADS == 2

G_WIDTH = 2 * GLA_QK_WIDTH + 2 * GLA_V_WIDTH
MISC_WIDTH = LANES
KV6_WIDTH = 6 * KV_WIDTH


def _dot(a, b):
    return jnp.dot(a, b, preferred_element_type=F32)


def _dot_nt(a, b):
    return lax.dot_general(a, b, (((1,), (1,)), ((), ())), preferred_element_type=F32)


def _dot_tn(a, b):
    return lax.dot_general(a, b, (((0,), (0,)), ((), ())), preferred_element_type=F32)


def _split3(a):
    a0 = a.astype(BF16)
    r = a - a0.astype(F32)
    a1 = r.astype(BF16)
    a2 = (r - a1.astype(F32)).astype(BF16)
    return a0, a1, a2


def _dot_f32(a, b):
    a0, a1, a2 = _split3(a)
    b0, b1, b2 = _split3(b)
    return (_dot(a0, b0) + (_dot(a0, b1) + _dot(a1, b0)) + (_dot(a0, b2) + _dot(a1, b1) + _dot(a2, b0)))


def _dot_2term(a, b):
    a0 = a.astype(BF16)
    a1 = (a - a0.astype(F32)).astype(BF16)
    b0 = b.astype(BF16)
    b1 = (b - b0.astype(F32)).astype(BF16)
    return _dot(a0, b0) + (_dot(a0, b1) + _dot(a1, b0))


def _sigmoid(x):
    return 1.0 / (1.0 + jnp.exp(-x))


def _low_half(width=LANES):
    return lax.broadcasted_iota(jnp.int32, (1, width), 1) < HEAD_DIM


def _half_lane_rms(x, gain):
    x2 = x * x
    lo = _low_half()
    s_lo = jnp.sum(jnp.where(lo, x2, 0.0), axis=-1, keepdims=True)
    s_hi = jnp.sum(jnp.where(lo, 0.0, x2), axis=-1, keepdims=True)
    ms = jnp.where(lo, s_lo, s_hi) * (1.0 / HEAD_DIM)
    return x * lax.rsqrt(ms + EPS) * gain


def _proj_in_kernel(x_ref, ln_ref, wq_ref, wkv_ref, wg_ref, qg_ref, kg_ref, wa_ref, ba_ref, *rest, kv_transposed):
    if kv_transposed:
        _, _, q_out, rows_out, win_out, g_out, gate_out, la_out = rest
    else:
        q_out, kv_out, g_out, gate_out, la_out = rest
    x = x_ref[...]
    h = x * lax.rsqrt(jnp.mean(x * x, axis=-1, keepdims=True) + EPS) * ln_ref[...]
    hb = h.astype(BF16)

    q = _dot_nt(hb, wq_ref[...])
    for j in range(Q_WIDTH // LANES):
        sl = slice(j * LANES, (j + 1) * LANES)
        q_out[:, sl] = _half_lane_rms(q[:, sl], qg_ref[...])

    if kv_transposed:
        kv = _dot_nt(wkv_ref[...], hb)
        tm = kv.shape[1]
        for s in range(6):
            rows = slice(s * KV_WIDTH, (s + 1) * KV_WIDTH)
            out, s_out = (rows_out, s) if s < 4 else (win_out, s - 4)
            dst = slice(s_out * KV_WIDTH, (s_out + 1) * KV_WIDTH)
            if s % 2 == 0:
                k3 = kv[rows].reshape(NSA_KV_HEADS, HEAD_DIM, tm)
                ms = jnp.mean(k3 * k3, axis=1, keepdims=True)
                out[dst, :] = (k3 * lax.rsqrt(ms + EPS)).reshape(KV_WIDTH, tm) * kg_ref[s // 2]
            else:
                out[dst, :] = kv[rows]
    else:
        kv = _dot_nt(hb, wkv_ref[...])
        for s in range(6):
            sl = slice(s * KV_WIDTH, (s + 1) * KV_WIDTH)
            if s % 2 == 0:
                kv_out[:, sl] = _half_lane_rms(kv[:, sl], kg_ref[s // 2:s // 2 + 1, :])
            else:
                kv_out[:, sl] = kv[:, sl]

    gm = _dot_nt(hb, wg_ref[...])
    g_out[...] = gm[:, 0:G_WIDTH]
    m = gm[:, G_WIDTH:G_WIDTH + MISC_WIDTH]
    gate_out[...] = _sigmoid(m)
    z = _dot_2term(m, wa_ref[...]) + ba_ref[...]
    la_out[...] = (jnp.minimum(z, 0.0) - jnp.log1p(jnp.exp(-jnp.abs(z)))) * (1.0 / GLA_TAU)


def _proj_in(x, w, tm, kv_bufs=None, layer=None):
    n = x.shape[0]
    row = lambda width: pl.BlockSpec((tm, width), lambda i: (i, 0))
    full = lambda a: pl.BlockSpec(a.shape, lambda i: (0,) * a.ndim)
    args = [x, w["ln_mix"], w["wq"], w["wkv"], w["wg"], w["qg"], w["kg"] if kv_bufs is None else w["kg_t"],
            w["wa"], w["ba"]]
    in_specs = [row(D_MODEL)] + [full(a) for a in args[1:]]
    tail_specs = [row(G_WIDTH), row(MISC_WIDTH), row(GLA_QK_WIDTH)]
    tail_shapes = [jax.ShapeDtypeStruct((n, width), F32) for width in (G_WIDTH, MISC_WIDTH, GLA_QK_WIDTH)]
    q_shape = jax.ShapeDtypeStruct((n, Q_WIDTH), F32)
    if kv_bufs is None:
        kv_specs = [row(KV6_WIDTH)]
        kv_shapes = [jax.ShapeDtypeStruct((n, KV6_WIDTH), F32)]
        aliases = {}
    else:
        tiles = kv_bufs[0].shape[3] // tm
        kv_specs = [pl.BlockSpec((None, None, b.shape[2], tm), lambda i: (layer, i // tiles, 0, i % tiles)) for b in kv_bufs]
        kv_shapes = [jax.ShapeDtypeStruct(b.shape, b.dtype) for b in kv_bufs]
        aliases = {len(args): 1, len(args) + 1: 2}
        in_specs += [pl.BlockSpec(memory_space=pl.ANY)] * 2
        args += list(kv_bufs)
    return pl.pallas_call(
        functools.partial(_proj_in_kernel, kv_transposed=kv_bufs is not None),
        grid=(n // tm,),
        in_specs=in_specs,
        out_specs=[row(Q_WIDTH)] + kv_specs + tail_specs,
        out_shape=[q_shape] + kv_shapes + tail_shapes,
        input_output_aliases=aliases,
        compiler_params=pltpu.CompilerParams(dimension_semantics=("arbitrary",), vmem_limit_bytes=VMEM_LIMIT),
        name="proj_in",
    )(*args)


def _group_queries(q_tile, kvh):
    keep = _low_half() if kvh == 0 else ~_low_half()
    return jnp.concatenate([jnp.where(keep, q_tile[:, g * LANES:(g + 1) * LANES] * (SCALE * LOG2E), 0.0)
                            for g in range(NSA_GROUP)], axis=0).astype(BF16)


def _masked_neg_dist(dist, ok):
    return jnp.where(ok, -dist.astype(F32), NEG_INF)


def _softmax_rows(s_rows, neg_dist, kvh, tq):
    ps, inv = [], []
    for g in range(NSA_GROUP):
        s = s_rows[g * tq:(g + 1) * tq] + (SLOPES[kvh][g] * LOG2E) * neg_dist
        p = jnp.exp2(s - jnp.max(s, axis=-1, keepdims=True))
        inv.append(1.0 / jnp.sum(p, axis=-1, keepdims=True))
        ps.append(p)
    return jnp.concatenate(ps, axis=0).astype(BF16), jnp.concatenate(inv, axis=0)


N_SEL_SLOTS = 32
ALIBI_SLOT0 = N_SEL_SLOTS
MASK_BIG = 2.0 ** 100


def _bf16_terms(c):
    out = []
    for _ in range(3):
        t = float(np.float32(c).astype(BF16).astype(np.float32))
        out.append(t)
        c = c - t
    return out


ALIBI_TERMS = tuple(tuple(_bf16_terms(s * LOG2E) for s in row) for row in SLOPES)


def _slot_base(kvh):
    return HEAD_DIM if kvh == 0 else 0


def _augmented_keys(k_t, kvh, with_selection):
    n = k_t.shape[1]
    row = lax.broadcasted_iota(jnp.int32, (KV_WIDTH, 1), 0)
    kpos = lax.broadcasted_iota(jnp.int32, (1, n), 1)
    e = row - _slot_base(kvh)
    mine = (e < 0) | (e >= HEAD_DIM)
    k_hi = ((kpos >> 7) << 7).astype(F32)
    k_lo = (kpos & (LANES - 1)).astype(F32)
    extra = jnp.where((e >= ALIBI_SLOT0) & (e < ALIBI_SLOT0 + 3), k_hi,
                      jnp.where((e >= ALIBI_SLOT0 + 3) & (e < ALIBI_SLOT0 + 6), k_lo, 0.0))
    if with_selection:
        extra = jnp.where((e >= 0) & (e < N_SEL_SLOTS) & ((kpos >> int(math.log2(SEL_BLOCK))) == e), 1.0, extra)
    return jnp.where(mine, k_t, extra)


def _queries_t(slabs_t, kvh, alibi, sel_t):
    tq = slabs_t[0].shape[1]
    base = _slot_base(kvh)
    e = lax.broadcasted_iota(jnp.int32, (KV_WIDTH, 1), 0) - base
    own = (e < 0) | (e >= HEAD_DIM)
    if sel_t is None:
        sel_rows = None
    else:
        mask = (sel_t - 1.0) * MASK_BIG
        pieces = [jnp.zeros((base, tq), F32)] if base else []
        pieces += [mask, jnp.zeros((KV_WIDTH - base - mask.shape[0], tq), F32)]
        sel_rows = jnp.concatenate(pieces, axis=0)
    cols = []
    for g in range(NSA_GROUP):
        extra = jnp.zeros((KV_WIDTH, 1), F32)
        if alibi:
            for i, term in enumerate(ALIBI_TERMS[kvh][g] * 2):
                extra = jnp.where(e == ALIBI_SLOT0 + i, term, extra)
        if sel_rows is not None:
            extra = extra + sel_rows
        cols.append(jnp.where(own, slabs_t[g], extra))
    return jnp.concatenate(cols, axis=1).astype(BF16)


def _augmented_values(v_t, kvh):
    mine = (lax.broadcasted_iota(jnp.int32, (KV_WIDTH, 1), 0) < HEAD_DIM) == (kvh == 0)
    return jnp.where(mine, v_t, 1.0).astype(BF16)


def _attend_group_t(s_t, bias_t, v_aug, kvh, tq):
    ps = []
    for g in range(NSA_GROUP):
        s = s_t[:, g * tq:(g + 1) * tq] + bias_t
        ps.append(jnp.exp2((s - jnp.max(s, axis=0, keepdims=True)).astype(BF16)))
    o_t = _dot(v_aug, jnp.concatenate(ps, axis=1))
    denom_row = HEAD_DIM if kvh == 0 else 0
    return o_t, 1.0 / o_t[denom_row:denom_row + 1, :]


def _compressed_probs_t(s_t, qpos_row, kvh, tq, n_cmp, blk_of_row):
    dist = qpos_row - ((blk_of_row + 1) * CMP_BLOCK - 1)
    valid = (dist >= 0) & (blk_of_row < n_cmp)
    nd = _masked_neg_dist(dist, valid)
    ps = []
    imp = jnp.zeros(dist.shape, F32)
    for g in range(NSA_GROUP):
        s = s_t[:, g * tq:(g + 1) * tq] + (SLOPES[kvh][g] * LOG2E) * nd
        e = jnp.exp2(s - jnp.max(s, axis=0, keepdims=True))
        p = jnp.where(valid, e / jnp.sum(e, axis=0, keepdims=True), 0.0)
        imp = imp + p
        ps.append(p)
    return jnp.concatenate(ps, axis=1).astype(BF16), imp


def _split_order_block(w):
    slot = lax.broadcasted_iota(jnp.int32, (w, 1), 0)
    return jnp.where(slot < w // 2, 2 * slot, 2 * (slot - w // 2) + 1)


def _compressed_probs(s_rows, qpos, kvh, tq, n_cmp, blk_of_lane=None):
    w = s_rows.shape[1]
    lane = lax.broadcasted_iota(jnp.int32, (1, w), 1) if blk_of_lane is None else blk_of_lane
    dist = qpos - ((lane + 1) * CMP_BLOCK - 1)
    valid = (dist >= 0) & (lane < n_cmp)
    nd = _masked_neg_dist(dist, valid)
    ps = []
    imp = jnp.zeros((tq, w), F32)
    for g in range(NSA_GROUP):
        s = s_rows[g * tq:(g + 1) * tq] + (SLOPES[kvh][g] * LOG2E) * nd
        e = jnp.exp2(s - jnp.max(s, axis=-1, keepdims=True))
        p = jnp.where(valid, e / jnp.sum(e, axis=-1, keepdims=True), 0.0)
        imp = imp + p
        ps.append(p)
    return jnp.concatenate(ps, axis=0).astype(BF16), imp


def _select_blocks(imp, qpos, n_sel):
    tq, w = imp.shape
    lane = lax.broadcasted_iota(jnp.int32, (1, w), 1)
    pair = imp + pltpu.roll(imp, w - 1, 1)
    blk = lane >> 1
    cur = qpos >> int(math.log2(SEL_BLOCK))
    is_blk = ((lane & 1) == 0) & (blk < n_sel)
    forced = (blk == 0) | (blk == cur) | (blk == cur - 1)
    score = jnp.where(forced, FORCE_SCORE, jnp.where(blk <= cur, pair, NEG_INF))
    score = jnp.where(is_blk, score, BELOW_ALL)
    rank = jnp.zeros((tq, w), jnp.int32)
    for i in range(n_sel):
        col = score[:, 2 * i:2 * i + 1]
        beats = (col > score) | ((col == score) & (lane > 2 * i))
        rank = rank + jnp.where(beats, 1, 0)
    return jnp.where((rank < min(TOP_K, n_sel)) & is_blk, 1.0, 0.0)


def _select_blocks_t(imp_t, qpos_row, n_sel):
    w, tq = imp_t.shape
    nb = -(-n_sel // SUBLANES) * SUBLANES
    pair_t = imp_t[0:nb] + imp_t[w // 2:w // 2 + nb]
    blk = lax.broadcasted_iota(jnp.int32, (nb, 1), 0)
    cur = qpos_row >> int(math.log2(SEL_BLOCK))
    forced = (blk == 0) | (blk == cur) | (blk == cur - 1)
    score = jnp.where(forced, FORCE_SCORE, jnp.where(blk <= cur, pair_t, NEG_INF))
    score = jnp.where(blk < n_sel, score, BELOW_ALL)
    rank = jnp.zeros((nb, tq), jnp.int32)
    for i in range(n_sel):
        row = score[i:i + 1, :]
        beats = (row > score) | ((row == score) & (blk > i))
        rank = rank + jnp.where(beats, 1, 0)
    return jnp.where((rank < min(TOP_K, n_sel)) & (blk < n_sel), 1.0, 0.0)


def _gated_sum(gates, kvh, tq, branches):
    out = []
    for g in range(NSA_GROUP):
        h = kvh * NSA_GROUP + g
        rows = slice(g * tq, (g + 1) * tq)
        out.append(sum(gates[:, c * NSA_HEADS + h:c * NSA_HEADS + h + 1] * o[rows] for c, o in enumerate(branches)))
    return out


def _gated_sum_t(gates_t, kvh, branches):
    total = None
    for c, (o_t, scale) in enumerate(branches):
        first = c * NSA_HEADS + kvh * NSA_GROUP
        row = jnp.concatenate([gates_t[first + g:first + g + 1, :] for g in range(NSA_GROUP)], axis=1)
        term = o_t * (row if scale is None else row * scale)
        total = term if total is None else total + term
    return total


def _store_slabs(o_ref, per_group):
    lo = _low_half()
    for g in range(NSA_GROUP):
        o_ref[:, g * LANES:(g + 1) * LANES] = jnp.where(lo, per_group[0][g], per_group[1][g])


def _nsa_prompt_kernel(q_ref, kv_ref, kvw_ref, gate_ref, wc_ref, o_ref, kc_ref, vc_ref, ks_ref, kw_ref, vs_ref, vw_ref,
                       *, t_len, tq, key_step):
    qi = pl.program_id(1)
    n_cmp = t_len // CMP_BLOCK
    n_sel = -(-t_len // SEL_BLOCK)
    w = kc_ref.shape[0]
    stream = lambda s: slice(s * KV_WIDTH, (s + 1) * KV_WIDTH)

    @pl.when(qi == 0)
    def _():
        kc_ref[...] = _dot_f32(kv_ref[stream(0), :], wc_ref[0]).T.astype(BF16)
        vc_ref[...] = _dot_f32(kv_ref[stream(1), :], wc_ref[1]).astype(BF16)
        for kvh in range(NSA_KV_HEADS):
            ks_ref[kvh] = _augmented_keys(kv_ref[stream(2), :], kvh, True).T.astype(BF16)
            kw_ref[kvh] = _augmented_keys(kvw_ref[stream(0), :], kvh, False).T.astype(BF16)
            vs_ref[kvh] = _augmented_values(kv_ref[stream(3), :], kvh)
            vw_ref[kvh] = _augmented_values(kvw_ref[stream(1), :], kvh)

    q0 = qi * tq
    qpos_row = q0 + lax.broadcasted_iota(jnp.int32, (1, tq), 1)
    blk_of_row = _split_order_block(w)
    n_win = min(WINDOW + tq, t_len)
    w_start = pl.multiple_of(jnp.maximum(q0 + tq - n_win, 0), LANES)

    def body(n_keys):
        q_tile = q_ref[...]
        gates_t = gate_ref[...].T
        slabs_t = [(q_tile[:, g * LANES:(g + 1) * LANES] * (SCALE * LOG2E)).T for g in range(NSA_GROUP)]
        dist_s = qpos_row - lax.broadcasted_iota(jnp.int32, (n_keys, 1), 0)
        dist_w = qpos_row - (w_start + lax.broadcasted_iota(jnp.int32, (n_win, 1), 0))
        bias_s = jnp.where(dist_s >= 0, 0.0, NEG_INF)
        bias_w = jnp.where((dist_w >= 0) & (dist_w < WINDOW), 0.0, NEG_INF)
        groups = range(NSA_KV_HEADS)
        s_w = [_dot(kw_ref[kvh, pl.ds(w_start, n_win), :], _queries_t(slabs_t, kvh, True, None)) for kvh in groups]
        o_c, s_s = [], []
        for kvh in groups:
            s_c = _dot(kc_ref[...], _queries_t(slabs_t, kvh, False, None))
            p_c, imp_t = _compressed_probs_t(s_c, qpos_row, kvh, tq, n_cmp, blk_of_row)
            o_c.append(_dot(vc_ref[...], p_c))
            sel_t = _select_blocks_t(imp_t, qpos_row, n_sel)
            s_s.append(_dot(ks_ref[kvh, 0:n_keys, :], _queries_t(slabs_t, kvh, True, sel_t)))
        totals = []
        for kvh in groups:
            o_w, inv_w = _attend_group_t(s_w[kvh], bias_w, vw_ref[kvh, :, pl.ds(w_start, n_win)], kvh, tq)
            o_s, inv_s = _attend_group_t(s_s[kvh], bias_s, vs_ref[kvh, :, 0:n_keys], kvh, tq)
            totals.append(_gated_sum_t(gates_t, kvh, [(o_c[kvh], None), (o_s, inv_s), (o_w, inv_w)]))
        low_rows = lax.broadcasted_iota(jnp.int32, (KV_WIDTH, 1), 0) < HEAD_DIM
        for g in range(NSA_GROUP):
            cols = slice(g * tq, (g + 1) * tq)
            o_ref[:, g * LANES:(g + 1) * LANES] = jnp.where(low_rows, totals[0][:, cols], totals[1][:, cols]).T

    n_classes = -(-t_len // key_step)
    for c in range(n_classes):
        n_keys = min((c + 1) * key_step, t_len)

        @pl.when((q0 + tq - 1) // key_step == c)
        def _(n_keys=n_keys):
            body(n_keys)


NSA_Q_TILE = 128


def _nsa_prompt(qn, rows_t, win_t, gates, wc, layer, bsz, t_len):
    tq = math.gcd(t_len, NSA_Q_TILE)
    key_step = math.gcd(t_len, 256)
    w = wc.shape[2]
    assert tq % LANES == 0 and key_step % tq == 0 and -(-t_len // SEL_BLOCK) <= N_SEL_SLOTS
    kern = functools.partial(_nsa_prompt_kernel, t_len=t_len, tq=tq, key_step=key_step)
    return pl.pallas_call(
        kern,
        grid=(bsz, t_len // tq),
        in_specs=[
            pl.BlockSpec((None, tq, Q_WIDTH), lambda b, i: (b, i, 0)),
            pl.BlockSpec((None, None, 4 * KV_WIDTH, t_len), lambda b, i: (layer, b, 0, 0)),
            pl.BlockSpec((None, None, 2 * KV_WIDTH, t_len), lambda b, i: (layer, b, 0, 0)),
            pl.BlockSpec((None, tq, MISC_WIDTH), lambda b, i: (b, i, 0)),
            pl.BlockSpec(wc.shape, lambda b, i: (0, 0, 0)),
        ],
        out_specs=pl.BlockSpec((None, tq, Q_WIDTH), lambda b, i: (b, i, 0)),
        out_shape=jax.ShapeDtypeStruct((bsz, t_len, Q_WIDTH), F32),
        scratch_shapes=[pltpu.VMEM((w, KV_WIDTH), BF16), pltpu.VMEM((KV_WIDTH, w), BF16),
                        pltpu.VMEM((NSA_KV_HEADS, t_len, KV_WIDTH), BF16), pltpu.VMEM((NSA_KV_HEADS, t_len, KV_WIDTH), BF16),
                        pltpu.VMEM((NSA_KV_HEADS, KV_WIDTH, t_len), BF16), pltpu.VMEM((NSA_KV_HEADS, KV_WIDTH, t_len), BF16)],
        compiler_params=pltpu.CompilerParams(dimension_semantics=("arbitrary", "arbitrary"), vmem_limit_bytes=VMEM_LIMIT),
        name="nsa_prompt",
    )(qn.reshape(bsz, t_len, Q_WIDTH), rows_t, win_t, gates.reshape(bsz, t_len, MISC_WIDTH), wc)


def _pad_rows(a, n):
    return jnp.concatenate([a, jnp.zeros((n - a.shape[0], a.shape[1]), a.dtype)], axis=0)


def _nsa_decode_kernel(pt_ref, q_ref, kv_ref, gate_ref, win_ref, wpos_ref, cache_ref, o_ref, win_out_ref,
                       s_ref, p_ref, pages_ref, sem, *, ts, past_len, page_size, n_pages, w, layer):
    b = pl.program_id(0)
    slot = b % 2

    def page_copy(seq, j, to_slot):
        return pltpu.make_async_copy(cache_ref.at[layer, pt_ref[seq, j]], pages_ref.at[to_slot, j], sem.at[to_slot])

    def fetch(seq, to_slot):
        def start(j, carry):
            page_copy(seq, j, to_slot).start()
            return carry
        lax.fori_loop(0, n_pages, start, 0)

    @pl.when(b == 0)
    def _():
        fetch(0, 0)

    @pl.when(b + 1 < pl.num_programs(0))
    def _():
        fetch(b + 1, 1 - slot)

    def wait(j, carry):
        page_copy(b, j, slot).wait()
        return carry
    lax.fori_loop(0, n_pages, wait, 0)
    page_refs = [pages_ref.at[slot, j] for j in range(n_pages)]

    n_cmp = past_len // CMP_BLOCK
    n_sel = past_len // SEL_BLOCK + 1
    n_keys = past_len + page_size
    n_win = win_ref.shape[1]
    rows_q = NSA_GROUP * ts
    stream = lambda s: slice(s * KV_WIDTH, (s + 1) * KV_WIDTH)

    new = kv_ref[...]
    q_tile = q_ref[...]
    gates = gate_ref[...]
    qs = jnp.concatenate([_group_queries(q_tile, kvh) for kvh in range(NSA_KV_HEADS)], axis=0)
    qpos = past_len + lax.broadcasted_iota(jnp.int32, (ts, 1), 0)

    per_pair = 2 * page_size // CMP_BLOCK
    wk = jnp.concatenate([wpos_ref[0]] * per_pair, axis=0).reshape(per_pair, CMP_BLOCK, KV_WIDTH)
    wv = jnp.concatenate([wpos_ref[1]] * per_pair, axis=0).reshape(per_pair, CMP_BLOCK, KV_WIDTH)
    kcs, vcs = [], []
    for j in range(0, n_pages, 2):
        k2 = jnp.concatenate([page_refs[j][stream(0), :].T, page_refs[j + 1][stream(0), :].T], axis=0)
        v2 = jnp.concatenate([page_refs[j][stream(1), :].T, page_refs[j + 1][stream(1), :].T], axis=0)
        kcs.append(jnp.sum(k2.reshape(per_pair, CMP_BLOCK, KV_WIDTH) * wk, axis=1))
        vcs.append(jnp.sum(v2.reshape(per_pair, CMP_BLOCK, KV_WIDTH) * wv, axis=1))
        for jj in (j, j + 1):
            s_ref[:, jj * page_size:(jj + 1) * page_size] = _dot(qs, page_refs[jj][stream(2), :].astype(BF16))
    kc = _pad_rows(jnp.concatenate(kcs, axis=0), w).astype(BF16)
    vc = _pad_rows(jnp.concatenate(vcs, axis=0), w).astype(BF16)

    s_c = _dot_nt(qs, kc)
    p_cs, sels = [], []
    for kvh in range(NSA_KV_HEADS):
        p_c, imp = _compressed_probs(s_c[kvh * rows_q:(kvh + 1) * rows_q], qpos, kvh, ts, n_cmp)
        p_cs.append(p_c)
        sels.append(_select_blocks(imp, qpos, n_sel))
    o_c = _dot(jnp.concatenate(p_cs, axis=0), vc)

    new_k = _pad_rows(new[:, stream(2)], page_size).astype(BF16)
    new_v = _pad_rows(new[:, stream(3)], page_size).astype(BF16)
    s_ref[:, past_len:n_keys] = _dot_nt(qs, new_k)
    dist_s = qpos - lax.broadcasted_iota(jnp.int32, (1, n_keys), 1)
    blocks_per_page = page_size // SEL_BLOCK
    page_lane_blk = lax.broadcasted_iota(jnp.int32, (1, page_size), 1) >> int(math.log2(SEL_BLOCK))
    invs = []
    for kvh in range(NSA_KV_HEADS):
        sel = sels[kvh]
        pieces = []
        for j in range(n_pages + 1):
            piece = jnp.zeros((ts, page_size), F32)
            for r in range(blocks_per_page):
                blk = j * blocks_per_page + r
                if blk < n_sel:
                    piece = jnp.where(page_lane_blk == r, sel[:, 2 * blk:2 * blk + 1], piece)
            pieces.append(piece)
        keymask = jnp.concatenate(pieces, axis=1)
        nd_s = _masked_neg_dist(dist_s, (keymask > 0.5) & (dist_s >= 0))
        p_s, inv_s = _softmax_rows(s_ref[kvh * rows_q:(kvh + 1) * rows_q, :], nd_s, kvh, ts)
        p_ref[kvh * rows_q:(kvh + 1) * rows_q, :] = p_s
        invs.append(inv_s)
    o_s = _dot(p_ref[:, past_len:n_keys], new_v)
    for j in range(n_pages):
        o_s = o_s + _dot_nt(p_ref[:, j * page_size:(j + 1) * page_size], page_refs[j][stream(3), :].astype(BF16))
    o_s = o_s * jnp.concatenate(invs, axis=0)

    new_kw = _pad_rows(new[:, stream(4)], LANES).astype(BF16)
    new_vw = _pad_rows(new[:, stream(5)], LANES).astype(BF16)
    s_w = jnp.concatenate([_dot(qs, win_ref[0:KV_WIDTH, :].astype(BF16)), _dot_nt(qs, new_kw)], axis=1)
    wpos = jnp.concatenate([past_len - n_win + lax.broadcasted_iota(jnp.int32, (1, n_win), 1),
                            past_len + lax.broadcasted_iota(jnp.int32, (1, LANES), 1)], axis=1)
    dist_w = qpos - wpos
    nd_w = _masked_neg_dist(dist_w, (dist_w >= 0) & (dist_w < WINDOW))
    p_ws, inv_ws = [], []
    for kvh in range(NSA_KV_HEADS):
        p_w, inv_w = _softmax_rows(s_w[kvh * rows_q:(kvh + 1) * rows_q], nd_w, kvh, ts)
        p_ws.append(p_w)
        inv_ws.append(inv_w)
    p_w = jnp.concatenate(p_ws, axis=0)
    o_w = (_dot_nt(p_w[:, 0:n_win], win_ref[KV_WIDTH:2 * KV_WIDTH, :].astype(BF16)) + _dot(p_w[:, n_win:], new_vw))
    o_w = o_w * jnp.concatenate(inv_ws, axis=0)

    per_group = [_gated_sum(gates, kvh, ts, [o[kvh * rows_q:(kvh + 1) * rows_q] for o in (o_c, o_s, o_w)])
                 for kvh in range(NSA_KV_HEADS)]
    _store_slabs(o_ref, per_group)

    new_t = _pad_rows(new[:, 4 * KV_WIDTH:6 * KV_WIDTH], LANES).T
    new_t = pltpu.roll(new_t, LANES - ts, 1)
    shifted = pltpu.roll(win_ref[...], n_win - ts, 1)
    tail_lane = lax.broadcasted_iota(jnp.int32, (1, LANES), 1)
    win_out_ref[:, 0:n_win - LANES] = shifted[:, 0:n_win - LANES]
    win_out_ref[:, n_win - LANES:n_win] = jnp.where(tail_lane >= LANES - ts, new_t, shifted[:, n_win - LANES:n_win])


def _nsa_decode(qn, kvn, gates, win_t, cache_t, page_table, wpos_rows, layer, dbsz, ts):
    n_pages = page_table.shape[1]
    page_size = cache_t.shape[3]
    past_len = n_pages * page_size
    n_win = win_t.shape[3]
    assert ts < CMP_BLOCK and ts % SUBLANES == 0 and page_size == LANES and n_pages % 2 == 0
    assert n_win == WINDOW and past_len >= WINDOW
    n_cmp = past_len // CMP_BLOCK
    n_sel = past_len // SEL_BLOCK + 1
    w = -(-max(n_cmp, 2 * n_sel) // LANES) * LANES
    n_keys = past_len + page_size
    kern = functools.partial(_nsa_decode_kernel, ts=ts, past_len=past_len, page_size=page_size, n_pages=n_pages, w=w,
                             layer=layer)
    grid_spec = pltpu.PrefetchScalarGridSpec(
        num_scalar_prefetch=1,
        grid=(dbsz,),
        in_specs=[
            pl.BlockSpec((None, ts, Q_WIDTH), lambda b, pt: (b, 0, 0)),
            pl.BlockSpec((None, ts, KV6_WIDTH), lambda b, pt: (b, 0, 0)),
            pl.BlockSpec((None, ts, MISC_WIDTH), lambda b, pt: (b, 0, 0)),
            pl.BlockSpec((None, None, 2 * KV_WIDTH, n_win), lambda b, pt: (layer, b, 0, 0)),
            pl.BlockSpec(wpos_rows.shape, lambda b, pt: (0, 0, 0)),
            pl.BlockSpec(memory_space=pl.ANY),
        ],
        out_specs=[
            pl.BlockSpec((None, ts, Q_WIDTH), lambda b, pt: (b, 0, 0)),
            pl.BlockSpec((None, 2 * KV_WIDTH, n_win), lambda b, pt: (b, 0, 0)),
        ],
        scratch_shapes=[pltpu.VMEM((NSA_HEADS * ts, n_keys), F32), pltpu.VMEM((NSA_HEADS * ts, n_keys), BF16),
                        pltpu.VMEM((2, n_pages, 4 * KV_WIDTH, page_size), F32),
                        pltpu.SemaphoreType.DMA((2,))],
    )
    return pl.pallas_call(
        kern,
        grid_spec=grid_spec,
        out_shape=[jax.ShapeDtypeStruct((dbsz, ts, Q_WIDTH), F32), jax.ShapeDtypeStruct((dbsz, 2 * KV_WIDTH, n_win), F32)],
        compiler_params=pltpu.CompilerParams(dimension_semantics=("arbitrary",), vmem_limit_bytes=VMEM_LIMIT),
        name="nsa_decode",
    )(page_table, qn.reshape(dbsz, ts, Q_WIDTH), kvn.reshape(dbsz, ts, KV6_WIDTH), gates.reshape(dbsz, ts, MISC_WIDTH),
      win_t, wpos_rows, cache_t)


def _gla_kernel(*refs, t_len, chunk, has_s0, nb):
    for i in range(nb):
        _gla_one([r if j == 5 else r.at[i] for j, r in enumerate(refs)], t_len, chunk, has_s0)


def _gla_one(refs, t_len, chunk, has_s0):
    if has_s0:
        q_ref, k_ref, v_ref, gg_ref, la_ref, gn_ref, s0_ref, o_ref, s_out_ref, st_ref, u_ref, d_ref, sb_ref, qe_ref = refs
    else:
        q_ref, k_ref, v_ref, gg_ref, la_ref, gn_ref, o_ref, s_out_ref, st_ref, u_ref, d_ref, sb_ref, qe_ref = refs
    dk2, dv2 = 2 * GLA_DK, 2 * GLA_DV
    n_chunks = t_len // chunk
    per_group = math.gcd(n_chunks, 4)
    rows_g = per_group * chunk
    n_groups = n_chunks // per_group
    rp = max(rows_g, LANES)
    pad = rp - rows_g

    rr = lax.broadcasted_iota(jnp.int32, (dv2, dk2), 0) // GLA_DV
    cc = lax.broadcasted_iota(jnp.int32, (dv2, dk2), 1) // GLA_DK
    diag = rr == cc
    if has_s0:
        z = jnp.zeros((GLA_DK, GLA_DV), F32)
        s_full = jnp.concatenate([jnp.concatenate([s0_ref[0], z], axis=1), jnp.concatenate([z, s0_ref[1]], axis=1)], axis=0)
        st_ref[...] = s_full.T
    else:
        st_ref[...] = jnp.zeros((dv2, dk2), F32)

    lane_head = lax.broadcasted_iota(jnp.int32, (1, dk2), 1) // GLA_DK
    trow = lax.broadcasted_iota(jnp.int32, (rows_g, rp), 0)
    tcol = lax.broadcasted_iota(jnp.int32, (rows_g, rp), 1)
    causal = (tcol <= trow) & (tcol // chunk == trow // chunk)
    tril = jnp.where(causal, 1.0, 0.0).astype(BF16)
    prow_chunk = lax.broadcasted_iota(jnp.int32, (rp, 1), 0) // chunk
    mid = chunk // 2
    gn = gn_ref[...]

    def pad_rows(a):
        return a if pad == 0 else jnp.concatenate([a, jnp.zeros((pad, a.shape[1]), a.dtype)], axis=0)

    def aligned(start, multiple):
        return start if isinstance(start, int) else pl.multiple_of(start, multiple)

    def per_chunk_row(cum, r):
        return jnp.concatenate([jnp.broadcast_to(cum[c * chunk + r:c * chunk + r + 1, :], (chunk, dk2))
                                for c in range(per_group)], axis=0)

    def group_local(gi, carry):
        r0 = aligned(gi * rows_g, rows_g)
        rows = pl.ds(r0, rows_g)
        q = q_ref[rows, :] * (GLA_DK ** -0.5)
        k = k_ref[rows, :]
        la0, la1, _ = _split3(pad_rows(la_ref[rows, :]))
        cum = _dot(tril, la0) + _dot(tril, la1)
        m = per_chunk_row(cum, mid)
        last = per_chunk_row(cum, chunk - 1)
        qe_ref[rows, :] = q * jnp.exp(cum)
        qs = q * jnp.exp(cum - m)
        ks = pad_rows((k * jnp.exp(m - cum)).astype(BF16))
        vp = pad_rows(v_ref[rows, :].astype(BF16))
        intra = []
        for h in range(2):
            a = _dot_nt(jnp.where(lane_head == h, qs, 0.0).astype(BF16), ks)
            a = jnp.where(causal, a, 0.0).astype(BF16)
            intra.append(_dot(a, vp[:, h * GLA_DV:(h + 1) * GLA_DV]))
        o_ref[rows, :] = jnp.concatenate(intra, axis=1)
        kd = pad_rows(k * jnp.exp(last - cum))
        kd_by_chunk = jnp.concatenate([jnp.where(prow_chunk == c, kd, 0.0) for c in range(per_group)], axis=1).astype(BF16)
        u_all = _dot_tn(vp, kd_by_chunk)
        for c in range(per_group):
            ci = gi * per_group + c
            u_ref[ci] = jnp.where(diag, u_all[:, c * dk2:(c + 1) * dk2], 0.0)
            d_ref[ci] = jnp.exp(last[c * chunk:c * chunk + SUBLANES, :])
        return carry

    def chunk_state(ci, carry):
        st = st_ref[...]
        sb_ref[ci] = st.astype(BF16)
        st_ref[...] = st * d_ref[ci][0:1, :] + u_ref[ci]
        return carry

    def group_output(gi, carry):
        r0 = aligned(gi * rows_g, rows_g)
        rows = pl.ds(r0, rows_g)
        inter = [_dot_nt(qe_ref[pl.ds(aligned(r0 + c * chunk, chunk), chunk), :].astype(BF16),
                         sb_ref[gi * per_group + c]) for c in range(per_group)]
        o = o_ref[rows, :] + jnp.concatenate(inter, axis=0)
        gg = gg_ref[rows, :]
        outs = []
        for h in range(2):
            oh = o[:, h * GLA_DV:(h + 1) * GLA_DV]
            y = oh * lax.rsqrt(jnp.mean(oh * oh, axis=-1, keepdims=True) + EPS) * gn
            gh = gg[:, h * GLA_DV:(h + 1) * GLA_DV]
            outs.append(y * (gh * _sigmoid(gh)))
        o_ref[rows, :] = jnp.concatenate(outs, axis=1)
        return carry

    def loop(n, body, unroll=1):
        if n == 1:
            body(0, 0)
        else:
            lax.fori_loop(0, n, body, 0, unroll=unroll if n % unroll == 0 else 1)

    loop(n_groups, group_local, 2)
    loop(n_chunks, chunk_state)
    loop(n_groups, group_output, 2)
    s_fin = st_ref[...].T
    s_out_ref[0] = s_fin[0:GLA_DK, 0:GLA_DV]
    s_out_ref[1] = s_fin[GLA_DK:dk2, GLA_DV:dv2]


def _gla(g, la, gnorm, s0, bsz, t_len):
    chunk = math.gcd(t_len, GLA_CHUNK)
    n_chunks = t_len // chunk
    nb = math.gcd(bsz, 16) if n_chunks == 1 else 1
    has_s0 = s0 is not None
    kern = functools.partial(_gla_kernel, t_len=t_len, chunk=chunk, has_s0=has_s0, nb=nb)
    qk_blk = lambda off: pl.BlockSpec((nb, t_len, 2 * GLA_DK), lambda b, p: (b, 0, off + p))
    v_blk = lambda off: pl.BlockSpec((nb, t_len, 2 * GLA_DV), lambda b, p: (b, 0, off + p))
    g3 = g.reshape(bsz, t_len, G_WIDTH)
    n_qk = GLA_QK_WIDTH // (2 * GLA_DK)
    in_specs = [qk_blk(0), qk_blk(n_qk), v_blk(n_qk), v_blk(n_qk + GLA_V_WIDTH // (2 * GLA_DV)),
                pl.BlockSpec((nb, t_len, 2 * GLA_DK), lambda b, p: (b, 0, p)),
                pl.BlockSpec(gnorm.shape, lambda b, p: (0, 0))]
    args = [g3, g3, g3, g3, la.reshape(bsz, t_len, GLA_QK_WIDTH), gnorm]
    if has_s0:
        in_specs.append(pl.BlockSpec((nb, 2, GLA_DK, GLA_DV), lambda b, p: (b, p, 0, 0)))
        args.append(s0)
    return pl.pallas_call(
        kern,
        grid=(bsz // nb, GLA_HEADS // 2),
        in_specs=in_specs,
        out_specs=[pl.BlockSpec((nb, t_len, 2 * GLA_DV), lambda b, p: (b, 0, p)),
                   pl.BlockSpec((nb, 2, GLA_DK, GLA_DV), lambda b, p: (b, p, 0, 0))],
        out_shape=[jax.ShapeDtypeStruct((bsz, t_len, GLA_V_WIDTH), F32),
                   jax.ShapeDtypeStruct((bsz, GLA_HEADS, GLA_DK, GLA_DV), F32)],
        scratch_shapes=[pltpu.VMEM((nb, 2 * GLA_DV, 2 * GLA_DK), F32),
                        pltpu.VMEM((nb, n_chunks, 2 * GLA_DV, 2 * GLA_DK), F32),
                        pltpu.VMEM((nb, n_chunks, SUBLANES, 2 * GLA_DK), F32),
                        pltpu.VMEM((nb, n_chunks, 2 * GLA_DV, 2 * GLA_DK), BF16),
                        pltpu.VMEM((nb, t_len, 2 * GLA_DK), F32)],
        compiler_params=pltpu.CompilerParams(dimension_semantics=("arbitrary", "arbitrary"), vmem_limit_bytes=VMEM_LIMIT),
        name="gla",
    )(*args)


FFN_ROWS = 512
FF_STEPS = 2
FF_STEPS_ONE_TILE = 11
assert D_FF % (FF_STEPS * LANES) == 0 and D_FF % (FF_STEPS_ONE_TILE * LANES) == 0


def _out_ffn_kernel(x_ref, on_ref, og_ref, wo_n_ref, wo_g_ref, ln_ref, wg_ref, wu_ref, wd_ref, y_ref, h_ref):
    j = pl.program_id(1)

    @pl.when(j == 0)
    def _():
        x1 = x_ref[...] + _dot(on_ref[...].astype(BF16), wo_n_ref[...]) + _dot(og_ref[...].astype(BF16), wo_g_ref[...])
        h_ref[...] = (x1 * lax.rsqrt(jnp.mean(x1 * x1, axis=-1, keepdims=True) + EPS) * ln_ref[...]).astype(BF16)
        y_ref[...] = x1

    h = h_ref[...]
    gate = _dot(h, wg_ref[...])
    up = _dot(h, wu_ref[...])
    y_ref[...] += _dot((gate * _sigmoid(gate) * up).astype(BF16), wd_ref[...])


def _out_ffn(x, o_nsa, o_gla, wo_n, wo_g, ln, wg, wu, wd, tm):
    n = x.shape[0]
    ff_steps = FF_STEPS if n // tm > 1 else FF_STEPS_ONE_TILE
    ff_chunk = D_FF // ff_steps
    row = lambda w: pl.BlockSpec((tm, w), lambda i, j: (i, 0))
    full = lambda a: pl.BlockSpec(a.shape, lambda i, j: (0,) * a.ndim)
    return pl.pallas_call(
        _out_ffn_kernel,
        grid=(n // tm, ff_steps),
        in_specs=[row(D_MODEL), row(Q_WIDTH), row(GLA_V_WIDTH), full(wo_n), full(wo_g), full(ln),
                  pl.BlockSpec((D_MODEL, ff_chunk), lambda i, j: (0, j)),
                  pl.BlockSpec((D_MODEL, ff_chunk), lambda i, j: (0, j)),
                  pl.BlockSpec((ff_chunk, D_MODEL), lambda i, j: (j, 0))],
        out_specs=row(D_MODEL),
        out_shape=jax.ShapeDtypeStruct((n, D_MODEL), F32),
        scratch_shapes=[pltpu.VMEM((tm, D_MODEL), BF16)],
        compiler_params=pltpu.CompilerParams(dimension_semantics=("arbitrary", "arbitrary"), vmem_limit_bytes=VMEM_LIMIT),
        name="out_ffn",
    )(x, o_nsa, o_gla, wo_n, wo_g, ln, wg, wu, wd)


def _layer_weights(ln_mix, w_in, q_norm, k_norm, cmp_pos_w, w_a2, b_a, gla_norm, w_out, ln_ffn, w_gate, w_up, w_down,
                   tm_prompt, t_len):
    w_in_t = w_in.T
    o = 0
    parts = []
    for width in (Q_WIDTH, KV6_WIDTH, N_GATES, GLA_QK_WIDTH, GLA_QK_WIDTH, GLA_V_WIDTH, GLA_V_WIDTH, GLA_RANK):
        parts.append(w_in_t[o:o + width])
        o += width
    wq, wkv, wgate, wgq, wgk, wgv, wgg, wga = parts
    wq = jnp.concatenate([wq[h * HEAD_DIM:(h + 1) * HEAD_DIM] for h in SLAB_HEADS], axis=0)
    pad = jnp.zeros((MISC_WIDTH - N_GATES - GLA_RANK, D_MODEL), w_in.dtype)
    wa = jnp.zeros((MISC_WIDTH, GLA_QK_WIDTH), F32).at[N_GATES:N_GATES + GLA_RANK].set(w_a2)
    n_cmp = t_len // CMP_BLOCK
    w_lanes = -(-max(n_cmp, 2 * (-(-t_len // SEL_BLOCK))) // LANES) * LANES
    tok = jnp.arange(t_len)
    lane = jnp.arange(w_lanes)
    blk_of_lane = jnp.where(lane < w_lanes // 2, 2 * lane, 2 * (lane - w_lanes // 2) + 1)
    in_blk = (tok[:, None] // CMP_BLOCK == blk_of_lane[None, :]) & (tok[:, None] < n_cmp * CMP_BLOCK)
    wc = jnp.where(in_blk[None], cmp_pos_w[:, tok % CMP_BLOCK][:, :, None], 0.0)
    kg_rows = jnp.tile(k_norm, (1, NSA_KV_HEADS))
    return dict(
        ln_mix=ln_mix.reshape(1, D_MODEL),
        wq=wq.astype(BF16), wkv=wkv.astype(BF16),
        wg=jnp.concatenate([wgq, wgk, wgv, wgg, wgate, wga, pad], axis=0).astype(BF16),
        qg=jnp.tile(q_norm, LANES // HEAD_DIM).reshape(1, LANES),
        kg=kg_rows, kg_t=jnp.broadcast_to(kg_rows[:, :, None], (3, KV_WIDTH, tm_prompt)),
        wa=wa, ba=b_a.reshape(1, GLA_QK_WIDTH),
        wpos=jnp.broadcast_to(cmp_pos_w[:, :, None], (2, CMP_BLOCK, LANES)), wc=wc,
        gnorm=gla_norm.reshape(1, GLA_DV),
        wo_n=jnp.concatenate([w_out[h * HEAD_DIM:(h + 1) * HEAD_DIM] for h in SLAB_HEADS], axis=0).astype(BF16),
        wo_g=w_out[Q_WIDTH:].astype(BF16),
        ln_ffn=ln_ffn.reshape(1, D_MODEL),
        w_gate=w_gate.astype(BF16), w_up=w_up.astype(BF16), w_down=w_down.astype(BF16),
    )


def _row_tile(n):
    return math.gcd(n, 512)


def _mix_and_ffn(x, w, attn, s0, bsz, t_len, kv_bufs=None, layer=None):
    tm = _row_tile(x.shape[0])
    qn, *kv, g, gates, la = _proj_in(x, w, tm, kv_bufs, layer)
    o_nsa, extra = attn(qn, kv, gates)
    o_gla, s_new = _gla(g, la, w["gnorm"], s0, bsz, t_len)
    y = _out_ffn(x, o_nsa.reshape(-1, Q_WIDTH), o_gla.reshape(-1, GLA_V_WIDTH), w["wo_n"], w["wo_g"], w["ln_ffn"],
                 w["w_gate"], w["w_up"], w["w_down"], math.gcd(x.shape[0], FFN_ROWS))
    return y, kv, s_new, extra


def _token_major(a_t, lead):
    n_lead = len(lead)
    a = a_t.reshape(*lead, -1, NSA_KV_HEADS, HEAD_DIM, a_t.shape[-1])
    return a.transpose(*range(n_lead), n_lead + 3, n_lead, n_lead + 1, n_lead + 2)


def kernel(x_prompt, x_sample, cache_nsa_kv, state_nsa_win, state_gla, page_table, ln_mix, w_in, q_norm, k_norm, cmp_pos_w,
           w_a2, b_a, gla_norm, w_out, ln_ffn, w_gate, w_up, w_down):
    bsz, t_len = x_prompt.shape[:2]
    dbsz, ts = x_sample.shape[:2]
    depth = w_in.shape[0]
    n_phys, page_size = cache_nsa_kv.shape[1:3]
    n_win = state_nsa_win.shape[2]
    cache_t = cache_nsa_kv.transpose(0, 1, 3, 4, 5, 2).reshape(depth, n_phys, 4 * KV_WIDTH, page_size)
    win_t = state_nsa_win.transpose(0, 1, 3, 4, 5, 2).reshape(depth, dbsz, 2 * KV_WIDTH, n_win)
    keep_p = min(WINDOW, t_len)
    tm_prompt = _row_tile(bsz * t_len)
    assert t_len % tm_prompt == 0

    yp = x_prompt.reshape(bsz * t_len, D_MODEL)
    ys = x_sample.reshape(dbsz * ts, D_MODEL)
    kv_bufs = [jnp.zeros((depth, bsz, 4 * KV_WIDTH, t_len), F32), jnp.zeros((depth, bsz, 2 * KV_WIDTH, t_len), F32)]
    gla_p, rows_s, win_s, gla_s = [], [], [], []
    for l in range(depth):
        w = _layer_weights(ln_mix[l], w_in[l], q_norm[l], k_norm[l], cmp_pos_w[l], w_a2[l], b_a[l], gla_norm[l], w_out[l],
                           ln_ffn[l], w_gate[l], w_up[l], w_down[l], tm_prompt, t_len)

        def attn_prompt(qn, kv, gates):
            return _nsa_prompt(qn, kv[0], kv[1], gates, w["wc"], l, bsz, t_len), None

        def attn_sample(qn, kv, gates):
            return _nsa_decode(qn, kv[0], gates, win_t, cache_t, page_table, w["wpos"], l, dbsz, ts)

        yp, kv_bufs, st_p, _ = _mix_and_ffn(yp, w, attn_prompt, None, bsz, t_len, kv_bufs, l)
        ys, (kvn_s,), st_s, nw_s = _mix_and_ffn(ys, w, attn_sample, state_gla[l], dbsz, ts)

        gla_p.append(st_p.astype(state_gla.dtype))
        rows_s.append(kvn_s.reshape(dbsz, ts, 6, NSA_KV_HEADS, HEAD_DIM)[:, :, :4])
        win_s.append(nw_s)
        gla_s.append(st_s.astype(state_gla.dtype))
    lead_p, lead_s = (depth, bsz), (depth, dbsz)
    return (yp.reshape(bsz, t_len, D_MODEL), ys.reshape(dbsz, ts, D_MODEL),
            _token_major(kv_bufs[0], lead_p), _token_major(kv_bufs[1][:, :, :, t_len - keep_p:], lead_p), jnp.stack(gla_p),
            jnp.stack(rows_s), _token_major(jnp.stack(win_s), lead_s), jnp.stack(gla_s))
```
